```python
import math
import jax, jax.numpy as jnp
from jax import lax
import numpy as np

D_MODEL = 1024
BATCH = 2
SEQ = 8192
DEPTH = 2
DEC_BATCH = 32
DEC_SEQ = 8
PAST_LEN = 8192
PAGE_SIZE = 128

N_EVEN = (DEPTH + 1) // 2
N_ODD = DEPTH // 2
HEAD_DIM = 64
SSM_HEADS = D_MODEL // 128
SSM_HEAD_DIM = 64
D_INNER = SSM_HEADS * SSM_HEAD_DIM
SSM_GROUPS = 2
SSM_STATE = 128
CONV_W = 4
CONV_DIM = D_INNER + 2 * SSM_GROUPS * SSM_STATE
SSD_CHUNK = 128
SB_HEADS = D_MODEL // 128
SB_W = SB_HEADS * HEAD_DIM
SB_BIAS_INIT = -6.0
IN_A = D_INNER + CONV_DIM + SSM_HEADS + 3 * SB_W
C_GROUPS = ((128, 1), (512, 4), (2048, 16))
C_SLOTS = D_MODEL // 128
C_W = len(C_GROUPS) * C_SLOTS * HEAD_DIM
ROT_DIM = HEAD_DIM // 4
ROPE_THETA = 500000.0
Q_BLOCK = 128
D_FF = 256 * ((8 * D_MODEL // 3 + 255) // 256)
N_EXPERTS = 8
TOP_K = 2
D_FF_EXPERT = 7 * D_MODEL // 2
DN_ALPHA = (2 * DEPTH) ** 0.25
DN_BETA = (8 * DEPTH) ** -0.25
LN_EPS = 1e-5
RMS_EPS = 1e-6

kernel_name = 'hybrid_ssd_stickbreak_dilated_moe_step'


def layer_norm(x, g, b):
    xf = x.astype(jnp.float32)
    mu = jnp.mean(xf, axis=-1, keepdims=True)
    var = jnp.mean(jnp.square(xf - mu), axis=-1, keepdims=True)
    return ((xf - mu) * lax.rsqrt(var + LN_EPS) * g + b).astype(x.dtype)


def swiglu(x, w1, w3, w2):
    return (jax.nn.silu(x @ w1) * (x @ w3)) @ w2


def rope_partial(x, pos):
    half = ROT_DIM // 2
    inv = ROPE_THETA ** (-jnp.arange(0, ROT_DIM, 2, dtype=jnp.float32) / ROT_DIM)
    ang = pos.astype(jnp.float32)[:, None] * inv[None, :]
    shape = (1, pos.shape[0]) + (1,) * (x.ndim - 3) + (half,)
    cos = jnp.cos(ang).reshape(shape)
    sin = jnp.sin(ang).reshape(shape)
    xf = x.astype(jnp.float32)
    x1, x2 = xf[..., :half], xf[..., half:ROT_DIM]
    return jnp.concatenate([x1 * cos - x2 * sin, x2 * cos + x1 * sin, xf[..., ROT_DIM:]], axis=-1).astype(x.dtype)


def segsum(a):
    t = a.shape[-1]
    rep = jnp.broadcast_to(a[..., :, None], a.shape + (t,))
    strict = jnp.tril(jnp.ones((t, t), dtype=bool), -1)
    s = jnp.cumsum(jnp.where(strict, rep, 0.0), axis=-2)
    return jnp.where(jnp.tril(jnp.ones((t, t), dtype=bool)), s, -jnp.inf)


def ssd_scan(x, dt, a_neg, bm, cm, h0):
    b, L, H, P = x.shape
    G, N = bm.shape[2], bm.shape[3]
    R = H // G
    q = math.gcd(L, SSD_CHUNK)
    c = L // q
    xs = (x * dt[..., None]).reshape(b, c, q, G, R, P)
    a = (dt * a_neg).reshape(b, c, q, G, R).transpose(0, 3, 4, 1, 2)
    bc = bm.reshape(b, c, q, G, N)
    cc = cm.reshape(b, c, q, G, N)
    a_cum = jnp.cumsum(a, axis=-1)
    scores = jnp.einsum('bclgn,bcsgn->bgcls', cc, bc)
    w_diag = scores[:, :, None] * jnp.exp(segsum(a))
    y_diag = jnp.einsum('bgrcls,bcsgrp->bclgrp', w_diag, xs)
    decay_states = jnp.exp(a_cum[..., -1:] - a_cum)
    states = jnp.einsum('bcsgn,bgrcs,bcsgrp->bcgrpn', bc, decay_states, xs)
    states = jnp.concatenate([h0.reshape(b, 1, G, R, P, N), states], axis=1)
    chunk_a = jnp.pad(a_cum[..., -1], ((0, 0), (0, 0), (0, 0), (1, 0)))
    new_states = jnp.einsum('bgrzc,bcgrpn->bzgrpn', jnp.exp(segsum(chunk_a)), states)
    y_off = jnp.einsum('bclgn,bcgrpn,bgrcl->bclgrp', cc, new_states[:, :-1], jnp.exp(a_cum))
    y = (y_diag + y_off).reshape(b, L, H, P)
    return y, new_states[:, -1].reshape(b, H, P, N)


def causal_dwconv(xbc, conv_state, w, bias):
    L = xbc.shape[1]
    xpad = jnp.concatenate([conv_state.astype(xbc.dtype), xbc], axis=1)
    out = bias + xpad[:, 0:L] * w[0]
    for j in range(1, CONV_W):
        out = out + xpad[:, j:j + L] * w[j]
    return out, xpad[:, L:]


def stick_breaking(q, k, v, q_pos, sb_bias):
    z = jnp.einsum('bqhd,bkhd->bhqk', q, k).astype(jnp.float32) * (HEAD_DIM ** -0.5)
    z = z + sb_bias.astype(jnp.float32)[None, :, None, None]
    mask = jnp.arange(k.shape[1])[None, :] < q_pos[:, None]
    log_fail = jnp.where(mask, jax.nn.log_sigmoid(-z), 0.0)
    tail = lax.cumsum(log_fail, axis=3, reverse=True) - log_fail
    w = jnp.where(mask, jnp.exp(jax.nn.log_sigmoid(z) + tail), 0.0)
    return jnp.einsum('bhqk,bkhd->bqhd', w.astype(v.dtype), v)


def sb_prompt(q, k, v, sb_bias):
    b, L = q.shape[0], q.shape[1]

    def blk(i):
        qb = lax.dynamic_slice_in_dim(q, i * Q_BLOCK, Q_BLOCK, axis=1)
        return stick_breaking(qb, k, v, i * Q_BLOCK + jnp.arange(Q_BLOCK), sb_bias)

    out = lax.map(blk, jnp.arange(L // Q_BLOCK))
    return jnp.moveaxis(out, 0, 1).reshape(b, L, SB_HEADS, HEAD_DIM)


def mixer_ab(x, conv_state, ssm_state, k_past, v_past, w_in, conv_w, conv_b, dt_bias, a_log, d_skip, norm_w, sb_bias, w_out):
    b, L, _ = x.shape
    cuts = [int(c) for c in np.cumsum([D_INNER, CONV_DIM, SSM_HEADS, SB_W, SB_W])]
    z, xbc, dt_raw, q, k, v = jnp.split(x @ w_in, cuts, axis=-1)
    xbc, new_conv = causal_dwconv(xbc, conv_state, conv_w, conv_b)
    xbc = jax.nn.silu(xbc).astype(jnp.float32)
    gn = SSM_GROUPS * SSM_STATE
    xs, bm, cm = jnp.split(xbc, [D_INNER, D_INNER + gn], axis=-1)
    xs = xs.reshape(b, L, SSM_HEADS, SSM_HEAD_DIM)
    dt = jax.nn.softplus(dt_raw.astype(jnp.float32) + dt_bias)
    a_neg = -jnp.exp(a_log.astype(jnp.float32))
    y, new_ssm = ssd_scan(xs, dt, a_neg, bm.reshape(b, L, SSM_GROUPS, SSM_STATE),
                          cm.reshape(b, L, SSM_GROUPS, SSM_STATE), ssm_state.astype(jnp.float32))
    y = y + d_skip.astype(jnp.float32)[:, None] * xs
    gsz = D_INNER // SSM_GROUPS
    y = y.reshape(b, L, SSM_GROUPS, gsz) * jax.nn.silu(z.astype(jnp.float32)).reshape(b, L, SSM_GROUPS, gsz)
    y = y * lax.rsqrt(jnp.mean(jnp.square(y), axis=-1, keepdims=True) + RMS_EPS)
    y_a = (y.reshape(b, L, D_INNER) * norm_w).astype(x.dtype)
    q = q.reshape(b, L, SB_HEADS, HEAD_DIM)
    k = k.reshape(b, L, SB_HEADS, HEAD_DIM)
    v = v.reshape(b, L, SB_HEADS, HEAD_DIM)
    if k_past is None:
        o = sb_prompt(q, k, v, sb_bias)
    else:
        kk = jnp.concatenate([k_past, k.astype(k_past.dtype)], axis=1)
        vv = jnp.concatenate([v_past, v.astype(v_past.dtype)], axis=1)
        o = stick_breaking(q, kk, vv, k_past.shape[1] + jnp.arange(L), sb_bias)
    y = jnp.concatenate([y_a, o.reshape(b, L, SB_W).astype(x.dtype)], axis=-1) @ w_out
    return y, new_conv, new_ssm.astype(ssm_state.dtype), k, v


def dilated_group(q, kseq, vseq, q_idx, dil, n_back):
    idx = q_idx[:, None] - dil * jnp.arange(n_back + 1)[None, :]
    valid = idx >= 0
    idx = jnp.maximum(idx, 0)
    kg = kseq[:, idx]
    vg = vseq[:, idx]
    s = jnp.einsum('bqhd,bqmhd->bqhm', q, kg).astype(jnp.float32) * (HEAD_DIM ** -0.5)
    s = jnp.where(valid[None, :, None, :], s, -jnp.inf)
    lse = jax.nn.logsumexp(s, axis=-1)
    p = jnp.exp(s - lse[..., None])
    return jnp.einsum('bqhm,bqmhd->bqhd', p, vg.astype(jnp.float32)), lse


def dilated_mix(q, ks, vs, q_idxs):
    outs, lses = [], []
    for g, (win, dil) in enumerate(C_GROUPS):
        o, lse = dilated_group(q[:, :, g], ks[g], vs[g], q_idxs[g], dil, win // dil)
        outs.append(o)
        lses.append(lse)
    wts = jax.nn.softmax(jnp.stack(lses), axis=0)
    return jnp.einsum('gbqh,gbqhd->bqhd', wts, jnp.stack(outs))


def mixer_c(x, bufs_k, bufs_v, start, w_in, w_out):
    b, L, _ = x.shape
    n_g = len(C_GROUPS)
    shape = (b, L, n_g, C_SLOTS, HEAD_DIM)
    q, k, v = jnp.split(x @ w_in, 3, axis=-1)
    pos = start + jnp.arange(L)
    q = rope_partial(q.reshape(shape), pos)
    k = rope_partial(k.reshape(shape), pos)
    v = v.reshape(shape)
    if bufs_k is None:
        ks = [k[:, :, g] for g in range(n_g)]
        vs = [v[:, :, g] for g in range(n_g)]

        def blk(i):
            qb = lax.dynamic_slice_in_dim(q, i * Q_BLOCK, Q_BLOCK, axis=1)
            idx = i * Q_BLOCK + jnp.arange(Q_BLOCK)
            return dilated_mix(qb, ks, vs, [idx] * n_g)

        o = jnp.moveaxis(lax.map(blk, jnp.arange(L // Q_BLOCK)), 0, 1).reshape(b, L, C_SLOTS, HEAD_DIM)
        new_k = [kg[:, L - min(w, L):] for kg, (w, _) in zip(ks, C_GROUPS)]
        new_v = [vg[:, L - min(w, L):] for vg, (w, _) in zip(vs, C_GROUPS)]
    else:
        ks = [jnp.concatenate([bk, k[:, :, g].astype(bk.dtype)], axis=1) for g, bk in enumerate(bufs_k)]
        vs = [jnp.concatenate([bv, v[:, :, g].astype(bv.dtype)], axis=1) for g, bv in enumerate(bufs_v)]
        idxs = [bk.shape[1] + jnp.arange(L) for bk in bufs_k]
        o = dilated_mix(q, ks, vs, idxs)
        new_k = [kg[:, kg.shape[1] - bk.shape[1]:] for kg, bk in zip(ks, bufs_k)]
        new_v = [vg[:, vg.shape[1] - bv.shape[1]:] for vg, bv in zip(vs, bufs_v)]
    y = o.reshape(b, L, C_SLOTS * HEAD_DIM).astype(x.dtype) @ w_out
    return y, new_k, new_v


def moe_swiglu(x, w_router, w1, w3, w2):
    b, L, d = x.shape
    t = x.reshape(b * L, d)
    logits = (t @ w_router).astype(jnp.float32)
    top_v, top_i = lax.top_k(logits, TOP_K)
    gates = jnp.einsum('tk,tke->te', jax.nn.softmax(top_v, axis=-1),
                       jax.nn.one_hot(top_i, N_EXPERTS, dtype=jnp.float32))
    out = jnp.zeros((b * L, d), jnp.float32)
    for e in range(N_EXPERTS):
        out = out + gates[:, e:e + 1] * swiglu(t, w1[e], w3[e], w2[e]).astype(jnp.float32)
    return out.astype(x.dtype).reshape(b, L, d)


def setup_inputs(seed: int = 0) -> dict:
    key = jax.random.key(seed)
    keys = iter(jax.random.split(key, 48))

    def nrm(shape, scale):
        return scale * jax.random.normal(next(keys), shape, dtype=jnp.float32)

    n_pages = PAST_LEN // PAGE_SIZE
    n_pool = (DEC_BATCH * n_pages * 5) // 4
    page_table = jax.random.permutation(next(keys), n_pool)[: DEC_BATCH * n_pages].reshape(DEC_BATCH, n_pages).astype(jnp.int32)
    dt0 = jnp.exp(jax.random.uniform(next(keys), (N_EVEN, SSM_HEADS), minval=math.log(1e-3), maxval=math.log(1e-1)))
    inputs = {
        'x_prompt': nrm((BATCH, SEQ, D_MODEL), 1.0),
        'x_sample': nrm((DEC_BATCH, DEC_SEQ, D_MODEL), 1.0),
        'cache_sb_k': nrm((N_EVEN, n_pool, PAGE_SIZE, SB_HEADS, HEAD_DIM), 1.0),
        'cache_sb_v': nrm((N_EVEN, n_pool, PAGE_SIZE, SB_HEADS, HEAD_DIM), 1.0),
        'page_table': page_table,
        'state_ssm': nrm((N_EVEN, DEC_BATCH, SSM_HEADS, SSM_HEAD_DIM, SSM_STATE), 0.1),
        'state_conv': nrm((N_EVEN, DEC_BATCH, CONV_W - 1, CONV_DIM), 1.0),
    }
    for w, _ in C_GROUPS:
        inputs[f'cache_c_w{w}_k'] = nrm((N_ODD, DEC_BATCH, min(w, PAST_LEN), C_SLOTS, HEAD_DIM), 1.0)
        inputs[f'cache_c_w{w}_v'] = nrm((N_ODD, DEC_BATCH, min(w, PAST_LEN), C_SLOTS, HEAD_DIM), 1.0)
    inputs.update(dict(
        w_in_a=nrm((N_EVEN, D_MODEL, IN_A), D_MODEL ** -0.5),
        conv_w=nrm((N_EVEN, CONV_W, CONV_DIM), CONV_W ** -0.5),
        conv_b=nrm((N_EVEN, CONV_DIM), 0.01),
        dt_bias=dt0 + jnp.log(-jnp.expm1(-dt0)),
        a_log=jnp.log(jax.random.uniform(next(keys), (N_EVEN, SSM_HEADS), minval=1.0, maxval=16.0)),
        d_skip=1.0 + nrm((N_EVEN, SSM_HEADS), 0.1),
        ssm_norm_w=1.0 + nrm((N_EVEN, D_INNER), 0.02),
        sb_bias=SB_BIAS_INIT + nrm((N_EVEN, SB_HEADS), 0.1),
        w_out_a=nrm((N_EVEN, D_INNER + SB_W, D_MODEL), DN_BETA * (D_INNER + SB_W) ** -0.5),
        ln_a_mix_g=1.0 + nrm((N_EVEN, D_MODEL), 0.02),
        ln_a_mix_b=nrm((N_EVEN, D_MODEL), 0.02),
        ffn_w1=nrm((N_EVEN, D_MODEL, D_FF), D_MODEL ** -0.5),
        ffn_w3=nrm((N_EVEN, D_MODEL, D_FF), D_MODEL ** -0.5),
        ffn_w2=nrm((N_EVEN, D_FF, D_MODEL), DN_BETA * D_FF ** -0.5),
        ln_a_ffn_g=1.0 + nrm((N_EVEN, D_MODEL), 0.02),
        ln_a_ffn_b=nrm((N_EVEN, D_MODEL), 0.02),
        w_in_c=nrm((N_ODD, D_MODEL, 3 * C_W), D_MODEL ** -0.5),
        w_out_c=nrm((N_ODD, C_SLOTS * HEAD_DIM, D_MODEL), DN_BETA * (C_SLOTS * HEAD_DIM) ** -0.5),
        ln_c_mix_g=1.0 + nrm((N_ODD, D_MODEL), 0.02),
        ln_c_mix_b=nrm((N_ODD, D_MODEL), 0.02),
        router_w=nrm((N_ODD, D_MODEL, N_EXPERTS), D_MODEL ** -0.5),
        moe_w1=nrm((N_ODD, N_EXPERTS, D_MODEL, D_FF_EXPERT), D_MODEL ** -0.5),
        moe_w3=nrm((N_ODD, N_EXPERTS, D_MODEL, D_FF_EXPERT), D_MODEL ** -0.5),
        moe_w2=nrm((N_ODD, N_EXPERTS, D_FF_EXPERT, D_MODEL), DN_BETA * D_FF_EXPERT ** -0.5),
        ln_c_ffn_g=1.0 + nrm((N_ODD, D_MODEL), 0.02),
        ln_c_ffn_b=nrm((N_ODD, D_MODEL), 0.02),
    ))
    return inputs


def reference(x_prompt, x_sample, cache_sb_k, cache_sb_v, page_table, state_ssm, state_conv,
              cache_c_w128_k, cache_c_w128_v, cache_c_w512_k, cache_c_w512_v, cache_c_w2048_k, cache_c_w2048_v,
              w_in_a, conv_w, conv_b, dt_bias, a_log, d_skip, ssm_norm_w, sb_bias, w_out_a, ln_a_mix_g, ln_a_mix_b,
              ffn_w1, ffn_w3, ffn_w2, ln_a_ffn_g, ln_a_ffn_b,
              w_in_c, w_out_c, ln_c_mix_g, ln_c_mix_b, router_w, moe_w1, moe_w3, moe_w2, ln_c_ffn_g, ln_c_ffn_b):
    n_dec, n_pages = page_table.shape
    past_len = n_pages * PAGE_SIZE
    n_pr = x_prompt.shape[0]
    n_g = len(C_GROUPS)
    win_k = (cache_c_w128_k, cache_c_w512_k, cache_c_w2048_k)
    win_v = (cache_c_w128_v, cache_c_w512_v, cache_c_w2048_v)
    yp, ys = x_prompt, x_sample
    sb_kp, sb_vp, sb_ks, sb_vs, ssm_p, ssm_s, conv_p, conv_s = ([] for _ in range(8))
    c_kp, c_vp, c_ks, c_vs = ([[] for _ in range(n_g)] for _ in range(4))
    for layer in range(DEPTH):
        i = layer // 2
        if layer % 2 == 0:
            pa = (w_in_a[i], conv_w[i], conv_b[i], dt_bias[i], a_log[i], d_skip[i], ssm_norm_w[i], sb_bias[i], w_out_a[i])
            zc = jnp.zeros((n_pr, CONV_W - 1, CONV_DIM), yp.dtype)
            zs = jnp.zeros((n_pr, SSM_HEADS, SSM_HEAD_DIM, SSM_STATE), yp.dtype)
            mp, cp, sp, kp, vp = mixer_ab(yp, zc, zs, None, None, *pa)
            k_past = cache_sb_k[i][page_table].reshape(n_dec, past_len, SB_HEADS, HEAD_DIM)
            v_past = cache_sb_v[i][page_table].reshape(n_dec, past_len, SB_HEADS, HEAD_DIM)
            ms, cs, ss, kn, vn = mixer_ab(ys, state_conv[i], state_ssm[i], k_past, v_past, *pa)
            sb_kp.append(kp); sb_vp.append(vp); sb_ks.append(kn); sb_vs.append(vn)
            ssm_p.append(sp); ssm_s.append(ss); conv_p.append(cp); conv_s.append(cs)
            yp = layer_norm(DN_ALPHA * yp + mp, ln_a_mix_g[i], ln_a_mix_b[i])
            ys = layer_norm(DN_ALPHA * ys + ms, ln_a_mix_g[i], ln_a_mix_b[i])
            yp = layer_norm(DN_ALPHA * yp + swiglu(yp, ffn_w1[i], ffn_w3[i], ffn_w2[i]), ln_a_ffn_g[i], ln_a_ffn_b[i])
            ys = layer_norm(DN_ALPHA * ys + swiglu(ys, ffn_w1[i], ffn_w3[i], ffn_w2[i]), ln_a_ffn_g[i], ln_a_ffn_b[i])
        else:
            mp, kpl, vpl = mixer_c(yp, None, None, 0, w_in_c[i], w_out_c[i])
            ms, ksl, vsl = mixer_c(ys, [wk[i] for wk in win_k], [wv[i] for wv in win_v], past_len, w_in_c[i], w_out_c[i])
            for g in range(n_g):
                c_kp[g].append(kpl[g]); c_vp[g].append(vpl[g]); c_ks[g].append(ksl[g]); c_vs[g].append(vsl[g])
            yp = layer_norm(DN_ALPHA * yp + mp, ln_c_mix_g[i], ln_c_mix_b[i])
            ys = layer_norm(DN_ALPHA * ys + ms, ln_c_mix_g[i], ln_c_mix_b[i])
            yp = layer_norm(DN_ALPHA * yp + moe_swiglu(yp, router_w[i], moe_w1[i], moe_w3[i], moe_w2[i]), ln_c_ffn_g[i], ln_c_ffn_b[i])
            ys = layer_norm(DN_ALPHA * ys + moe_swiglu(ys, router_w[i], moe_w1[i], moe_w3[i], moe_w2[i]), ln_c_ffn_g[i], ln_c_ffn_b[i])
    return (yp, ys,
            jnp.stack(sb_kp), jnp.stack(sb_vp), jnp.stack(sb_ks), jnp.stack(sb_vs),
            jnp.stack(ssm_p), jnp.stack(ssm_s), jnp.stack(conv_p), jnp.stack(conv_s),
            jnp.stack(c_kp[0]), jnp.stack(c_vp[0]), jnp.stack(c_ks[0]), jnp.stack(c_vs[0]),
            jnp.stack(c_kp[1]), jnp.stack(c_vp[1]), jnp.stack(c_ks[1]), jnp.stack(c_vs[1]),
            jnp.stack(c_kp[2]), jnp.stack(c_vp[2]), jnp.stack(c_ks[2]), jnp.stack(c_vs[2]))
```

```python
import functools
import math

import numpy as np
import jax
import jax.numpy as jnp
from jax import lax
from jax.experimental import pallas as pl
from jax.experimental.pallas import tpu as pltpu

F32 = jnp.float32
BF16 = jnp.bfloat16
I32 = jnp.int32

HEAD_DIM = 64
SSM_HEAD_DIM = 64
SSM_GROUPS = 2
SSM_STATE = 128
CONV_W = 4
SSD_CHUNK = 128
C_GROUPS = ((128, 1), (512, 4), (2048, 16))
ROT_DIM = HEAD_DIM // 4
ROPE_THETA = 500000.0
TOP_K = 2
LN_EPS = 1e-5
RMS_EPS = 1e-6

LANES = 128
SUBLANES = 8
VMEM_LIMIT_BYTES = 56 * 1024 * 1024


def _params(*semantics):
    return pltpu.CompilerParams(dimension_semantics=semantics, vmem_limit_bytes=VMEM_LIMIT_BYTES)


def _dot(a, b):
    return jnp.dot(a, b, preferred_element_type=F32)


def _dot_nt(a, b):
    return lax.dot_general(a, b, (((1,), (1,)), ((), ())), preferred_element_type=F32)


def _layer_norm(y, g, b):
    mu = jnp.mean(y, axis=-1, keepdims=True)
    d = y - mu
    var = jnp.mean(d * d, axis=-1, keepdims=True)
    return d * lax.rsqrt(var + LN_EPS) * g + b


def _silu(a):
    return a * jax.nn.sigmoid(a)


def _split3(a):
    hi = a.astype(BF16)
    r = a - hi.astype(F32)
    mid = r.astype(BF16)
    lo = (r - mid.astype(F32)).astype(BF16)
    return hi, mid, lo


def _mm_body(x_ref, w_ref, o_ref):
    o_ref[...] = _dot(x_ref[...].astype(BF16), w_ref[...])


def _matmul(x, w, tm, name):
    m, k = x.shape
    n = w.shape[1]
    return pl.pallas_call(
        _mm_body,
        grid=(m // tm,),
        in_specs=[pl.BlockSpec((tm, k), lambda i: (i, 0)), pl.BlockSpec((k, n), lambda i: (0, 0))],
        out_specs=pl.BlockSpec((tm, n), lambda i: (i, 0)),
        out_shape=jax.ShapeDtypeStruct((m, n), F32),
        compiler_params=_params("parallel"),
        name=name,
    )(x, w)


def _mm_rope_body(x_ref, w_ref, cos_ref, sa_ref, sb_ref, o_ref, *, n_rot_tiles):
    y = _dot(x_ref[...].astype(BF16), w_ref[...])
    j = pl.program_id(1)

    @pl.when(j < n_rot_tiles)
    def _():
        cos = cos_ref[...]
        sa = sa_ref[...]
        sb = sb_ref[...]
        for c in range(y.shape[1] // LANES):
            yc = y[:, c * LANES:(c + 1) * LANES]
            o_ref[:, c * LANES:(c + 1) * LANES] = (
                yc * cos + pltpu.roll(yc, LANES - ROT_DIM // 2, 1) * sa + pltpu.roll(yc, ROT_DIM // 2, 1) * sb)

    @pl.when(j >= n_rot_tiles)
    def _():
        o_ref[...] = y


def _matmul_rope(x, w, cos, sa, sb, tm, name):
    m, k = x.shape
    n = w.shape[1]
    tn = n // 3
    nb = cos.shape[0] // tm
    return pl.pallas_call(
        functools.partial(_mm_rope_body, n_rot_tiles=2),
        grid=(m // tm, 3),
        in_specs=[pl.BlockSpec((tm, k), lambda i, j: (i, 0)),
                  pl.BlockSpec((k, tn), lambda i, j: (0, j)),
                  pl.BlockSpec((tm, LANES), lambda i, j: (i % nb, 0)),
                  pl.BlockSpec((tm, LANES), lambda i, j: (i % nb, 0)),
                  pl.BlockSpec((tm, LANES), lambda i, j: (i % nb, 0))],
        out_specs=pl.BlockSpec((tm, tn), lambda i, j: (i, j)),
        out_shape=jax.ShapeDtypeStruct((m, n), F32),
        compiler_params=_params("parallel", "arbitrary"),
        name=name,
    )(x, w, cos, sa, sb)


def _proj_ln_body(h1_ref, h2_ref, w1_ref, w2_ref, r_ref, g_ref, b_ref, o_ref, *, alpha):
    m = _dot(h1_ref[...].astype(BF16), w1_ref[...]) + _dot(h2_ref[...].astype(BF16), w2_ref[...])
    o_ref[...] = _layer_norm(alpha * r_ref[...] + m, g_ref[...], b_ref[...])


def _proj_ln(h1, h2, w1, w2, resid, g, b, alpha, tm, name):
    m, d = resid.shape
    k1, k2 = h1.shape[1], h2.shape[1]
    row = lambda i: (i, 0)
    fix = lambda i: (0, 0)
    return pl.pallas_call(
        functools.partial(_proj_ln_body, alpha=alpha),
        grid=(m // tm,),
        in_specs=[pl.BlockSpec((tm, k1), row), pl.BlockSpec((tm, k2), row),
                  pl.BlockSpec((k1, d), fix), pl.BlockSpec((k2, d), fix),
                  pl.BlockSpec((tm, d), row), pl.BlockSpec((1, d), fix), pl.BlockSpec((1, d), fix)],
        out_specs=pl.BlockSpec((tm, d), row),
        out_shape=jax.ShapeDtypeStruct((m, d), F32),
        compiler_params=_params("parallel"),
        name=name,
    )(h1, h2, w1, w2, resid, g, b)


def _merge_proj_ln_body(o0_ref, o1_ref, o2_ref, l0_ref, l1_ref, l2_ref, w_ref, r_ref, g_ref, b_ref, o_ref, *, alpha):
    l0, l1, l2 = l0_ref[...], l1_ref[...], l2_ref[...]
    mx = jnp.maximum(jnp.maximum(l0, l1), l2)
    e0, e1, e2 = jnp.exp(l0 - mx), jnp.exp(l1 - mx), jnp.exp(l2 - mx)
    o = (e0 * o0_ref[...] + e1 * o1_ref[...] + e2 * o2_ref[...]) / (e0 + e1 + e2)
    m = _dot(o.astype(BF16), w_ref[...])
    o_ref[...] = _layer_norm(alpha * r_ref[...] + m, g_ref[...], b_ref[...])


def _merge_proj_ln(os_, ls_, w, resid, g, b, alpha, tm, name):
    m, d = resid.shape
    k = w.shape[0]
    row = lambda i: (i, 0)
    fix = lambda i: (0, 0)
    return pl.pallas_call(
        functools.partial(_merge_proj_ln_body, alpha=alpha),
        grid=(m // tm,),
        in_specs=[pl.BlockSpec((tm, k), row)] * 6 + [pl.BlockSpec((k, d), fix), pl.BlockSpec((tm, d), row),
                                                     pl.BlockSpec((1, d), fix), pl.BlockSpec((1, d), fix)],
        out_specs=pl.BlockSpec((tm, d), row),
        out_shape=jax.ShapeDtypeStruct((m, d), F32),
        compiler_params=_params("parallel"),
        name=name,
    )(*os_, *ls_, w, resid, g, b)


def _ffn_ln_body(x_ref, w1_ref, w3_ref, w2_ref, g_ref, b_ref, o_ref, xb_ref, acc_ref, *, alpha):
    f = pl.program_id(1)

    @pl.when(f == 0)
    def _():
        xb_ref[...] = x_ref[...].astype(BF16)
        acc_ref[...] = jnp.zeros_like(acc_ref)

    xb = xb_ref[...]
    hid = _silu(_dot(xb, w1_ref[...])) * _dot(xb, w3_ref[...])
    acc_ref[...] += _dot(hid.astype(BF16), w2_ref[...])

    @pl.when(f == pl.num_programs(1) - 1)
    def _():
        o_ref[...] = _layer_norm(alpha * x_ref[...] + acc_ref[...], g_ref[...], b_ref[...])


def _ffn_ln(x, w1, w3, w2, g, b, alpha, tm, tf, name):
    m, d = x.shape
    ff = w1.shape[1]
    return pl.pallas_call(
        functools.partial(_ffn_ln_body, alpha=alpha),
        grid=(m // tm, ff // tf),
        in_specs=[pl.BlockSpec((tm, d), lambda i, f: (i, 0)),
                  pl.BlockSpec((d, tf), lambda i, f: (0, f)),
                  pl.BlockSpec((d, tf), lambda i, f: (0, f)),
                  pl.BlockSpec((tf, d), lambda i, f: (f, 0)),
                  pl.BlockSpec((1, d), lambda i, f: (0, 0)),
                  pl.BlockSpec((1, d), lambda i, f: (0, 0))],
        out_specs=pl.BlockSpec((tm, d), lambda i, f: (i, 0)),
        out_shape=jax.ShapeDtypeStruct((m, d), F32),
        scratch_shapes=[pltpu.VMEM((tm, d), BF16), pltpu.VMEM((tm, d), F32)],
        compiler_params=_params("parallel", "arbitrary"),
        name=name,
    )(x, w1, w3, w2, g, b)


def _ssd_body(xbc_ref, z_ref, dt_ref, conv0_ref, h0_ref, cw_ref, cb_ref, dtb_ref, aneg_ref, dsk_ref, nw_ref,
              y_ref, hn_ref, xpad_ref, *, q, valid_len, n_heads, d_inner):
    c = pl.program_id(1)
    conv_dim = xpad_ref.shape[1]
    gn = SSM_GROUPS * SSM_STATE
    heads_per_group = n_heads // SSM_GROUPS
    p = SSM_HEAD_DIM

    @pl.when(c == 0)
    def _():
        xpad_ref[0:SUBLANES, :] = jnp.zeros((SUBLANES, conv_dim), F32)
        xpad_ref[SUBLANES - (CONV_W - 1):SUBLANES, :] = conv0_ref[...]
        hn_ref[...] = h0_ref[...]

    x_raw = xbc_ref[...]
    xpad_ref[SUBLANES:SUBLANES + q, :] = x_raw
    cw = cw_ref[...]
    acc = cb_ref[...] + x_raw * cw[CONV_W - 1:CONV_W]
    for j in range(CONV_W - 1):
        off = SUBLANES - (CONV_W - 1) + j
        acc = acc + xpad_ref[off:off + q, :] * cw[j:j + 1]
    xpad_ref[SUBLANES - (CONV_W - 1):SUBLANES, :] = x_raw[q - (CONV_W - 1):q, :]
    xc = _silu(acc)
    xs = xc[:, :d_inner]
    bm = xc[:, d_inner:d_inner + gn].astype(BF16)
    cm = xc[:, d_inner + gn:d_inner + 2 * gn].astype(BF16)

    t = dt_ref[...] + dtb_ref[...]
    dtv = jnp.maximum(t, 0.0) + jnp.log(1.0 + jnp.exp(-jnp.abs(t)))
    rows = lax.broadcasted_iota(I32, (q, q), 0)
    cols = lax.broadcasted_iota(I32, (q, q), 1)
    if valid_len < q:
        dtv = jnp.where(lax.broadcasted_iota(I32, dtv.shape, 0) < valid_len, dtv, 0.0)
    a = dtv * aneg_ref[...]
    causal = cols <= rows
    tri = jnp.where(causal, 1.0, 0.0).astype(BF16)
    a_hi, a_mid, a_lo = _split3(a)
    a_cum = _dot(tri, a_hi) + _dot(tri, a_mid) + _dot(tri, a_lo)
    a_cum_t = a_cum.T
    a_tot = a_cum[q - 1:q, :]

    ys = []
    xws = []
    for g in range(SSM_GROUPS):
        cg = cm[:, g * SSM_STATE:(g + 1) * SSM_STATE]
        bg = bm[:, g * SSM_STATE:(g + 1) * SSM_STATE]
        scores = _dot_nt(cg, bg)
        for r in range(heads_per_group):
            h = g * heads_per_group + r
            col = a_cum[:, h:h + 1]
            row = a_cum_t[h:h + 1, :]
            decay_ls = jnp.where(causal, jnp.exp(col - row), 0.0)
            w_diag = (scores * decay_ls).astype(BF16)
            xs_h = xs[:, h * p:(h + 1) * p]
            xdt = xs_h * dtv[:, h:h + 1]
            y_diag = _dot(w_diag, xdt.astype(BF16))
            y_off = _dot_nt(cg, hn_ref[h].astype(BF16)) * jnp.exp(col)
            ys.append(y_diag + y_off)
            xws.append(xdt * jnp.exp(a_tot[:, h:h + 1] - col))
    y = jnp.concatenate(ys, axis=1) + dsk_ref[...] * xs
    xw = jnp.concatenate(xws, axis=1)

    heads_per_tile = LANES // p
    for j in range(d_inner // LANES):
        xw_t = xw[:, j * LANES:(j + 1) * LANES].T.astype(BF16)
        for hh in range(heads_per_tile):
            h = j * heads_per_tile + hh
            g = h // heads_per_group
            bg = bm[:, g * SSM_STATE:(g + 1) * SSM_STATE]
            st = _dot(xw_t[hh * p:(hh + 1) * p, :], bg)
            hn_ref[h] = jnp.exp(a_tot[:, h:h + 1]) * hn_ref[h] + st

    zz = z_ref[...]
    yg = y * _silu(zz)
    gsz = d_inner // SSM_GROUPS
    outs = []
    for g in range(SSM_GROUPS):
        blk = yg[:, g * gsz:(g + 1) * gsz]
        ms = jnp.mean(blk * blk, axis=-1, keepdims=True)
        outs.append(blk * lax.rsqrt(ms + RMS_EPS))
    y_ref[...] = jnp.concatenate(outs, axis=1) * nw_ref[...]


def _ssd(proj3, conv0, h0, cw, cb, dtb, aneg, dsk, nw, valid_len, name):
    b, length, _ = proj3.shape
    n_heads, p, n = h0.shape[1], h0.shape[2], h0.shape[3]
    d_inner = n_heads * p
    conv_dim = cw.shape[1]
    q = SSD_CHUNK
    z_blk = conv_dim // d_inner
    dt_blk = (proj3.shape[2] - LANES) // LANES
    fix2 = lambda i, c: (0, 0)
    return pl.pallas_call(
        functools.partial(_ssd_body, q=q, valid_len=valid_len, n_heads=n_heads, d_inner=d_inner),
        grid=(b, length // q),
        in_specs=[pl.BlockSpec((None, q, conv_dim), lambda i, c: (i, c, 0)),
                  pl.BlockSpec((None, q, d_inner), lambda i, c: (i, c, z_blk)),
                  pl.BlockSpec((None, q, LANES), lambda i, c: (i, c, dt_blk)),
                  pl.BlockSpec((None, CONV_W - 1, conv_dim), lambda i, c: (i, 0, 0)),
                  pl.BlockSpec((None, n_heads, p, n), lambda i, c: (i, 0, 0, 0)),
                  pl.BlockSpec((CONV_W, conv_dim), fix2),
                  pl.BlockSpec((1, conv_dim), fix2),
                  pl.BlockSpec((1, LANES), fix2),
                  pl.BlockSpec((1, LANES), fix2),
                  pl.BlockSpec((1, d_inner), fix2),
                  pl.BlockSpec((1, d_inner), fix2)],
        out_specs=[pl.BlockSpec((None, q, d_inner), lambda i, c: (i, c, 0)),
                   pl.BlockSpec((None, n_heads, p, n), lambda i, c: (i, 0, 0, 0))],
        out_shape=[jax.ShapeDtypeStruct((b, length, d_inner), F32),
                   jax.ShapeDtypeStruct((b, n_heads, p, n), F32)],
        scratch_shapes=[pltpu.VMEM((SUBLANES + q, conv_dim), F32)],
        compiler_params=_params("parallel", "arbitrary"),
        name=name,
    )(proj3, proj3, proj3, conv0, h0, cw, cb, dtb, aneg, dsk, nw)


SB_SEGMENTS = SUBLANES


def _sb_weights(negz, carry, valid):
    tk, tq = negz.shape
    seglen = tk // SB_SEGMENTS
    beta = 1.0 / (1.0 + jnp.exp(negz))
    fail = 1.0 - beta
    if valid is not None:
        beta = jnp.where(valid, beta, 0.0)
        fail = jnp.where(valid, fail, 1.0)
    fail3 = fail.reshape(seglen, SB_SEGMENTS, tq)
    run = jnp.ones((SB_SEGMENTS, tq), F32)
    excl = [None] * seglen
    for v in reversed(range(seglen)):
        excl[v] = run
        run = run * fail3[v]
    offs = [None] * SB_SEGMENTS
    tot = carry
    for s in reversed(range(SB_SEGMENTS)):
        offs[s] = tot
        tot = tot * run[s:s + 1, :]
    offs = jnp.concatenate(offs, axis=0)
    w3 = beta.reshape(seglen, SB_SEGMENTS, tq) * (jnp.stack(excl, axis=0) * offs[None])
    return w3.reshape(tk, tq), tot


def _sb_prompt_body(bias_ref, qt_ref, k_ref, vt_ref, o_ref, *, tq):
    h = pl.program_id(1)
    i = pl.program_id(2)
    bias = bias_ref[h]
    qt = qt_ref[...]
    seglen = tq // SB_SEGMENTS
    r = lax.broadcasted_iota(I32, (tq, tq), 0)
    key_pos = (r % SB_SEGMENTS) * seglen + r // SB_SEGMENTS
    strictly_before = key_pos < lax.broadcasted_iota(I32, (tq, tq), 1)

    def block(kb, carry, acc, valid):
        negz = _dot(k_ref[kb], qt) - bias
        w, carry = _sb_weights(negz, carry, valid)
        return carry, acc + _dot(vt_ref[kb], w.astype(BF16))

    carry = jnp.ones((1, tq), F32)
    acc = jnp.zeros((qt.shape[0], tq), F32)
    carry, acc = block(i, carry, acc, strictly_before)

    def step(it, state):
        return block(i - 1 - it, state[0], state[1], None)

    carry, acc = lax.fori_loop(0, i, step, (carry, acc))
    o_ref[...] = acc


def _sb_prompt(bias, qt, kp, vt, tq, name):
    b, nh, d, length = qt.shape
    nkb = length // tq
    return pl.pallas_call(
        functools.partial(_sb_prompt_body, tq=tq),
        grid=(b, nh, nkb),
        in_specs=[pl.BlockSpec(memory_space=pltpu.SMEM),
                  pl.BlockSpec((None, None, d, tq), lambda bi, hi, i: (bi, hi, 0, i)),
                  pl.BlockSpec((None, None, nkb, tq, d), lambda bi, hi, i: (bi, hi, 0, 0, 0)),
                  pl.BlockSpec((None, None, nkb, d, tq), lambda bi, hi, i: (bi, hi, 0, 0, 0))],
        out_specs=pl.BlockSpec((None, None, d, tq), lambda bi, hi, i: (bi, hi, 0, i)),
        out_shape=jax.ShapeDtypeStruct((b, nh, d, length), F32),
        compiler_params=_params("parallel", "parallel", "arbitrary"),
        name=name,
    )(bias, qt, kp, vt)


def _sb_sample_body(pt_ref, qbd_ref, bias_ref, knew_ref, vnew_ref, *refs, pages_per_step, n_new):
    k_refs = refs[:pages_per_step]
    v_refs = refs[pages_per_step:2 * pages_per_step]
    o_ref = refs[2 * pages_per_step]
    carry_ref = refs[2 * pages_per_step + 1]
    j = pl.program_id(1)
    qbd = qbd_ref[...]
    bias = bias_ref[...]
    nr = qbd.shape[0]
    tk = knew_ref.shape[0]
    kk = lax.broadcasted_iota(I32, (tk, tk), 0)
    ss = lax.broadcasted_iota(I32, (tk, tk), 1)
    upper = jnp.where(kk > ss, 1.0, 0.0).astype(BF16)

    def block(k, v, valid):
        z = _dot_nt(qbd, k.astype(BF16)) + bias
        sp = jnp.maximum(z, 0.0) + jnp.log(1.0 + jnp.exp(-jnp.abs(z)))
        log_fail = -sp
        log_beta = z - sp
        if valid is not None:
            log_fail = jnp.where(valid, log_fail, 0.0)
        lf_hi = log_fail.astype(BF16)
        lf_lo = (log_fail - lf_hi.astype(F32)).astype(BF16)
        tail = _dot(lf_hi, upper) + _dot(lf_lo, upper) + carry_ref[...]
        w = jnp.exp(log_beta + tail)
        if valid is not None:
            w = jnp.where(valid, w, 0.0)
        carry_ref[...] += jnp.sum(log_fail, axis=-1, keepdims=True)
        o_ref[...] += _dot(w.astype(BF16), v.astype(BF16))

    @pl.when(j == 0)
    def _():
        carry_ref[...] = jnp.zeros_like(carry_ref)
        o_ref[...] = jnp.zeros_like(o_ref)
        qi = lax.broadcasted_iota(I32, (nr, tk), 0) % n_new
        jn = lax.broadcasted_iota(I32, (nr, tk), 1)
        block(knew_ref[...], vnew_ref[...], jn < qi)

    for p in range(pages_per_step):
        block(k_refs[p][...], v_refs[p][...], None)


def _sb_sample(page_table, qbd, bias_rows, knew, vnew, cache_k, cache_v, pages_per_step, n_new, name):
    n_seq, n_pages = page_table.shape
    nr, hd = qbd.shape[1], qbd.shape[2]
    page = cache_k.shape[1]
    steps = n_pages // pages_per_step

    def page_spec(p):
        return pl.BlockSpec((None, page, hd), lambda s, j, pt: (pt[s, n_pages - 1 - (j * pages_per_step + p)], 0, 0))

    seq3 = lambda s, j, pt: (s, 0, 0)
    grid_spec = pltpu.PrefetchScalarGridSpec(
        num_scalar_prefetch=1,
        grid=(n_seq, steps),
        in_specs=[pl.BlockSpec((None, nr, hd), seq3),
                  pl.BlockSpec((nr, 1), lambda s, j, pt: (0, 0)),
                  pl.BlockSpec((None, page, hd), seq3),
                  pl.BlockSpec((None, page, hd), seq3)]
                 + [page_spec(p) for p in range(pages_per_step)] * 2,
        out_specs=pl.BlockSpec((None, nr, hd), seq3),
        scratch_shapes=[pltpu.VMEM((nr, 1), F32)],
    )
    return pl.pallas_call(
        functools.partial(_sb_sample_body, pages_per_step=pages_per_step, n_new=n_new),
        grid_spec=grid_spec,
        out_shape=jax.ShapeDtypeStruct((n_seq, nr, hd), F32),
        compiler_params=_params("parallel", "arbitrary"),
        name=name,
    )(page_table, qbd, bias_rows, knew, vnew, *([cache_k] * pages_per_step), *([cache_v] * pages_per_step))


DIL_BLOCK = 128


def _dil_prompt_body(q_ref, k_ref, v_ref, o_ref, l_ref, *, bq):
    j = pl.program_id(1)
    t = DIL_BLOCK
    rows = lax.broadcasted_iota(I32, (t, t), 0)
    cols = lax.broadcasted_iota(I32, (t, t), 1)
    for sub in range(bq // t):
        u0 = pl.multiple_of(j * bq + sub * t, t)
        up = pl.multiple_of(jnp.maximum(u0 - t, 0), t)
        q = q_ref[sub * t:(sub + 1) * t, :]
        s_cur = jnp.where(cols <= rows, _dot_nt(q, k_ref[pl.ds(u0, t), :]), -jnp.inf)
        s_prev = jnp.where(jnp.logical_and(cols >= rows, u0 > 0), _dot_nt(q, k_ref[pl.ds(up, t), :]), -jnp.inf)
        mx = jnp.maximum(jnp.max(s_cur, axis=-1, keepdims=True), jnp.max(s_prev, axis=-1, keepdims=True))
        p_cur = jnp.exp(s_cur - mx)
        p_prev = jnp.exp(s_prev - mx)
        den = jnp.sum(p_cur, axis=-1, keepdims=True) + jnp.sum(p_prev, axis=-1, keepdims=True)
        acc = _dot(p_cur.astype(BF16), v_ref[pl.ds(u0, t), :]) + _dot(p_prev.astype(BF16), v_ref[pl.ds(up, t), :])
        o_ref[sub * t:(sub + 1) * t, :] = acc / den
        l_ref[sub * t:(sub + 1) * t, :] = jnp.broadcast_to(mx + jnp.log(den), acc.shape)


def _dil_prompt(q, k, v, name):
    n_seq, ls, d = q.shape
    bq = min(512, ls)
    blk = lambda s, j: (s, j, 0)
    whole = lambda s, j: (s, 0, 0)
    return pl.pallas_call(
        functools.partial(_dil_prompt_body, bq=bq),
        grid=(n_seq, ls // bq),
        in_specs=[pl.BlockSpec((None, bq, d), blk), pl.BlockSpec((None, ls, d), whole), pl.BlockSpec((None, ls, d), whole)],
        out_specs=[pl.BlockSpec((None, bq, d), blk), pl.BlockSpec((None, bq, d), blk)],
        out_shape=[jax.ShapeDtypeStruct((n_seq, ls, d), F32)] * 2,
        compiler_params=_params("parallel", "arbitrary"),
        name=name,
    )(q, k, v)


def _dil_sample_body(qbd_ref, kbuf_ref, vbuf_ref, knew_ref, vnew_ref, o_ref, l_ref, kout_ref, vout_ref,
                     *, dil, n_new):
    w = kbuf_ref.shape[0]
    nr = qbd_ref.shape[0]
    pad = knew_ref.shape[0]
    kb = kbuf_ref[...]
    vb = vbuf_ref[...]
    kn = knew_ref[...]
    vn = vnew_ref[...]
    kout_ref[0:w - n_new, :] = kb[n_new:w, :]
    kout_ref[w - n_new:w, :] = kn[0:n_new, :]
    vout_ref[0:w - n_new, :] = vb[n_new:w, :]
    vout_ref[w - n_new:w, :] = vn[0:n_new, :]

    qbd = qbd_ref[...]
    qi = lax.broadcasted_iota(I32, (nr, w), 0) % n_new
    diff = w + qi - lax.broadcasted_iota(I32, (nr, w), 1)
    ok_buf = jnp.logical_and(diff % dil == 0, diff <= dil * DIL_BLOCK)
    qi_n = lax.broadcasted_iota(I32, (nr, pad), 0) % n_new
    jn = lax.broadcasted_iota(I32, (nr, pad), 1)
    diff_n = qi_n - jn
    ok_new = jnp.logical_and(jnp.logical_and(diff_n >= 0, diff_n % dil == 0), jn < n_new)
    s_buf = jnp.where(ok_buf, _dot_nt(qbd, kb.astype(BF16)), -jnp.inf)
    s_new = jnp.where(ok_new, _dot_nt(qbd, kn.astype(BF16)), -jnp.inf)
    mx = jnp.maximum(jnp.max(s_buf, axis=-1, keepdims=True), jnp.max(s_new, axis=-1, keepdims=True))
    p_buf = jnp.exp(s_buf - mx)
    p_new = jnp.exp(s_new - mx)
    den = jnp.sum(p_buf, axis=-1, keepdims=True) + jnp.sum(p_new, axis=-1, keepdims=True)
    acc = _dot(p_buf.astype(BF16), vb.astype(BF16)) + _dot(p_new.astype(BF16), vn.astype(BF16))
    o_ref[...] = acc / den
    l_ref[...] = jnp.broadcast_to(mx + jnp.log(den), l_ref.shape)


def _dil_sample(qbd, kbuf, vbuf, knew, vnew, dil, n_new, name):
    n_seq, w, hd = kbuf.shape
    nr = qbd.shape[1]
    pad = knew.shape[1]
    s3 = lambda s: (s, 0, 0)
    return pl.pallas_call(
        functools.partial(_dil_sample_body, dil=dil, n_new=n_new),
        grid=(n_seq,),
        in_specs=[pl.BlockSpec((None, nr, hd), s3), pl.BlockSpec((None, w, hd), s3), pl.BlockSpec((None, w, hd), s3),
                  pl.BlockSpec((None, pad, hd), s3), pl.BlockSpec((None, pad, hd), s3)],
        out_specs=[pl.BlockSpec((None, nr, hd), s3), pl.BlockSpec((None, nr, LANES), s3),
                   pl.BlockSpec((None, w, hd), s3), pl.BlockSpec((None, w, hd), s3)],
        out_shape=[jax.ShapeDtypeStruct((n_seq, nr, hd), F32), jax.ShapeDtypeStruct((n_seq, nr, LANES), F32),
                   jax.ShapeDtypeStruct((n_seq, w, hd), F32), jax.ShapeDtypeStruct((n_seq, w, hd), F32)],
        compiler_params=_params("parallel"),
        name=name,
    )(qbd, kbuf, vbuf, knew, vnew)


def _router_body(x_ref, w_ref, idx_ref, gate_ref, *, n_experts):
    xh, xm, _ = _split3(x_ref[...])
    wh, wm, _ = _split3(w_ref[...])
    logits = _dot(xh, wh) + _dot(xm, wh) + _dot(xh, wm)
    lane = lax.broadcasted_iota(I32, logits.shape, 1).astype(F32)
    logits = jnp.where(lane < n_experts, logits, -jnp.inf)
    m1 = jnp.max(logits, axis=-1, keepdims=True)
    i1 = jnp.min(jnp.where(logits == m1, lane, float(LANES)), axis=-1, keepdims=True)
    rest = jnp.where(lane == i1, -jnp.inf, logits)
    m2 = jnp.max(rest, axis=-1, keepdims=True)
    i2 = jnp.min(jnp.where(rest == m2, lane, float(LANES)), axis=-1, keepdims=True)
    e = jnp.exp(m2 - m1)
    g1 = 1.0 / (1.0 + e)
    idx_ref[...] = jnp.where(lane == 0.0, i1, jnp.where(lane == 1.0, i2, 0.0)).astype(I32)
    gate_ref[...] = jnp.where(lane == 0.0, g1, jnp.where(lane == 1.0, e * g1, 0.0))


def _router(x, w_pad, n_experts, tm, name):
    m, d = x.shape
    return pl.pallas_call(
        functools.partial(_router_body, n_experts=n_experts),
        grid=(m // tm,),
        in_specs=[pl.BlockSpec((tm, d), lambda i: (i, 0)), pl.BlockSpec((d, LANES), lambda i: (0, 0))],
        out_specs=[pl.BlockSpec((tm, LANES), lambda i: (i, 0))] * 2,
        out_shape=[jax.ShapeDtypeStruct((m, LANES), I32), jax.ShapeDtypeStruct((m, LANES), F32)],
        compiler_params=_params("parallel"),
        name=name,
    )(x, w_pad)


def _row_copy(src_hbm, row, dst, r, sem):
    return pltpu.make_async_copy(src_hbm.at[pl.ds(row, 1)], dst.at[pl.ds(r, 1)], sem)


def _gather_body(src_ref, x_hbm, o_ref, buf, sem, *, rows):
    base = pl.program_id(0) * rows

    def start(r, c):
        _row_copy(x_hbm, src_ref[base + r], buf, r, sem).start()
        return c

    def wait(r, c):
        _row_copy(x_hbm, src_ref[base + r], buf, r, sem).wait()
        return c

    lax.fori_loop(0, rows, start, 0)
    lax.fori_loop(0, rows, wait, 0)
    o_ref[...] = buf[...].astype(BF16)


def _gather_rows(src, x, rows, name):
    n = src.shape[0]
    d = x.shape[1]
    grid_spec = pltpu.PrefetchScalarGridSpec(
        num_scalar_prefetch=1,
        grid=(n // rows,),
        in_specs=[pl.BlockSpec(memory_space=pl.ANY)],
        out_specs=pl.BlockSpec((rows, d), lambda i, s: (i, 0)),
        scratch_shapes=[pltpu.VMEM((rows, d), x.dtype), pltpu.SemaphoreType.DMA(())],
    )
    return pl.pallas_call(
        functools.partial(_gather_body, rows=rows),
        grid_spec=grid_spec,
        out_shape=jax.ShapeDtypeStruct((n, d), BF16),
        compiler_params=_params("arbitrary"),
        name=name,
    )(src, x)


def _experts_body(te_ref, tb_ref, nu_ref, x_ref, w1_ref, w3_ref, w2_ref, o_ref, acc_ref):
    i = pl.program_id(0)
    f = pl.program_id(1)

    @pl.when(i < nu_ref[0])
    def _():
        @pl.when(f == 0)
        def _():
            acc_ref[...] = jnp.zeros_like(acc_ref)

        x = x_ref[...]
        hid = _silu(_dot(x, w1_ref[...].astype(BF16))) * _dot(x, w3_ref[...].astype(BF16))
        acc_ref[...] += _dot(hid.astype(BF16), w2_ref[...].astype(BF16))

        @pl.when(f == pl.num_programs(1) - 1)
        def _():
            o_ref[...] = acc_ref[...]

    @pl.when(jnp.logical_and(i >= nu_ref[0], f == pl.num_programs(1) - 1))
    def _():
        o_ref[...] = jnp.zeros_like(o_ref)


def _experts(tile_expert, tile_block, n_used, xs, w1, w3, w2, tm, tf, name):
    n_tiles = tile_expert.shape[0]
    d = xs.shape[1]
    ff = w1.shape[2]
    nf = ff // tf

    def fcol(i, f, nu):
        return jnp.where(i < nu[0], f, nf - 1)

    grid_spec = pltpu.PrefetchScalarGridSpec(
        num_scalar_prefetch=3,
        grid=(n_tiles, nf),
        in_specs=[pl.BlockSpec((tm, d), lambda i, f, te, tb, nu: (tb[i], 0)),
                  pl.BlockSpec((None, d, tf), lambda i, f, te, tb, nu: (te[i], 0, fcol(i, f, nu))),
                  pl.BlockSpec((None, d, tf), lambda i, f, te, tb, nu: (te[i], 0, fcol(i, f, nu))),
                  pl.BlockSpec((None, tf, d), lambda i, f, te, tb, nu: (te[i], fcol(i, f, nu), 0))],
        out_specs=pl.BlockSpec((tm, d), lambda i, f, te, tb, nu: (i, 0)),
        scratch_shapes=[pltpu.VMEM((tm, d), F32)],
    )
    return pl.pallas_call(
        _experts_body,
        grid_spec=grid_spec,
        out_shape=jax.ShapeDtypeStruct((n_tiles * tm, d), F32),
        compiler_params=_params("arbitrary", "arbitrary"),
        name=name,
    )(tile_expert, tile_block, n_used, xs, w1, w3, w2)


def _combine_ln_body(slot_ref, y_hbm, x_ref, gate_ref, g_ref, b_ref, o_ref, buf1, buf2, sem, *, rows, t0, alpha):
    base = (pl.program_id(0) * rows + t0) * TOP_K

    def start(r, c):
        _row_copy(y_hbm, slot_ref[base + TOP_K * r], buf1, r, sem).start()
        _row_copy(y_hbm, slot_ref[base + TOP_K * r + 1], buf2, r, sem).start()
        return c

    def wait(r, c):
        _row_copy(y_hbm, slot_ref[base + TOP_K * r], buf1, r, sem).wait()
        _row_copy(y_hbm, slot_ref[base + TOP_K * r + 1], buf2, r, sem).wait()
        return c

    lax.fori_loop(0, rows, start, 0)
    lax.fori_loop(0, rows, wait, 0)
    gate = gate_ref[...]
    moe = gate[:, 0:1] * buf1[...] + gate[:, 1:2] * buf2[...]
    o_ref[...] = _layer_norm(alpha * x_ref[...] + moe, g_ref[...], b_ref[...])


def _combine_ln(slot, ys, x, gate, g, b, t0, alpha, rows, name):
    m, d = x.shape
    gb = t0 // rows
    grid_spec = pltpu.PrefetchScalarGridSpec(
        num_scalar_prefetch=1,
        grid=(m // rows,),
        in_specs=[pl.BlockSpec(memory_space=pl.ANY),
                  pl.BlockSpec((rows, d), lambda i, s: (i, 0)),
                  pl.BlockSpec((rows, LANES), lambda i, s: (i + gb, 0)),
                  pl.BlockSpec((1, d), lambda i, s: (0, 0)),
                  pl.BlockSpec((1, d), lambda i, s: (0, 0))],
        out_specs=pl.BlockSpec((rows, d), lambda i, s: (i, 0)),
        scratch_shapes=[pltpu.VMEM((rows, d), F32), pltpu.VMEM((rows, d), F32), pltpu.SemaphoreType.DMA(())],
    )
    return pl.pallas_call(
        functools.partial(_combine_ln_body, rows=rows, t0=t0, alpha=alpha),
        grid_spec=grid_spec,
        out_shape=jax.ShapeDtypeStruct((m, d), F32),
        compiler_params=_params("arbitrary"),
        name=name,
    )(slot, ys, x, gate, g, b)


def _moe_ln(xp, xs_, router_w, w1, w3, w2, g, b, alpha):
    n_experts = router_w.shape[1]
    d = xp.shape[1]
    tp, ts = xp.shape[0], xs_.shape[0]
    x_all = jnp.concatenate([xp, xs_], axis=0)
    t_all = tp + ts
    rw = jnp.pad(router_w, ((0, 0), (0, LANES - n_experts)))
    idx, gate = _router(x_all, rw, n_experts, ts, "moe_router")

    tm = 1024
    n_assign = t_all * TOP_K
    e_flat = idx[:, :TOP_K].reshape(n_assign)
    onehot = (e_flat[:, None] == jnp.arange(n_experts, dtype=I32)[None, :]).astype(I32)
    csum = jnp.cumsum(onehot, axis=0)
    pos = jnp.take_along_axis(csum, e_flat[:, None], axis=1)[:, 0] - 1
    counts = csum[-1]
    padded = ((counts + tm - 1) // tm) * tm
    ends = jnp.cumsum(padded)
    offs = ends - padded
    slot = (offs[e_flat] + pos).astype(I32)
    n_tiles = (n_assign + n_experts * (tm - 1)) // tm
    src = jnp.zeros((n_tiles * tm,), I32).at[slot].set(jnp.arange(n_assign, dtype=I32) // TOP_K)
    n_used = (ends[-1] // tm).astype(I32)
    tiles = jnp.arange(n_tiles, dtype=I32)
    t_exp = jnp.minimum(jnp.searchsorted(ends, tiles * tm, side="right").astype(I32), n_experts - 1)
    last = jnp.maximum(n_used - 1, 0)
    tile_expert = jnp.where(tiles < n_used, t_exp, t_exp[last])
    tile_block = jnp.where(tiles < n_used, tiles, last)

    x_sorted = _gather_rows(src, x_all, 256, "moe_gather")
    y_sorted = _experts(tile_expert, tile_block, n_used.reshape(1), x_sorted, w1, w3, w2, tm, 512, "moe_experts")
    yp = _combine_ln(slot, y_sorted, xp, gate, g, b, 0, alpha, 256, "moe_combine_prompt")
    ys = _combine_ln(slot, y_sorted, xs_, gate, g, b, tp, alpha, 256, "moe_combine_sample")
    return yp, ys


def _rope_tables(pos):
    half = ROT_DIM // 2
    inv = ROPE_THETA ** (-jnp.arange(0, ROT_DIM, 2, dtype=F32) / ROT_DIM)
    ang = pos.astype(F32)[:, None] * inv[None, :]
    cos, sin = jnp.cos(ang), jnp.sin(ang)
    n = pos.shape[0]
    ones = jnp.ones((n, HEAD_DIM - ROT_DIM), F32)
    zeros = jnp.zeros((n, HEAD_DIM - ROT_DIM), F32)
    zh = jnp.zeros((n, half), F32)
    reps = LANES // HEAD_DIM
    c = jnp.tile(jnp.concatenate([cos, cos, ones], axis=1), (1, reps))
    sa = jnp.tile(jnp.concatenate([-sin, zh, zeros], axis=1), (1, reps))
    sb = jnp.tile(jnp.concatenate([zh, sin, zeros], axis=1), (1, reps))
    return c, sa, sb


def _block_diag_q(q, scale):
    n_seq, n_new, nh, d = q.shape
    eye = jnp.eye(nh, dtype=F32)
    out = jnp.einsum("sihd,hg->shigd", q * scale, eye)
    return out.reshape(n_seq, nh * n_new, nh * d).astype(BF16)


def _diag_blocks(o, n_new, nh, d):
    n_seq = o.shape[0]
    o5 = o.reshape(n_seq, nh, n_new, nh, d)
    sel = jnp.stack([o5[:, h, :, h, :] for h in range(nh)], axis=2)
    return sel.reshape(n_seq * n_new, nh * d)


def kernel(x_prompt, x_sample, cache_sb_k, cache_sb_v, page_table, state_ssm, state_conv, cache_c_w128_k, cache_c_w128_v, cache_c_w512_k, cache_c_w512_v, cache_c_w2048_k, cache_c_w2048_v, w_in_a, conv_w, conv_b, dt_bias, a_log, d_skip, ssm_norm_w, sb_bias, w_out_a, ln_a_mix_g, ln_a_mix_b, ffn_w1, ffn_w3, ffn_w2, ln_a_ffn_g, ln_a_ffn_b, w_in_c, w_out_c, ln_c_mix_g, ln_c_mix_b, router_w, moe_w1, moe_w3, moe_w2, ln_c_ffn_g, ln_c_ffn_b):
    n_pr, seq, d_model = x_prompt.shape
    n_dec, dec_seq, _ = x_sample.shape
    n_pages = page_table.shape[1]
    page = cache_sb_k.shape[2]
    past_len = n_pages * page
    ssm_heads, p, n_state = state_ssm.shape[2], state_ssm.shape[3], state_ssm.shape[4]
    d_inner = ssm_heads * p
    conv_dim = state_conv.shape[3]
    sb_heads = cache_sb_k.shape[3]
    sb_w = sb_heads * HEAD_DIM
    c_slots = cache_c_w128_k.shape[3]
    c_wd = c_slots * HEAD_DIM
    n_groups = len(C_GROUPS)
    depth = w_in_a.shape[0] + w_in_c.shape[0]
    alpha = (2 * depth) ** 0.25
    tp = n_pr * seq
    ts = n_dec * dec_seq
    win_k = (cache_c_w128_k, cache_c_w512_k, cache_c_w2048_k)
    win_v = (cache_c_w128_v, cache_c_w512_v, cache_c_w2048_v)
    row = lambda v: v.reshape(1, -1)

    yp = x_prompt.reshape(tp, d_model)
    ys = x_sample.reshape(ts, d_model)
    outs = {}

    i = 0
    w = w_in_a[i]
    c0, c1, c2 = d_inner, d_inner + conv_dim, d_inner + conv_dim + ssm_heads
    w_re = jnp.concatenate([w[:, c0:c1], w[:, :c0], w[:, c2:], w[:, c1:c2],
                            jnp.zeros((d_model, LANES - ssm_heads), F32)], axis=1).astype(BF16)
    o_q = conv_dim + d_inner
    proj_p = _matmul(yp, w_re, 512, "in_proj_a_prompt")
    proj_s = _matmul(ys, w_re, ts, "in_proj_a_sample")

    pad_l = LANES - ssm_heads
    ssd_par = (conv_w[i], row(conv_b[i]), row(jnp.pad(dt_bias[i], (0, pad_l))),
               row(jnp.pad(-jnp.exp(a_log[i]), (0, pad_l))), row(jnp.repeat(d_skip[i], p)), row(ssm_norm_w[i]))
    ya_p, ssm_p = _ssd(proj_p.reshape(n_pr, seq, -1), jnp.zeros((n_pr, CONV_W - 1, conv_dim), F32),
                       jnp.zeros((n_pr, ssm_heads, p, n_state), F32), *ssd_par, SSD_CHUNK, "ssd_prompt")
    proj_s3 = proj_s.reshape(n_dec, dec_seq, -1)
    proj_s_pad = jnp.pad(proj_s3, ((0, 0), (0, SSD_CHUNK - dec_seq), (0, 0)))
    ya_s, ssm_s = _ssd(proj_s_pad, state_conv[i], state_ssm[i], *ssd_par, dec_seq, "ssd_sample")
    ya_p = ya_p.reshape(tp, d_inner)
    ya_s = ya_s[:, :dec_seq].reshape(ts, d_inner)
    conv_p = proj_p.reshape(n_pr, seq, -1)[:, seq - (CONV_W - 1):, :conv_dim]
    conv_s = jnp.concatenate([state_conv[i], proj_s3[:, :, :conv_dim]], axis=1)[:, dec_seq:]

    tq = 256
    seglen = tq // SB_SEGMENTS
    nkb = seq // tq
    q_p = proj_p[:, o_q:o_q + sb_w].reshape(n_pr, seq, sb_heads, HEAD_DIM)
    k_p = proj_p[:, o_q + sb_w:o_q + 2 * sb_w].reshape(n_pr, seq, sb_heads, HEAD_DIM)
    v_p = proj_p[:, o_q + 2 * sb_w:o_q + 3 * sb_w].reshape(n_pr, seq, sb_heads, HEAD_DIM)
    qt = (q_p * (-(HEAD_DIM ** -0.5))).astype(BF16).transpose(0, 2, 3, 1)
    kperm = (k_p.astype(BF16).reshape(n_pr, nkb, SB_SEGMENTS, seglen, sb_heads, HEAD_DIM)
             .transpose(0, 4, 1, 3, 2, 5).reshape(n_pr, sb_heads, nkb, tq, HEAD_DIM))
    vperm = (v_p.astype(BF16).reshape(n_pr, nkb, SB_SEGMENTS, seglen, sb_heads, HEAD_DIM)
             .transpose(0, 4, 1, 5, 3, 2).reshape(n_pr, sb_heads, nkb, HEAD_DIM, tq))
    ot = _sb_prompt(sb_bias[i], qt, kperm, vperm, tq, "sb_prompt")
    o_p = ot.transpose(0, 3, 1, 2).reshape(tp, sb_w)

    q_s = proj_s[:, o_q:o_q + sb_w].reshape(n_dec, dec_seq, sb_heads, HEAD_DIM)
    k_s = proj_s[:, o_q + sb_w:o_q + 2 * sb_w].reshape(n_dec, dec_seq, sb_w)
    v_s = proj_s[:, o_q + 2 * sb_w:o_q + 3 * sb_w].reshape(n_dec, dec_seq, sb_w)
    qbd = _block_diag_q(q_s, HEAD_DIM ** -0.5)
    bias_rows = jnp.repeat(sb_bias[i], dec_seq).reshape(sb_heads * dec_seq, 1)
    pad_new = ((0, 0), (0, page - dec_seq), (0, 0))
    o_s = _sb_sample(page_table, qbd, bias_rows, jnp.pad(k_s, pad_new), jnp.pad(v_s, pad_new),
                     cache_sb_k[i].reshape(-1, page, sb_w), cache_sb_v[i].reshape(-1, page, sb_w),
                     16, dec_seq, "sb_sample")
    o_s = _diag_blocks(o_s, dec_seq, sb_heads, HEAD_DIM)

    wo = w_out_a[i].astype(BF16)
    ln_g, ln_b = row(ln_a_mix_g[i]), row(ln_a_mix_b[i])
    yp = _proj_ln(ya_p, o_p, wo[:d_inner], wo[d_inner:], yp, ln_g, ln_b, alpha, 512, "out_proj_a_prompt")
    ys = _proj_ln(ya_s, o_s, wo[:d_inner], wo[d_inner:], ys, ln_g, ln_b, alpha, ts, "out_proj_a_sample")
    f1, f3, f2 = ffn_w1[i].astype(BF16), ffn_w3[i].astype(BF16), ffn_w2[i].astype(BF16)
    ln_g, ln_b = row(ln_a_ffn_g[i]), row(ln_a_ffn_b[i])
    d_ff = f1.shape[1]
    tf = 2 * LANES if d_ff % (2 * LANES) == 0 else LANES
    yp = _ffn_ln(yp, f1, f3, f2, ln_g, ln_b, alpha, 1024, tf, "ffn_prompt")
    ys = _ffn_ln(ys, f1, f3, f2, ln_g, ln_b, alpha, ts, tf, "ffn_sample")

    outs["sb_kp"] = k_p[None]
    outs["sb_vp"] = v_p[None]
    outs["sb_ks"] = k_s.reshape(1, n_dec, dec_seq, sb_heads, HEAD_DIM)
    outs["sb_vs"] = v_s.reshape(1, n_dec, dec_seq, sb_heads, HEAD_DIM)
    outs["ssm_p"], outs["ssm_s"], outs["conv_p"], outs["conv_s"] = ssm_p[None], ssm_s[None], conv_p[None], conv_s[None]

    wc = w_in_c[i].astype(BF16)
    gw = n_groups * c_wd
    cos_p, sa_p, sb_p = _rope_tables(jnp.arange(seq))
    cos_s, sa_s, sb_s = _rope_tables(past_len + jnp.arange(ts) % dec_seq)
    qkv_p = _matmul_rope(yp, wc, cos_p, sa_p, sb_p, 512, "in_proj_c_prompt")
    qkv_s = _matmul_rope(ys, wc, cos_s, sa_s, sb_s, ts, "in_proj_c_sample")

    scale = HEAD_DIM ** -0.5
    o_groups_p, l_groups_p, o_groups_s, l_groups_s = [], [], [], []
    qbd_s = None
    for g, (win, dil) in enumerate(C_GROUPS):
        cq, ck, cv = g * c_wd, gw + g * c_wd, 2 * gw + g * c_wd
        ls = seq // dil

        def deint(a, s=1.0):
            a = (a * s).astype(BF16).reshape(n_pr, ls, dil, c_slots, HEAD_DIM)
            return a.transpose(0, 3, 2, 1, 4).reshape(n_pr * c_slots * dil, ls, HEAD_DIM)

        def reint(a):
            a = a.reshape(n_pr, c_slots, dil, ls, HEAD_DIM)
            return a.transpose(0, 3, 2, 1, 4).reshape(tp, c_wd)

        kg_p = qkv_p[:, ck:ck + c_wd]
        vg_p = qkv_p[:, cv:cv + c_wd]
        o_g, l_g = _dil_prompt(deint(qkv_p[:, cq:cq + c_wd], scale), deint(kg_p), deint(vg_p), f"dilated_prompt_{g}")
        o_groups_p.append(reint(o_g))
        l_groups_p.append(reint(l_g))
        keep = min(win, seq)
        outs[f"c_kp{g}"] = kg_p.reshape(n_pr, seq, c_slots, HEAD_DIM)[:, seq - keep:][None]
        outs[f"c_vp{g}"] = vg_p.reshape(n_pr, seq, c_slots, HEAD_DIM)[:, seq - keep:][None]

        qg_s = qkv_s[:, cq:cq + c_wd].reshape(n_dec, dec_seq, c_slots, HEAD_DIM)
        pad_rows = ((0, 0), (0, DIL_BLOCK - dec_seq), (0, 0))
        kn = jnp.pad(qkv_s[:, ck:ck + c_wd].reshape(n_dec, dec_seq, c_wd), pad_rows)
        vn = jnp.pad(qkv_s[:, cv:cv + c_wd].reshape(n_dec, dec_seq, c_wd), pad_rows)
        wlen = win_k[g].shape[2]
        o_g, l_g, k_new, v_new = _dil_sample(_block_diag_q(qg_s, scale), win_k[g][i].reshape(n_dec, wlen, c_wd),
                                             win_v[g][i].reshape(n_dec, wlen, c_wd), kn, vn, dil, dec_seq,
                                             f"dilated_sample_{g}")
        o_groups_s.append(_diag_blocks(o_g, dec_seq, c_slots, HEAD_DIM))
        l_s = l_g[:, :, 0].reshape(n_dec, c_slots, dec_seq).transpose(0, 2, 1)
        l_groups_s.append(jnp.repeat(l_s.reshape(ts, c_slots), HEAD_DIM, axis=1))
        outs[f"c_ks{g}"] = k_new.reshape(1, n_dec, wlen, c_slots, HEAD_DIM)
        outs[f"c_vs{g}"] = v_new.reshape(1, n_dec, wlen, c_slots, HEAD_DIM)

    woc = w_out_c[i].astype(BF16)
    ln_g, ln_b = row(ln_c_mix_g[i]), row(ln_c_mix_b[i])
    yp = _merge_proj_ln(o_groups_p, l_groups_p, woc, yp, ln_g, ln_b, alpha, 512, "out_proj_c_prompt")
    ys = _merge_proj_ln(o_groups_s, l_groups_s, woc, ys, ln_g, ln_b, alpha, ts, "out_proj_c_sample")
    yp, ys = _moe_ln(yp, ys, router_w[i], moe_w1[i], moe_w3[i], moe_w2[i],
                     row(ln_c_ffn_g[i]), row(ln_c_ffn_b[i]), alpha)

    res = [yp.reshape(n_pr, seq, d_model), ys.reshape(n_dec, dec_seq, d_model),
           outs["sb_kp"], outs["sb_vp"], outs["sb_ks"], outs["sb_vs"],
           outs["ssm_p"], outs["ssm_s"], outs["conv_p"], outs["conv_s"]]
    for g in range(n_groups):
        res += [outs[f"c_kp{g}"], outs[f"c_vp{g}"], outs[f"c_ks{g}"], outs[f"c_vs{g}"]]
    return tuple(res)
```

```python
import functools
import math

import numpy as np
import jax
import jax.numpy as jnp
from jax import lax
from jax.experimental import pallas as pl
from jax.experimental.pallas import tpu as pltpu

F32 = jnp.float32
BF16 = jnp.bfloat16
I32 = jnp.int32

HEAD_DIM = 64
SSM_HEAD_DIM = 64
SSM_GROUPS = 2
SSM_STATE = 128
CONV_W = 4
SSD_CHUNK = 128
C_GROUPS = ((128, 1), (512, 4), (2048, 16))
ROT_DIM = HEAD_DIM // 4
ROPE_THETA = 500000.0
TOP_K = 2
LN_EPS = 1e-5
RMS_EPS = 1e-6

LANES = 128
SUBLANES = 8
VMEM_LIMIT_BYTES = 56 * 1024 * 1024


def _params(*semantics):
    return pltpu.CompilerParams(dimension_semantics=semantics, vmem_limit_bytes=VMEM_LIMIT_BYTES)


def _dot(a, b):
    return jnp.dot(a, b, preferred_element_type=F32)


def _dot_nt(a, b):
    return lax.dot_general(a, b, (((1,), (1,)), ((), ())), preferred_element_type=F32)


def _layer_norm(y, g, b):
    mu = jnp.mean(y, axis=-1, keepdims=True)
    d = y - mu
    var = jnp.mean(d * d, axis=-1, keepdims=True)
    return d * lax.rsqrt(var + LN_EPS) * g + b


def _silu(a):
    return a * jax.nn.sigmoid(a)


def _split3(a):
    hi = a.astype(BF16)
    r = a - hi.astype(F32)
    mid = r.astype(BF16)
    lo = (r - mid.astype(F32)).astype(BF16)
    return hi, mid, lo


def _mm_body(x_ref, w_ref, o_ref):
    o_ref[...] = _dot(x_ref[...].astype(BF16), w_ref[...])


def _matmul(x, w, tm, name):
    m, k = x.shape
    n = w.shape[1]
    return pl.pallas_call(
        _mm_body,
        grid=(m // tm,),
        in_specs=[pl.BlockSpec((tm, k), lambda i: (i, 0)), pl.BlockSpec((k, n), lambda i: (0, 0))],
        out_specs=pl.BlockSpec((tm, n), lambda i: (i, 0)),
        out_shape=jax.ShapeDtypeStruct((m, n), F32),
        compiler_params=_params("parallel"),
        name=name,
    )(x, w)


def _mm_rope_body(x_ref, w_ref, cos_ref, sa_ref, sb_ref, o_ref, *, n_rot_tiles):
    y = _dot(x_ref[...].astype(BF16), w_ref[...])
    j = pl.program_id(1)

    @pl.when(j < n_rot_tiles)
    def _():
        cos = cos_ref[...]
        sa = sa_ref[...]
        sb = sb_ref[...]
        for c in range(y.shape[1] // LANES):
            yc = y[:, c * LANES:(c + 1) * LANES]
            o_ref[:, c * LANES:(c + 1) * LANES] = (
                yc * cos + pltpu.roll(yc, LANES - ROT_DIM // 2, 1) * sa + pltpu.roll(yc, ROT_DIM // 2, 1) * sb)

    @pl.when(j >= n_rot_tiles)
    def _():
        o_ref[...] = y


def _matmul_rope(x, w, cos, sa, sb, tm, name):
    m, k = x.shape
    n = w.shape[1]
    tn = n // 3
    nb = cos.shape[0] // tm
    return pl.pallas_call(
        functools.partial(_mm_rope_body, n_rot_tiles=2),
        grid=(m // tm, 3),
        in_specs=[pl.BlockSpec((tm, k), lambda i, j: (i, 0)),
                  pl.BlockSpec((k, tn), lambda i, j: (0, j)),
                  pl.BlockSpec((tm, LANES), lambda i, j: (i % nb, 0)),
                  pl.BlockSpec((tm, LANES), lambda i, j: (i % nb, 0)),
                  pl.BlockSpec((tm, LANES), lambda i, j: (i % nb, 0))],
        out_specs=pl.BlockSpec((tm, tn), lambda i, j: (i, j)),
        out_shape=jax.ShapeDtypeStruct((m, n), F32),
        compiler_params=_params("parallel", "arbitrary"),
        name=name,
    )(x, w, cos, sa, sb)


def _proj_ln_body(h1_ref, h2_ref, w1_ref, w2_ref, r_ref, g_ref, b_ref, o_ref, *, alpha):
    m = _dot(h1_ref[...].astype(BF16), w1_ref[...]) + _dot(h2_ref[...].astype(BF16), w2_ref[...])
    o_ref[...] = _layer_norm(alpha * r_ref[...] + m, g_ref[...], b_ref[...])


def _proj_ln(h1, h2, w1, w2, resid, g, b, alpha, tm, name):
    m, d = resid.shape
    k1, k2 = h1.shape[1], h2.shape[1]
    row = lambda i: (i, 0)
    fix = lambda i: (0, 0)
    return pl.pallas_call(
        functools.partial(_proj_ln_body, alpha=alpha),
        grid=(m // tm,),
        in_specs=[pl.BlockSpec((tm, k1), row), pl.BlockSpec((tm, k2), row),
                  pl.BlockSpec((k1, d), fix), pl.BlockSpec((k2, d), fix),
                  pl.BlockSpec((tm, d), row), pl.BlockSpec((1, d), fix), pl.BlockSpec((1, d), fix)],
        out_specs=pl.BlockSpec((tm, d), row),
        out_shape=jax.ShapeDtypeStruct((m, d), F32),
        compiler_params=_params("parallel"),
        name=name,
    )(h1, h2, w1, w2, resid, g, b)


def _merge_proj_ln_body(o0_ref, o1_ref, o2_ref, l0_ref, l1_ref, l2_ref, w_ref, r_ref, g_ref, b_ref, o_ref, *, alpha):
    l0, l1, l2 = l0_ref[...], l1_ref[...], l2_ref[...]
    mx = jnp.maximum(jnp.maximum(l0, l1), l2)
    e0, e1, e2 = jnp.exp(l0 - mx), jnp.exp(l1 - mx), jnp.exp(l2 - mx)
    o = (e0 * o0_ref[...] + e1 * o1_ref[...] + e2 * o2_ref[...]) / (e0 + e1 + e2)
    m = _dot(o.astype(BF16), w_ref[...])
    o_ref[...] = _layer_norm(alpha * r_ref[...] + m, g_ref[...], b_ref[...])


def _merge_proj_ln(os_, ls_, w, resid, g, b, alpha, tm, name):
    m, d = resid.shape
    k = w.shape[0]
    row = lambda i: (i, 0)
    fix = lambda i: (0, 0)
    return pl.pallas_call(
        functools.partial(_merge_proj_ln_body, alpha=alpha),
        grid=(m // tm,),
        in_specs=[pl.BlockSpec((tm, k), row)] * 6 + [pl.BlockSpec((k, d), fix), pl.BlockSpec((tm, d), row),
                                                     pl.BlockSpec((1, d), fix), pl.BlockSpec((1, d), fix)],
        out_specs=pl.BlockSpec((tm, d), row),
        out_shape=jax.ShapeDtypeStruct((m, d), F32),
        compiler_params=_params("parallel"),
        name=name,
    )(*os_, *ls_, w, resid, g, b)


def _ffn_ln_body(x_ref, w1_ref, w3_ref, w2_ref, g_ref, b_ref, o_ref, xb_ref, acc_ref, *, alpha):
    f = pl.program_id(1)

    @pl.when(f == 0)
    def _():
        xb_ref[...] = x_ref[...].astype(BF16)
        acc_ref[...] = jnp.zeros_like(acc_ref)

    xb = xb_ref[...]
    hid = _silu(_dot(xb, w1_ref[...])) * _dot(xb, w3_ref[...])
    acc_ref[...] += _dot(hid.astype(BF16), w2_ref[...])

    @pl.when(f == pl.num_programs(1) - 1)
    def _():
        o_ref[...] = _layer_norm(alpha * x_ref[...] + acc_ref[...], g_ref[...], b_ref[...])


def _ffn_ln(x, w1, w3, w2, g, b, alpha, tm, tf, name):
    m, d = x.shape
    ff = w1.shape[1]
    return pl.pallas_call(
        functools.partial(_ffn_ln_body, alpha=alpha),
        grid=(m // tm, ff // tf),
        in_specs=[pl.BlockSpec((tm, d), lambda i, f: (i, 0)),
                  pl.BlockSpec((d, tf), lambda i, f: (0, f)),
                  pl.BlockSpec((d, tf), lambda i, f: (0, f)),
                  pl.BlockSpec((tf, d), lambda i, f: (f, 0)),
                  pl.BlockSpec((1, d), lambda i, f: (0, 0)),
                  pl.BlockSpec((1, d), lambda i, f: (0, 0))],
        out_specs=pl.BlockSpec((tm, d), lambda i, f: (i, 0)),
        out_shape=jax.ShapeDtypeStruct((m, d), F32),
        scratch_shapes=[pltpu.VMEM((tm, d), BF16), pltpu.VMEM((tm, d), F32)],
        compiler_params=_params("parallel", "arbitrary"),
        name=name,
    )(x, w1, w3, w2, g, b)


def _ssd_body(xbc_ref, z_ref, dt_ref, conv0_ref, h0_ref, cw_ref, cb_ref, dtb_ref, aneg_ref, dsk_ref, nw_ref,
              y_ref, hn_ref, xpad_ref, *, q, valid_len, n_heads, d_inner):
    c = pl.program_id(1)
    conv_dim = xpad_ref.shape[1]
    gn = SSM_GROUPS * SSM_STATE
    heads_per_group = n_heads // SSM_GROUPS
    p = SSM_HEAD_DIM

    @pl.when(c == 0)
    def _():
        xpad_ref[0:SUBLANES, :] = jnp.zeros((SUBLANES, conv_dim), F32)
        xpad_ref[SUBLANES - (CONV_W - 1):SUBLANES, :] = conv0_ref[...]
        hn_ref[...] = h0_ref[...]

    x_raw = xbc_ref[...]
    xpad_ref[SUBLANES:SUBLANES + q, :] = x_raw
    cw = cw_ref[...]
    acc = cb_ref[...] + x_raw * cw[CONV_W - 1:CONV_W]
    for j in range(CONV_W - 1):
        off = SUBLANES - (CONV_W - 1) + j
        acc = acc + xpad_ref[off:off + q, :] * cw[j:j + 1]
    xpad_ref[SUBLANES - (CONV_W - 1):SUBLANES, :] = x_raw[q - (CONV_W - 1):q, :]
    xc = _silu(acc)
    xs = xc[:, :d_inner]
    bm = xc[:, d_inner:d_inner + gn].astype(BF16)
    cm = xc[:, d_inner + gn:d_inner + 2 * gn].astype(BF16)

    t = dt_ref[...] + dtb_ref[...]
    dtv = jnp.maximum(t, 0.0) + jnp.log(1.0 + jnp.exp(-jnp.abs(t)))
    rows = lax.broadcasted_iota(I32, (q, q), 0)
    cols = lax.broadcasted_iota(I32, (q, q), 1)
    if valid_len < q:
        dtv = jnp.where(lax.broadcasted_iota(I32, dtv.shape, 0) < valid_len, dtv, 0.0)
    a = dtv * aneg_ref[...]
    causal = cols <= rows
    tri = jnp.where(causal, 1.0, 0.0).astype(BF16)
    a_hi, a_mid, a_lo = _split3(a)
    a_cum = _dot(tri, a_hi) + _dot(tri, a_mid) + _dot(tri, a_lo)
    a_cum_t = a_cum.T
    a_tot = a_cum[q - 1:q, :]

    ys = []
    xws = []
    for g in range(SSM_GROUPS):
        cg = cm[:, g * SSM_STATE:(g + 1) * SSM_STATE]
        bg = bm[:, g * SSM_STATE:(g + 1) * SSM_STATE]
        scores = _dot_nt(cg, bg)
        for r in range(heads_per_group):
            h = g * heads_per_group + r
            col = a_cum[:, h:h + 1]
            row = a_cum_t[h:h + 1, :]
            decay_ls = jnp.where(causal, jnp.exp(col - row), 0.0)
            w_diag = (scores * decay_ls).astype(BF16)
            xs_h = xs[:, h * p:(h + 1) * p]
            xdt = xs_h * dtv[:, h:h + 1]
            y_diag = _dot(w_diag, xdt.astype(BF16))
            y_off = _dot_nt(cg, hn_ref[h].astype(BF16)) * jnp.exp(col)
            ys.append(y_diag + y_off)
            xws.append(xdt * jnp.exp(a_tot[:, h:h + 1] - col))
    y = jnp.concatenate(ys, axis=1) + dsk_ref[...] * xs
    xw = jnp.concatenate(xws, axis=1)

    heads_per_tile = LANES // p
    for j in range(d_inner // LANES):
        xw_t = xw[:, j * LANES:(j + 1) * LANES].T.astype(BF16)
        for hh in range(heads_per_tile):
            h = j * heads_per_tile + hh
            g = h // heads_per_group
            bg = bm[:, g * SSM_STATE:(g + 1) * SSM_STATE]
            st = _dot(xw_t[hh * p:(hh + 1) * p, :], bg)
            hn_ref[h] = jnp.exp(a_tot[:, h:h + 1]) * hn_ref[h] + st

    zz = z_ref[...]
    yg = y * _silu(zz)
    gsz = d_inner // SSM_GROUPS
    outs = []
    for g in range(SSM_GROUPS):
        blk = yg[:, g * gsz:(g + 1) * gsz]
        ms = jnp.mean(blk * blk, axis=-1, keepdims=True)
        outs.append(blk * lax.rsqrt(ms + RMS_EPS))
    y_ref[...] = jnp.concatenate(outs, axis=1) * nw_ref[...]


def _ssd(proj3, conv0, h0, cw, cb, dtb, aneg, dsk, nw, valid_len, name):
    b, length, _ = proj3.shape
    n_heads, p, n = h0.shape[1], h0.shape[2], h0.shape[3]
    d_inner = n_heads * p
    conv_dim = cw.shape[1]
    q = SSD_CHUNK
    z_blk = conv_dim // d_inner
    dt_blk = (proj3.shape[2] - LANES) // LANES
    fix2 = lambda i, c: (0, 0)
    return pl.pallas_call(
        functools.partial(_ssd_body, q=q, valid_len=valid_len, n_heads=n_heads, d_inner=d_inner),
        grid=(b, length // q),
        in_specs=[pl.BlockSpec((None, q, conv_dim), lambda i, c: (i, c, 0)),
                  pl.BlockSpec((None, q, d_inner), lambda i, c: (i, c, z_blk)),
                  pl.BlockSpec((None, q, LANES), lambda i, c: (i, c, dt_blk)),
                  pl.BlockSpec((None, CONV_W - 1, conv_dim), lambda i, c: (i, 0, 0)),
                  pl.BlockSpec((None, n_heads, p, n), lambda i, c: (i, 0, 0, 0)),
                  pl.BlockSpec((CONV_W, conv_dim), fix2),
                  pl.BlockSpec((1, conv_dim), fix2),
                  pl.BlockSpec((1, LANES), fix2),
                  pl.BlockSpec((1, LANES), fix2),
                  pl.BlockSpec((1, d_inner), fix2),
                  pl.BlockSpec((1, d_inner), fix2)],
        out_specs=[pl.BlockSpec((None, q, d_inner), lambda i, c: (i, c, 0)),
                   pl.BlockSpec((None, n_heads, p, n), lambda i, c: (i, 0, 0, 0))],
        out_shape=[jax.ShapeDtypeStruct((b, length, d_inner), F32),
                   jax.ShapeDtypeStruct((b, n_heads, p, n), F32)],
        scratch_shapes=[pltpu.VMEM((SUBLANES + q, conv_dim), F32)],
        compiler_params=_params("parallel", "arbitrary"),
        name=name,
    )(proj3, proj3, proj3, conv0, h0, cw, cb, dtb, aneg, dsk, nw)


SB_SEGMENTS = SUBLANES


def _sb_weights(z, carry, valid):
    tk, tq = z.shape
    seglen = tk // SB_SEGMENTS
    fail = 1.0 / (1.0 + jnp.exp(z))
    if valid is not None:
        fail = jnp.where(valid, fail, 1.0)
    fail3 = fail.reshape(seglen, SB_SEGMENTS, tq)
    prods = [None] * (seglen + 1)
    prods[seglen] = jnp.ones((SB_SEGMENTS, tq), F32)
    for v in reversed(range(seglen)):
        prods[v] = prods[v + 1] * fail3[v]
    offs = [None] * SB_SEGMENTS
    tot = carry
    for s in reversed(range(SB_SEGMENTS)):
        offs[s] = tot
        tot = tot * prods[0][s:s + 1, :]
    offs = jnp.concatenate(offs, axis=0)
    w3 = jnp.stack([prods[v + 1] - prods[v] for v in range(seglen)], axis=0) * offs[None]
    return w3.reshape(tk, tq), tot


def _sb_prompt_body(bias_ref, qt_ref, k_ref, vt_ref, o_ref, *, tq, hb):
    i = pl.program_id(2)
    h0 = pl.program_id(1) * hb
    seglen = tq // SB_SEGMENTS
    r = lax.broadcasted_iota(I32, (tq, tq), 0)
    key_pos = (r % SB_SEGMENTS) * seglen + r // SB_SEGMENTS
    strictly_before = key_pos < lax.broadcasted_iota(I32, (tq, tq), 1)
    d = qt_ref.shape[1]

    def logits(hh, kb):
        return _dot(k_ref[hh, kb], qt_ref[hh]) + bias_ref[h0 + hh]

    state = []
    for hh in range(hb):
        w, carry = _sb_weights(logits(hh, i), jnp.ones((1, tq), F32), strictly_before)
        state.append((logits(hh, jnp.maximum(i - 1, 0)), w.astype(BF16), carry, jnp.zeros((d, tq), F32)))

    def step(it, st):
        kb = i - 1 - it
        out = []
        for hh in range(hb):
            z, w_prev, carry, acc = st[hh]
            acc = acc + _dot(vt_ref[hh, kb + 1], w_prev)
            z_next = logits(hh, jnp.maximum(kb - 1, 0))
            w, carry = _sb_weights(z, carry, None)
            out.append((z_next, w.astype(BF16), carry, acc))
        return tuple(out)

    state = lax.fori_loop(0, i, step, tuple(state))
    for hh in range(hb):
        o_ref[hh] = state[hh][3] + _dot(vt_ref[hh, 0], state[hh][1])


def _sb_prompt(bias, qt, kp, vt, tq, hb, name):
    b, nh, d, length = qt.shape
    nkb = length // tq
    return pl.pallas_call(
        functools.partial(_sb_prompt_body, tq=tq, hb=hb),
        grid=(b, nh // hb, nkb),
        in_specs=[pl.BlockSpec(memory_space=pltpu.SMEM),
                  pl.BlockSpec((None, hb, d, tq), lambda bi, hi, i: (bi, hi, 0, i)),
                  pl.BlockSpec((None, hb, nkb, tq, d), lambda bi, hi, i: (bi, hi, 0, 0, 0)),
                  pl.BlockSpec((None, hb, nkb, d, tq), lambda bi, hi, i: (bi, hi, 0, 0, 0))],
        out_specs=pl.BlockSpec((None, hb, d, tq), lambda bi, hi, i: (bi, hi, 0, i)),
        out_shape=jax.ShapeDtypeStruct((b, nh, d, length), F32),
        compiler_params=_params("parallel", "parallel", "arbitrary"),
        name=name,
    )(bias, qt, kp, vt)


def _sb_sample_body(pt_ref, qbd_ref, bias_ref, knew_ref, vnew_ref, *refs, pages_per_step, n_new):
    k_refs = refs[:pages_per_step]
    v_refs = refs[pages_per_step:2 * pages_per_step]
    o_ref = refs[2 * pages_per_step]
    carry_ref = refs[2 * pages_per_step + 1]
    j = pl.program_id(1)
    qbd = qbd_ref[...]
    bias = bias_ref[...]
    nr = qbd.shape[0]
    tk = knew_ref.shape[0]
    kk = lax.broadcasted_iota(I32, (tk, tk), 0)
    ss = lax.broadcasted_iota(I32, (tk, tk), 1)
    upper = jnp.where(kk > ss, 1.0, 0.0).astype(BF16)

    def block(k, v, valid):
        z = _dot_nt(qbd, k.astype(BF16)) + bias
        sp = jnp.maximum(z, 0.0) + jnp.log(1.0 + jnp.exp(-jnp.abs(z)))
        log_fail = -sp
        log_beta = z - sp
        if valid is not None:
            log_fail = jnp.where(valid, log_fail, 0.0)
        lf_hi = log_fail.astype(BF16)
        lf_lo = (log_fail - lf_hi.astype(F32)).astype(BF16)
        tail = _dot(lf_hi, upper) + _dot(lf_lo, upper) + carry_ref[...]
        w = jnp.exp(log_beta + tail)
        if valid is not None:
            w = jnp.where(valid, w, 0.0)
        carry_ref[...] += jnp.sum(log_fail, axis=-1, keepdims=True)
        o_ref[...] += _dot(w.astype(BF16), v.astype(BF16))

    @pl.when(j == 0)
    def _():
        carry_ref[...] = jnp.zeros_like(carry_ref)
        o_ref[...] = jnp.zeros_like(o_ref)
        qi = lax.broadcasted_iota(I32, (nr, tk), 0) % n_new
        jn = lax.broadcasted_iota(I32, (nr, tk), 1)
        block(knew_ref[...], vnew_ref[...], jn < qi)

    for p in range(pages_per_step):
        block(k_refs[p][...], v_refs[p][...], None)


def _sb_sample(page_table, qbd, bias_rows, knew, vnew, cache_k, cache_v, pages_per_step, n_new, name):
    n_seq, n_pages = page_table.shape
    nr, hd = qbd.shape[1], qbd.shape[2]
    page = cache_k.shape[1]
    steps = n_pages // pages_per_step

    def page_spec(p):
        return pl.BlockSpec((None, page, hd), lambda s, j, pt: (pt[s, n_pages - 1 - (j * pages_per_step + p)], 0, 0))

    seq3 = lambda s, j, pt: (s, 0, 0)
    grid_spec = pltpu.PrefetchScalarGridSpec(
        num_scalar_prefetch=1,
        grid=(n_seq, steps),
        in_specs=[pl.BlockSpec((None, nr, hd), seq3),
                  pl.BlockSpec((nr, 1), lambda s, j, pt: (0, 0)),
                  pl.BlockSpec((None, page, hd), seq3),
                  pl.BlockSpec((None, page, hd), seq3)]
                 + [page_spec(p) for p in range(pages_per_step)] * 2,
        out_specs=pl.BlockSpec((None, nr, hd), seq3),
        scratch_shapes=[pltpu.VMEM((nr, 1), F32)],
    )
    return pl.pallas_call(
        functools.partial(_sb_sample_body, pages_per_step=pages_per_step, n_new=n_new),
        grid_spec=grid_spec,
        out_shape=jax.ShapeDtypeStruct((n_seq, nr, hd), F32),
        compiler_params=_params("parallel", "arbitrary"),
        name=name,
    )(page_table, qbd, bias_rows, knew, vnew, *([cache_k] * pages_per_step), *([cache_v] * pages_per_step))


DIL_BLOCK = 128


def _dil_prompt_body(q_ref, k_ref, v_ref, o_ref, l_ref, *, bq):
    j = pl.program_id(1)
    t = DIL_BLOCK
    rows = lax.broadcasted_iota(I32, (t, t), 0)
    cols = lax.broadcasted_iota(I32, (t, t), 1)
    for sub in range(bq // t):
        u0 = pl.multiple_of(j * bq + sub * t, t)
        up = pl.multiple_of(jnp.maximum(u0 - t, 0), t)
        q = q_ref[sub * t:(sub + 1) * t, :]
        s_cur = jnp.where(cols <= rows, _dot_nt(q, k_ref[pl.ds(u0, t), :]), -jnp.inf)
        s_prev = jnp.where(jnp.logical_and(cols >= rows, u0 > 0), _dot_nt(q, k_ref[pl.ds(up, t), :]), -jnp.inf)
        mx = jnp.maximum(jnp.max(s_cur, axis=-1, keepdims=True), jnp.max(s_prev, axis=-1, keepdims=True))
        p_cur = jnp.exp(s_cur - mx)
        p_prev = jnp.exp(s_prev - mx)
        den = jnp.sum(p_cur, axis=-1, keepdims=True) + jnp.sum(p_prev, axis=-1, keepdims=True)
        acc = _dot(p_cur.astype(BF16), v_ref[pl.ds(u0, t), :]) + _dot(p_prev.astype(BF16), v_ref[pl.ds(up, t), :])
        o_ref[sub * t:(sub + 1) * t, :] = acc / den
        l_ref[sub * t:(sub + 1) * t, :] = jnp.broadcast_to(mx + jnp.log(den), acc.shape)


def _dil_prompt(q, k, v, name):
    n_seq, ls, d = q.shape
    bq = min(512, ls)
    blk = lambda s, j: (s, j, 0)
    whole = lambda s, j: (s, 0, 0)
    return pl.pallas_call(
        functools.partial(_dil_prompt_body, bq=bq),
        grid=(n_seq, ls // bq),
        in_specs=[pl.BlockSpec((None, bq, d), blk), pl.BlockSpec((None, ls, d), whole), pl.BlockSpec((None, ls, d), whole)],
        out_specs=[pl.BlockSpec((None, bq, d), blk), pl.BlockSpec((None, bq, d), blk)],
        out_shape=[jax.ShapeDtypeStruct((n_seq, ls, d), F32)] * 2,
        compiler_params=_params("parallel", "arbitrary"),
        name=name,
    )(q, k, v)


def _dil_sample_body(qbd_ref, kbuf_ref, vbuf_ref, knew_ref, vnew_ref, o_ref, l_ref, kout_ref, vout_ref,
                     *, dil, n_new):
    w = kbuf_ref.shape[0]
    nr = qbd_ref.shape[0]
    pad = knew_ref.shape[0]
    kb = kbuf_ref[...]
    vb = vbuf_ref[...]
    kn = knew_ref[...]
    vn = vnew_ref[...]
    kout_ref[0:w - n_new, :] = kb[n_new:w, :]
    kout_ref[w - n_new:w, :] = kn[0:n_new, :]
    vout_ref[0:w - n_new, :] = vb[n_new:w, :]
    vout_ref[w - n_new:w, :] = vn[0:n_new, :]

    qbd = qbd_ref[...]
    qi = lax.broadcasted_iota(I32, (nr, w), 0) % n_new
    diff = w + qi - lax.broadcasted_iota(I32, (nr, w), 1)
    ok_buf = jnp.logical_and(diff % dil == 0, diff <= dil * DIL_BLOCK)
    qi_n = lax.broadcasted_iota(I32, (nr, pad), 0) % n_new
    jn = lax.broadcasted_iota(I32, (nr, pad), 1)
    diff_n = qi_n - jn
    ok_new = jnp.logical_and(jnp.logical_and(diff_n >= 0, diff_n % dil == 0), jn < n_new)
    s_buf = jnp.where(ok_buf, _dot_nt(qbd, kb.astype(BF16)), -jnp.inf)
    s_new = jnp.where(ok_new, _dot_nt(qbd, kn.astype(BF16)), -jnp.inf)
    mx = jnp.maximum(jnp.max(s_buf, axis=-1, keepdims=True), jnp.max(s_new, axis=-1, keepdims=True))
    p_buf = jnp.exp(s_buf - mx)
    p_new = jnp.exp(s_new - mx)
    den = jnp.sum(p_buf, axis=-1, keepdims=True) + jnp.sum(p_new, axis=-1, keepdims=True)
    acc = _dot(p_buf.astype(BF16), vb.astype(BF16)) + _dot(p_new.astype(BF16), vn.astype(BF16))
    o_ref[...] = acc / den
    l_ref[...] = jnp.broadcast_to(mx + jnp.log(den), l_ref.shape)


def _dil_sample(qbd, kbuf, vbuf, knew, vnew, dil, n_new, name):
    n_seq, w, hd = kbuf.shape
    nr = qbd.shape[1]
    pad = knew.shape[1]
    s3 = lambda s: (s, 0, 0)
    return pl.pallas_call(
        functools.partial(_dil_sample_body, dil=dil, n_new=n_new),
        grid=(n_seq,),
        in_specs=[pl.BlockSpec((None, nr, hd), s3), pl.BlockSpec((None, w, hd), s3), pl.BlockSpec((None, w, hd), s3),
                  pl.BlockSpec((None, pad, hd), s3), pl.BlockSpec((None, pad, hd), s3)],
        out_specs=[pl.BlockSpec((None, nr, hd), s3), pl.BlockSpec((None, nr, LANES), s3),
                   pl.BlockSpec((None, w, hd), s3), pl.BlockSpec((None, w, hd), s3)],
        out_shape=[jax.ShapeDtypeStruct((n_seq, nr, hd), F32), jax.ShapeDtypeStruct((n_seq, nr, LANES), F32),
                   jax.ShapeDtypeStruct((n_seq, w, hd), F32), jax.ShapeDtypeStruct((n_seq, w, hd), F32)],
        compiler_params=_params("parallel"),
        name=name,
    )(qbd, kbuf, vbuf, knew, vnew)


def _router_body(x_ref, w_ref, idx_ref, gate_ref, *, n_experts):
    xh, xm, _ = _split3(x_ref[...])
    wh, wm, _ = _split3(w_ref[...])
    logits = _dot(xh, wh) + _dot(xm, wh) + _dot(xh, wm)
    lane = lax.broadcasted_iota(I32, logits.shape, 1).astype(F32)
    logits = jnp.where(lane < n_experts, logits, -jnp.inf)
    m1 = jnp.max(logits, axis=-1, keepdims=True)
    i1 = jnp.min(jnp.where(logits == m1, lane, float(LANES)), axis=-1, keepdims=True)
    rest = jnp.where(lane == i1, -jnp.inf, logits)
    m2 = jnp.max(rest, axis=-1, keepdims=True)
    i2 = jnp.min(jnp.where(rest == m2, lane, float(LANES)), axis=-1, keepdims=True)
    e = jnp.exp(m2 - m1)
    g1 = 1.0 / (1.0 + e)
    idx_ref[...] = jnp.where(lane == 0.0, i1, jnp.where(lane == 1.0, i2, 0.0)).astype(I32)
    gate_ref[...] = jnp.where(lane == 0.0, g1, jnp.where(lane == 1.0, e * g1, 0.0))


def _router(x, w_pad, n_experts, tm, name):
    m, d = x.shape
    return pl.pallas_call(
        functools.partial(_router_body, n_experts=n_experts),
        grid=(m // tm,),
        in_specs=[pl.BlockSpec((tm, d), lambda i: (i, 0)), pl.BlockSpec((d, LANES), lambda i: (0, 0))],
        out_specs=[pl.BlockSpec((tm, LANES), lambda i: (i, 0))] * 2,
        out_shape=[jax.ShapeDtypeStruct((m, LANES), I32), jax.ShapeDtypeStruct((m, LANES), F32)],
        compiler_params=_params("parallel"),
        name=name,
    )(x, w_pad)


def _row_copy(src_hbm, row, dst, r, sem):
    return pltpu.make_async_copy(src_hbm.at[pl.ds(row, 1)], dst.at[pl.ds(r, 1)], sem)


def _gather_body(src_ref, x_hbm, o_ref, buf, sem, *, rows):
    base = pl.program_id(0) * rows

    def start(r, c):
        _row_copy(x_hbm, src_ref[base + r], buf, r, sem).start()
        return c

    def wait(r, c):
        _row_copy(x_hbm, src_ref[base + r], buf, r, sem).wait()
        return c

    lax.fori_loop(0, rows, start, 0)
    lax.fori_loop(0, rows, wait, 0)
    o_ref[...] = buf[...].astype(BF16)


def _gather_rows(src, x, rows, name):
    n = src.shape[0]
    d = x.shape[1]
    grid_spec = pltpu.PrefetchScalarGridSpec(
        num_scalar_prefetch=1,
        grid=(n // rows,),
        in_specs=[pl.BlockSpec(memory_space=pl.ANY)],
        out_specs=pl.BlockSpec((rows, d), lambda i, s: (i, 0)),
        scratch_shapes=[pltpu.VMEM((rows, d), x.dtype), pltpu.SemaphoreType.DMA(())],
    )
    return pl.pallas_call(
        functools.partial(_gather_body, rows=rows),
        grid_spec=grid_spec,
        out_shape=jax.ShapeDtypeStruct((n, d), BF16),
        compiler_params=_params("arbitrary"),
        name=name,
    )(src, x)


def _experts_body(te_ref, tb_ref, nu_ref, x_ref, w1_ref, w3_ref, w2_ref, o_ref, acc_ref):
    i = pl.program_id(0)
    f = pl.program_id(1)

    @pl.when(i < nu_ref[0])
    def _():
        @pl.when(f == 0)
        def _():
            acc_ref[...] = jnp.zeros_like(acc_ref)

        x = x_ref[...]
        hid = _silu(_dot(x, w1_ref[...].astype(BF16))) * _dot(x, w3_ref[...].astype(BF16))
        acc_ref[...] += _dot(hid.astype(BF16), w2_ref[...].astype(BF16))

        @pl.when(f == pl.num_programs(1) - 1)
        def _():
            o_ref[...] = acc_ref[...]

    @pl.when(jnp.logical_and(i >= nu_ref[0], f == pl.num_programs(1) - 1))
    def _():
        o_ref[...] = jnp.zeros_like(o_ref)


def _experts(tile_expert, tile_block, n_used, xs, w1, w3, w2, tm, tf, name):
    n_tiles = tile_expert.shape[0]
    d = xs.shape[1]
    ff = w1.shape[2]
    nf = ff // tf

    def fcol(i, f, nu):
        return jnp.where(i < nu[0], f, nf - 1)

    grid_spec = pltpu.PrefetchScalarGridSpec(
        num_scalar_prefetch=3,
        grid=(n_tiles, nf),
        in_specs=[pl.BlockSpec((tm, d), lambda i, f, te, tb, nu: (tb[i], 0)),
                  pl.BlockSpec((None, d, tf), lambda i, f, te, tb, nu: (te[i], 0, fcol(i, f, nu))),
                  pl.BlockSpec((None, d, tf), lambda i, f, te, tb, nu: (te[i], 0, fcol(i, f, nu))),
                  pl.BlockSpec((None, tf, d), lambda i, f, te, tb, nu: (te[i], fcol(i, f, nu), 0))],
        out_specs=pl.BlockSpec((tm, d), lambda i, f, te, tb, nu: (i, 0)),
        scratch_shapes=[pltpu.VMEM((tm, d), F32)],
    )
    return pl.pallas_call(
        _experts_body,
        grid_spec=grid_spec,
        out_shape=jax.ShapeDtypeStruct((n_tiles * tm, d), F32),
        compiler_params=_params("arbitrary", "arbitrary"),
        name=name,
    )(tile_expert, tile_block, n_used, xs, w1, w3, w2)


def _combine_ln_body(slot_ref, y_hbm, x_ref, gate_ref, g_ref, b_ref, o_ref, buf1, buf2, sem, *, rows, t0, alpha):
    base = (pl.program_id(0) * rows + t0) * TOP_K

    def start(r, c):
        _row_copy(y_hbm, slot_ref[base + TOP_K * r], buf1, r, sem).start()
        _row_copy(y_hbm, slot_ref[base + TOP_K * r + 1], buf2, r, sem).start()
        return c

    def wait(r, c):
        _row_copy(y_hbm, slot_ref[base + TOP_K * r], buf1, r, sem).wait()
        _row_copy(y_hbm, slot_ref[base + TOP_K * r + 1], buf2, r, sem).wait()
        return c

    lax.fori_loop(0, rows, start, 0)
    lax.fori_loop(0, rows, wait, 0)
    gate = gate_ref[...]
    moe = gate[:, 0:1] * buf1[...] + gate[:, 1:2] * buf2[...]
    o_ref[...] = _layer_norm(alpha * x_ref[...] + moe, g_ref[...], b_ref[...])


def _combine_ln(slot, ys, x, gate, g, b, t0, alpha, rows, name):
    m, d = x.shape
    gb = t0 // rows
    grid_spec = pltpu.PrefetchScalarGridSpec(
        num_scalar_prefetch=1,
        grid=(m // rows,),
        in_specs=[pl.BlockSpec(memory_space=pl.ANY),
                  pl.BlockSpec((rows, d), lambda i, s: (i, 0)),
                  pl.BlockSpec((rows, LANES), lambda i, s: (i + gb, 0)),
                  pl.BlockSpec((1, d), lambda i, s: (0, 0)),
                  pl.BlockSpec((1, d), lambda i, s: (0, 0))],
        out_specs=pl.BlockSpec((rows, d), lambda i, s: (i, 0)),
        scratch_shapes=[pltpu.VMEM((rows, d), F32), pltpu.VMEM((rows, d), F32), pltpu.SemaphoreType.DMA(())],
    )
    return pl.pallas_call(
        functools.partial(_combine_ln_body, rows=rows, t0=t0, alpha=alpha),
        grid_spec=grid_spec,
        out_shape=jax.ShapeDtypeStruct((m, d), F32),
        compiler_params=_params("arbitrary"),
        name=name,
    )(slot, ys, x, gate, g, b)


def _moe_ln(xp, xs_, router_w, w1, w3, w2, g, b, alpha):
    n_experts = router_w.shape[1]
    d = xp.shape[1]
    tp, ts = xp.shape[0], xs_.shape[0]
    x_all = jnp.concatenate([xp, xs_], axis=0)
    t_all = tp + ts
    rw = jnp.pad(router_w, ((0, 0), (0, LANES - n_experts)))
    idx, gate = _router(x_all, rw, n_experts, ts, "moe_router")

    tm = 1024
    n_assign = t_all * TOP_K
    e_flat = idx[:, :TOP_K].reshape(n_assign)
    onehot = (e_flat[:, None] == jnp.arange(n_experts, dtype=I32)[None, :]).astype(I32)
    csum = jnp.cumsum(onehot, axis=0)
    pos = jnp.take_along_axis(csum, e_flat[:, None], axis=1)[:, 0] - 1
    counts = csum[-1]
    padded = ((counts + tm - 1) // tm) * tm
    ends = jnp.cumsum(padded)
    offs = ends - padded
    slot = (offs[e_flat] + pos).astype(I32)
    n_tiles = (n_assign + n_experts * (tm - 1)) // tm
    src = jnp.zeros((n_tiles * tm,), I32).at[slot].set(jnp.arange(n_assign, dtype=I32) // TOP_K)
    n_used = (ends[-1] // tm).astype(I32)
    tiles = jnp.arange(n_tiles, dtype=I32)
    t_exp = jnp.minimum(jnp.sum((tiles[:, None] * tm >= ends[None, :]).astype(I32), axis=1), n_experts - 1)
    last = jnp.maximum(n_used - 1, 0)
    tile_expert = jnp.where(tiles < n_used, t_exp, t_exp[last])
    tile_block = jnp.where(tiles < n_used, tiles, last)

    x_sorted = _gather_rows(src, x_all, 256, "moe_gather")
    y_sorted = _experts(tile_expert, tile_block, n_used.reshape(1), x_sorted, w1, w3, w2, tm, 512, "moe_experts")
    yp = _combine_ln(slot, y_sorted, xp, gate, g, b, 0, alpha, 256, "moe_combine_prompt")
    ys = _combine_ln(slot, y_sorted, xs_, gate, g, b, tp, alpha, 256, "moe_combine_sample")
    return yp, ys


def _rope_tables(pos):
    half = ROT_DIM // 2
    inv = ROPE_THETA ** (-jnp.arange(0, ROT_DIM, 2, dtype=F32) / ROT_DIM)
    ang = pos.astype(F32)[:, None] * inv[None, :]
    cos, sin = jnp.cos(ang), jnp.sin(ang)
    n = pos.shape[0]
    ones = jnp.ones((n, HEAD_DIM - ROT_DIM), F32)
    zeros = jnp.zeros((n, HEAD_DIM - ROT_DIM), F32)
    zh = jnp.zeros((n, half), F32)
    reps = LANES // HEAD_DIM
    c = jnp.tile(jnp.concatenate([cos, cos, ones], axis=1), (1, reps))
    sa = jnp.tile(jnp.concatenate([-sin, zh, zeros], axis=1), (1, reps))
    sb = jnp.tile(jnp.concatenate([zh, sin, zeros], axis=1), (1, reps))
    return c, sa, sb


def _block_diag_q(q, scale):
    n_seq, n_new, nh, d = q.shape
    eye = jnp.eye(nh, dtype=F32)
    out = jnp.einsum("sihd,hg->shigd", q * scale, eye)
    return out.reshape(n_seq, nh * n_new, nh * d).astype(BF16)


def _diag_blocks(o, n_new, nh, d):
    n_seq = o.shape[0]
    o5 = o.reshape(n_seq, nh, n_new, nh, d)
    sel = jnp.stack([o5[:, h, :, h, :] for h in range(nh)], axis=2)
    return sel.reshape(n_seq * n_new, nh * d)


def kernel(x_prompt, x_sample, cache_sb_k, cache_sb_v, page_table, state_ssm, state_conv, cache_c_w128_k, cache_c_w128_v, cache_c_w512_k, cache_c_w512_v, cache_c_w2048_k, cache_c_w2048_v, w_in_a, conv_w, conv_b, dt_bias, a_log, d_skip, ssm_norm_w, sb_bias, w_out_a, ln_a_mix_g, ln_a_mix_b, ffn_w1, ffn_w3, ffn_w2, ln_a_ffn_g, ln_a_ffn_b, w_in_c, w_out_c, ln_c_mix_g, ln_c_mix_b, router_w, moe_w1, moe_w3, moe_w2, ln_c_ffn_g, ln_c_ffn_b):
    n_pr, seq, d_model = x_prompt.shape
    n_dec, dec_seq, _ = x_sample.shape
    n_pages = page_table.shape[1]
    page = cache_sb_k.shape[2]
    past_len = n_pages * page
    ssm_heads, p, n_state = state_ssm.shape[2], state_ssm.shape[3], state_ssm.shape[4]
    d_inner = ssm_heads * p
    conv_dim = state_conv.shape[3]
    sb_heads = cache_sb_k.shape[3]
    sb_w = sb_heads * HEAD_DIM
    c_slots = cache_c_w128_k.shape[3]
    c_wd = c_slots * HEAD_DIM
    n_groups = len(C_GROUPS)
    depth = w_in_a.shape[0] + w_in_c.shape[0]
    alpha = (2 * depth) ** 0.25
    tp = n_pr * seq
    ts = n_dec * dec_seq
    win_k = (cache_c_w128_k, cache_c_w512_k, cache_c_w2048_k)
    win_v = (cache_c_w128_v, cache_c_w512_v, cache_c_w2048_v)
    row = lambda v: v.reshape(1, -1)

    yp = x_prompt.reshape(tp, d_model)
    ys = x_sample.reshape(ts, d_model)
    outs = {}

    i = 0
    w = w_in_a[i]
    c0, c1, c2 = d_inner, d_inner + conv_dim, d_inner + conv_dim + ssm_heads
    w_re = jnp.concatenate([w[:, c0:c1], w[:, :c0], w[:, c2:], w[:, c1:c2],
                            jnp.zeros((d_model, LANES - ssm_heads), F32)], axis=1).astype(BF16)
    o_q = conv_dim + d_inner
    proj_p = _matmul(yp, w_re, 512, "in_proj_a_prompt")
    proj_s = _matmul(ys, w_re, ts, "in_proj_a_sample")

    pad_l = LANES - ssm_heads
    ssd_par = (conv_w[i], row(conv_b[i]), row(jnp.pad(dt_bias[i], (0, pad_l))),
               row(jnp.pad(-jnp.exp(a_log[i]), (0, pad_l))), row(jnp.repeat(d_skip[i], p)), row(ssm_norm_w[i]))
    ya_p, ssm_p = _ssd(proj_p.reshape(n_pr, seq, -1), jnp.zeros((n_pr, CONV_W - 1, conv_dim), F32),
                       jnp.zeros((n_pr, ssm_heads, p, n_state), F32), *ssd_par, SSD_CHUNK, "ssd_prompt")
    proj_s3 = proj_s.reshape(n_dec, dec_seq, -1)
    proj_s_pad = jnp.pad(proj_s3, ((0, 0), (0, SSD_CHUNK - dec_seq), (0, 0)))
    ya_s, ssm_s = _ssd(proj_s_pad, state_conv[i], state_ssm[i], *ssd_par, dec_seq, "ssd_sample")
    ya_p = ya_p.reshape(tp, d_inner)
    ya_s = ya_s[:, :dec_seq].reshape(ts, d_inner)
    conv_p = proj_p.reshape(n_pr, seq, -1)[:, seq - (CONV_W - 1):, :conv_dim]
    conv_s = jnp.concatenate([state_conv[i], proj_s3[:, :, :conv_dim]], axis=1)[:, dec_seq:]

    tq = 256
    seglen = tq // SB_SEGMENTS
    nkb = seq // tq
    q_p = proj_p[:, o_q:o_q + sb_w].reshape(n_pr, seq, sb_heads, HEAD_DIM)
    k_p = proj_p[:, o_q + sb_w:o_q + 2 * sb_w].reshape(n_pr, seq, sb_heads, HEAD_DIM)
    v_p = proj_p[:, o_q + 2 * sb_w:o_q + 3 * sb_w].reshape(n_pr, seq, sb_heads, HEAD_DIM)
    qt = (q_p * (HEAD_DIM ** -0.5)).astype(BF16).transpose(0, 2, 3, 1)
    kperm = (k_p.astype(BF16).reshape(n_pr, nkb, SB_SEGMENTS, seglen, sb_heads, HEAD_DIM)
             .transpose(0, 4, 1, 3, 2, 5).reshape(n_pr, sb_heads, nkb, tq, HEAD_DIM))
    vperm = (v_p.astype(BF16).reshape(n_pr, nkb, SB_SEGMENTS, seglen, sb_heads, HEAD_DIM)
             .transpose(0, 4, 1, 5, 3, 2).reshape(n_pr, sb_heads, nkb, HEAD_DIM, tq))
    ot = _sb_prompt(sb_bias[i], qt, kperm, vperm, tq, 2, "sb_prompt")
    o_p = ot.transpose(0, 3, 1, 2).reshape(tp, sb_w)

    q_s = proj_s[:, o_q:o_q + sb_w].reshape(n_dec, dec_seq, sb_heads, HEAD_DIM)
    k_s = proj_s[:, o_q + sb_w:o_q + 2 * sb_w].reshape(n_dec, dec_seq, sb_w)
    v_s = proj_s[:, o_q + 2 * sb_w:o_q + 3 * sb_w].reshape(n_dec, dec_seq, sb_w)
    qbd = _block_diag_q(q_s, HEAD_DIM ** -0.5)
    bias_rows = jnp.repeat(sb_bias[i], dec_seq).reshape(sb_heads * dec_seq, 1)
    pad_new = ((0, 0), (0, page - dec_seq), (0, 0))
    o_s = _sb_sample(page_table, qbd, bias_rows, jnp.pad(k_s, pad_new), jnp.pad(v_s, pad_new),
                     cache_sb_k[i].reshape(-1, page, sb_w), cache_sb_v[i].reshape(-1, page, sb_w),
                     16, dec_seq, "sb_sample")
    o_s = _diag_blocks(o_s, dec_seq, sb_heads, HEAD_DIM)

    wo = w_out_a[i].astype(BF16)
    ln_g, ln_b = row(ln_a_mix_g[i]), row(ln_a_mix_b[i])
    yp = _proj_ln(ya_p, o_p, wo[:d_inner], wo[d_inner:], yp, ln_g, ln_b, alpha, 512, "out_proj_a_prompt")
    ys = _proj_ln(ya_s, o_s, wo[:d_inner], wo[d_inner:], ys, ln_g, ln_b, alpha, ts, "out_proj_a_sample")
    f1, f3, f2 = ffn_w1[i].astype(BF16), ffn_w3[i].astype(BF16), ffn_w2[i].astype(BF16)
    ln_g, ln_b = row(ln_a_ffn_g[i]), row(ln_a_ffn_b[i])
    d_ff = f1.shape[1]
    tf = 2 * LANES if d_ff % (2 * LANES) == 0 else LANES
    yp = _ffn_ln(yp, f1, f3, f2, ln_g, ln_b, alpha, 1024, tf, "ffn_prompt")
    ys = _ffn_ln(ys, f1, f3, f2, ln_g, ln_b, alpha, ts, tf, "ffn_sample")

    outs["sb_kp"] = k_p[None]
    outs["sb_vp"] = v_p[None]
    outs["sb_ks"] = k_s.reshape(1, n_dec, dec_seq, sb_heads, HEAD_DIM)
    outs["sb_vs"] = v_s.reshape(1, n_dec, dec_seq, sb_heads, HEAD_DIM)
    outs["ssm_p"], outs["ssm_s"], outs["conv_p"], outs["conv_s"] = ssm_p[None], ssm_s[None], conv_p[None], conv_s[None]

    wc = w_in_c[i].astype(BF16)
    gw = n_groups * c_wd
    cos_p, sa_p, sb_p = _rope_tables(jnp.arange(seq))
    cos_s, sa_s, sb_s = _rope_tables(past_len + jnp.arange(ts) % dec_seq)
    qkv_p = _matmul_rope(yp, wc, cos_p, sa_p, sb_p, 512, "in_proj_c_prompt")
    qkv_s = _matmul_rope(ys, wc, cos_s, sa_s, sb_s, ts, "in_proj_c_sample")

    scale = HEAD_DIM ** -0.5
    o_groups_p, l_groups_p, o_groups_s, l_groups_s = [], [], [], []
    qbd_s = None
    for g, (win, dil) in enumerate(C_GROUPS):
        cq, ck, cv = g * c_wd, gw + g * c_wd, 2 * gw + g * c_wd
        ls = seq // dil

        def deint(a, s=1.0):
            a = (a * s).astype(BF16).reshape(n_pr, ls, dil, c_slots, HEAD_DIM)
            return a.transpose(0, 3, 2, 1, 4).reshape(n_pr * c_slots * dil, ls, HEAD_DIM)

        def reint(a):
            a = a.reshape(n_pr, c_slots, dil, ls, HEAD_DIM)
            return a.transpose(0, 3, 2, 1, 4).reshape(tp, c_wd)

        kg_p = qkv_p[:, ck:ck + c_wd]
        vg_p = qkv_p[:, cv:cv + c_wd]
        o_g, l_g = _dil_prompt(deint(qkv_p[:, cq:cq + c_wd], scale), deint(kg_p), deint(vg_p), f"dilated_prompt_{g}")
        o_groups_p.append(reint(o_g))
        l_groups_p.append(reint(l_g))
        keep = min(win, seq)
        outs[f"c_kp{g}"] = kg_p.reshape(n_pr, seq, c_slots, HEAD_DIM)[:, seq - keep:][None]
        outs[f"c_vp{g}"] = vg_p.reshape(n_pr, seq, c_slots, HEAD_DIM)[:, seq - keep:][None]

        qg_s = qkv_s[:, cq:cq + c_wd].reshape(n_dec, dec_seq, c_slots, HEAD_DIM)
        pad_rows = ((0, 0), (0, DIL_BLOCK - dec_seq), (0, 0))
        kn = jnp.pad(qkv_s[:, ck:ck + c_wd].reshape(n_dec, dec_seq, c_wd), pad_rows)
        vn = jnp.pad(qkv_s[:, cv:cv + c_wd].reshape(n_dec, dec_seq, c_wd), pad_rows)
        wlen = win_k[g].shape[2]
        o_g, l_g, k_new, v_new = _dil_sample(_block_diag_q(qg_s, scale), win_k[g][i].reshape(n_dec, wlen, c_wd),
                                             win_v[g][i].reshape(n_dec, wlen, c_wd), kn, vn, dil, dec_seq,
                                             f"dilated_sample_{g}")
        o_groups_s.append(_diag_blocks(o_g, dec_seq, c_slots, HEAD_DIM))
        l_s = l_g[:, :, 0].reshape(n_dec, c_slots, dec_seq).transpose(0, 2, 1)
        l_groups_s.append(jnp.repeat(l_s.reshape(ts, c_slots), HEAD_DIM, axis=1))
        outs[f"c_ks{g}"] = k_new.reshape(1, n_dec, wlen, c_slots, HEAD_DIM)
        outs[f"c_vs{g}"] = v_new.reshape(1, n_dec, wlen, c_slots, HEAD_DIM)

    woc = w_out_c[i].astype(BF16)
    ln_g, ln_b = row(ln_c_mix_g[i]), row(ln_c_mix_b[i])
    yp = _merge_proj_ln(o_groups_p, l_groups_p, woc, yp, ln_g, ln_b, alpha, 512, "out_proj_c_prompt")
    ys = _merge_proj_ln(o_groups_s, l_groups_s, woc, ys, ln_g, ln_b, alpha, ts, "out_proj_c_sample")
    yp, ys = _moe_ln(yp, ys, router_w[i], moe_w1[i], moe_w3[i], moe_w2[i],
                     row(ln_c_ffn_g[i]), row(ln_c_ffn_b[i]), alpha)

    res = [yp.reshape(n_pr, seq, d_model), ys.reshape(n_dec, dec_seq, d_model),
           outs["sb_kp"], outs["sb_vp"], outs["sb_ks"], outs["sb_vs"],
           outs["ssm_p"], outs["ssm_s"], outs["conv_p"], outs["conv_s"]]
    for g in range(n_groups):
        res += [outs[f"c_kp{g}"], outs[f"c_vp{g}"], outs[f"c_ks{g}"], outs[f"c_vs{g}"]]
    return tuple(res)
```

```python
import functools
import math

import numpy as np
import jax
import jax.numpy as jnp
from jax import lax
from jax.experimental import pallas as pl
from jax.experimental.pallas import tpu as pltpu

F32 = jnp.float32
BF16 = jnp.bfloat16
I32 = jnp.int32

HEAD_DIM = 64
SSM_HEAD_DIM = 64
SSM_GROUPS = 2
SSM_STATE = 128
CONV_W = 4
SSD_CHUNK = 128
C_GROUPS = ((128, 1), (512, 4), (2048, 16))
ROT_DIM = HEAD_DIM // 4
ROPE_THETA = 500000.0
TOP_K = 2
LN_EPS = 1e-5
RMS_EPS = 1e-6

LANES = 128
SUBLANES = 8
VMEM_LIMIT_BYTES = 56 * 1024 * 1024


def _params(*semantics):
    return pltpu.CompilerParams(dimension_semantics=semantics, vmem_limit_bytes=VMEM_LIMIT_BYTES)


def _dot(a, b):
    return jnp.dot(a, b, preferred_element_type=F32)


def _dot_nt(a, b):
    return lax.dot_general(a, b, (((1,), (1,)), ((), ())), preferred_element_type=F32)


def _layer_norm(y, g, b):
    mu = jnp.mean(y, axis=-1, keepdims=True)
    d = y - mu
    var = jnp.mean(d * d, axis=-1, keepdims=True)
    return d * lax.rsqrt(var + LN_EPS) * g + b


def _silu(a):
    return a * jax.nn.sigmoid(a)


def _split3(a):
    hi = a.astype(BF16)
    r = a - hi.astype(F32)
    mid = r.astype(BF16)
    lo = (r - mid.astype(F32)).astype(BF16)
    return hi, mid, lo


def _mm_body(x_ref, w_ref, o_ref):
    o_ref[...] = _dot(x_ref[...].astype(BF16), w_ref[...])


def _matmul(x, w, tm, name):
    m, k = x.shape
    n = w.shape[1]
    return pl.pallas_call(
        _mm_body,
        grid=(m // tm,),
        in_specs=[pl.BlockSpec((tm, k), lambda i: (i, 0)), pl.BlockSpec((k, n), lambda i: (0, 0))],
        out_specs=pl.BlockSpec((tm, n), lambda i: (i, 0)),
        out_shape=jax.ShapeDtypeStruct((m, n), F32),
        compiler_params=_params("parallel"),
        name=name,
    )(x, w)


def _mm_rope_body(x_ref, w_ref, cos_ref, sa_ref, sb_ref, o_ref, *, n_rot_tiles):
    y = _dot(x_ref[...].astype(BF16), w_ref[...])
    j = pl.program_id(1)

    @pl.when(j < n_rot_tiles)
    def _():
        cos = cos_ref[...]
        sa = sa_ref[...]
        sb = sb_ref[...]
        for c in range(y.shape[1] // LANES):
            yc = y[:, c * LANES:(c + 1) * LANES]
            o_ref[:, c * LANES:(c + 1) * LANES] = (
                yc * cos + pltpu.roll(yc, LANES - ROT_DIM // 2, 1) * sa + pltpu.roll(yc, ROT_DIM // 2, 1) * sb)

    @pl.when(j >= n_rot_tiles)
    def _():
        o_ref[...] = y


def _matmul_rope(x, w, cos, sa, sb, tm, name):
    m, k = x.shape
    n = w.shape[1]
    tn = n // 3
    nb = cos.shape[0] // tm
    return pl.pallas_call(
        functools.partial(_mm_rope_body, n_rot_tiles=2),
        grid=(m // tm, 3),
        in_specs=[pl.BlockSpec((tm, k), lambda i, j: (i, 0)),
                  pl.BlockSpec((k, tn), lambda i, j: (0, j)),
                  pl.BlockSpec((tm, LANES), lambda i, j: (i % nb, 0)),
                  pl.BlockSpec((tm, LANES), lambda i, j: (i % nb, 0)),
                  pl.BlockSpec((tm, LANES), lambda i, j: (i % nb, 0))],
        out_specs=pl.BlockSpec((tm, tn), lambda i, j: (i, j)),
        out_shape=jax.ShapeDtypeStruct((m, n), F32),
        compiler_params=_params("parallel", "arbitrary"),
        name=name,
    )(x, w, cos, sa, sb)


def _proj_ln_body(*refs, alpha, n_in):
    h_refs, w_refs = refs[:n_in], refs[n_in:2 * n_in]
    r_ref, g_ref, b_ref, o_ref = refs[2 * n_in:]
    m = _dot(h_refs[0][...].astype(BF16), w_refs[0][...])
    for h_ref, w_ref in zip(h_refs[1:], w_refs[1:]):
        m = m + _dot(h_ref[...].astype(BF16), w_ref[...])
    o_ref[...] = _layer_norm(alpha * r_ref[...] + m, g_ref[...], b_ref[...])


def _proj_ln(hs, ws, resid, g, b, alpha, tm, name):
    m, d = resid.shape
    row = lambda i: (i, 0)
    fix = lambda i: (0, 0)
    return pl.pallas_call(
        functools.partial(_proj_ln_body, alpha=alpha, n_in=len(hs)),
        grid=(m // tm,),
        in_specs=[pl.BlockSpec((tm, h.shape[1]), row) for h in hs]
                 + [pl.BlockSpec((w.shape[0], d), fix) for w in ws]
                 + [pl.BlockSpec((tm, d), row), pl.BlockSpec((1, d), fix), pl.BlockSpec((1, d), fix)],
        out_specs=pl.BlockSpec((tm, d), row),
        out_shape=jax.ShapeDtypeStruct((m, d), F32),
        compiler_params=_params("parallel"),
        name=name,
    )(*hs, *ws, resid, g, b)


def _merge_proj_ln_body(o0_ref, o1_ref, o2_ref, l0_ref, l1_ref, l2_ref, w_ref, r_ref, g_ref, b_ref, o_ref, *, alpha):
    l0, l1, l2 = l0_ref[...], l1_ref[...], l2_ref[...]
    mx = jnp.maximum(jnp.maximum(l0, l1), l2)
    e0, e1, e2 = jnp.exp(l0 - mx), jnp.exp(l1 - mx), jnp.exp(l2 - mx)
    o = (e0 * o0_ref[...] + e1 * o1_ref[...] + e2 * o2_ref[...]) / (e0 + e1 + e2)
    m = _dot(o.astype(BF16), w_ref[...])
    o_ref[...] = _layer_norm(alpha * r_ref[...] + m, g_ref[...], b_ref[...])


def _merge_proj_ln(os_, ls_, w, resid, g, b, alpha, tm, name):
    m, d = resid.shape
    k = w.shape[0]
    row = lambda i: (i, 0)
    fix = lambda i: (0, 0)
    return pl.pallas_call(
        functools.partial(_merge_proj_ln_body, alpha=alpha),
        grid=(m // tm,),
        in_specs=[pl.BlockSpec((tm, k), row)] * 6 + [pl.BlockSpec((k, d), fix), pl.BlockSpec((tm, d), row),
                                                     pl.BlockSpec((1, d), fix), pl.BlockSpec((1, d), fix)],
        out_specs=pl.BlockSpec((tm, d), row),
        out_shape=jax.ShapeDtypeStruct((m, d), F32),
        compiler_params=_params("parallel"),
        name=name,
    )(*os_, *ls_, w, resid, g, b)


def _ffn_ln_body(x_ref, w1_ref, w3_ref, w2_ref, g_ref, b_ref, o_ref, xb_ref, acc_ref, *, alpha):
    f = pl.program_id(1)

    @pl.when(f == 0)
    def _():
        xb_ref[...] = x_ref[...].astype(BF16)
        acc_ref[...] = jnp.zeros_like(acc_ref)

    xb = xb_ref[...]
    hid = _silu(_dot(xb, w1_ref[...])) * _dot(xb, w3_ref[...])
    acc_ref[...] += _dot(hid.astype(BF16), w2_ref[...])

    @pl.when(f == pl.num_programs(1) - 1)
    def _():
        o_ref[...] = _layer_norm(alpha * x_ref[...] + acc_ref[...], g_ref[...], b_ref[...])


def _ffn_ln(x, w1, w3, w2, g, b, alpha, tm, tf, name):
    m, d = x.shape
    ff = w1.shape[1]
    return pl.pallas_call(
        functools.partial(_ffn_ln_body, alpha=alpha),
        grid=(m // tm, ff // tf),
        in_specs=[pl.BlockSpec((tm, d), lambda i, f: (i, 0)),
                  pl.BlockSpec((d, tf), lambda i, f: (0, f)),
                  pl.BlockSpec((d, tf), lambda i, f: (0, f)),
                  pl.BlockSpec((tf, d), lambda i, f: (f, 0)),
                  pl.BlockSpec((1, d), lambda i, f: (0, 0)),
                  pl.BlockSpec((1, d), lambda i, f: (0, 0))],
        out_specs=pl.BlockSpec((tm, d), lambda i, f: (i, 0)),
        out_shape=jax.ShapeDtypeStruct((m, d), F32),
        scratch_shapes=[pltpu.VMEM((tm, d), BF16), pltpu.VMEM((tm, d), F32)],
        compiler_params=_params("parallel", "arbitrary"),
        name=name,
    )(x, w1, w3, w2, g, b)


def _ssd_body(xbc_ref, z_ref, dt_ref, conv0_ref, h0_ref, cw_ref, cb_ref, dtb_ref, aneg_ref, dsk_ref, nw_ref,
              y_ref, hn_ref, xpad_ref, *, q, valid_len, n_heads, d_inner):
    c = pl.program_id(1)
    conv_dim = xpad_ref.shape[1]
    gn = SSM_GROUPS * SSM_STATE
    heads_per_group = n_heads // SSM_GROUPS
    p = SSM_HEAD_DIM

    @pl.when(c == 0)
    def _():
        xpad_ref[0:SUBLANES, :] = jnp.zeros((SUBLANES, conv_dim), F32)
        xpad_ref[SUBLANES - (CONV_W - 1):SUBLANES, :] = conv0_ref[...]
        hn_ref[...] = h0_ref[...]

    x_raw = xbc_ref[...]
    xpad_ref[SUBLANES:SUBLANES + q, :] = x_raw
    cw = cw_ref[...]
    acc = cb_ref[...] + x_raw * cw[CONV_W - 1:CONV_W]
    for j in range(CONV_W - 1):
        off = SUBLANES - (CONV_W - 1) + j
        acc = acc + xpad_ref[off:off + q, :] * cw[j:j + 1]
    xpad_ref[SUBLANES - (CONV_W - 1):SUBLANES, :] = x_raw[q - (CONV_W - 1):q, :]
    xc = _silu(acc)
    xs = xc[:, :d_inner]
    bm = xc[:, d_inner:d_inner + gn].astype(BF16)
    cm = xc[:, d_inner + gn:d_inner + 2 * gn].astype(BF16)

    t = dt_ref[...] + dtb_ref[...]
    dtv = jnp.maximum(t, 0.0) + jnp.log(1.0 + jnp.exp(-jnp.abs(t)))
    rows = lax.broadcasted_iota(I32, (q, q), 0)
    cols = lax.broadcasted_iota(I32, (q, q), 1)
    if valid_len < q:
        dtv = jnp.where(lax.broadcasted_iota(I32, dtv.shape, 0) < valid_len, dtv, 0.0)
    a = dtv * aneg_ref[...]
    causal = cols <= rows
    tri = jnp.where(causal, 1.0, 0.0).astype(BF16)
    a_hi, a_mid, a_lo = _split3(a)
    a_cum = _dot(tri, a_hi) + _dot(tri, a_mid) + _dot(tri, a_lo)
    a_cum_t = a_cum.T
    a_tot = a_cum[q - 1:q, :]

    ys = []
    xws = []
    for g in range(SSM_GROUPS):
        cg = cm[:, g * SSM_STATE:(g + 1) * SSM_STATE]
        bg = bm[:, g * SSM_STATE:(g + 1) * SSM_STATE]
        scores = _dot_nt(cg, bg)
        for r in range(heads_per_group):
            h = g * heads_per_group + r
            col = a_cum[:, h:h + 1]
            row = a_cum_t[h:h + 1, :]
            decay_ls = jnp.where(causal, jnp.exp(col - row), 0.0)
            w_diag = (scores * decay_ls).astype(BF16)
            xs_h = xs[:, h * p:(h + 1) * p]
            xdt = xs_h * dtv[:, h:h + 1]
            y_diag = _dot(w_diag, xdt.astype(BF16))
            y_off = _dot_nt(cg, hn_ref[h].astype(BF16)) * jnp.exp(col)
            ys.append(y_diag + y_off)
            xws.append(xdt * jnp.exp(a_tot[:, h:h + 1] - col))
    y = jnp.concatenate(ys, axis=1) + dsk_ref[...] * xs
    xw = jnp.concatenate(xws, axis=1)

    heads_per_tile = LANES // p
    for j in range(d_inner // LANES):
        xw_t = xw[:, j * LANES:(j + 1) * LANES].T.astype(BF16)
        for hh in range(heads_per_tile):
            h = j * heads_per_tile + hh
            g = h // heads_per_group
            bg = bm[:, g * SSM_STATE:(g + 1) * SSM_STATE]
            st = _dot(xw_t[hh * p:(hh + 1) * p, :], bg)
            hn_ref[h] = jnp.exp(a_tot[:, h:h + 1]) * hn_ref[h] + st

    zz = z_ref[...]
    yg = y * _silu(zz)
    gsz = d_inner // SSM_GROUPS
    outs = []
    for g in range(SSM_GROUPS):
        blk = yg[:, g * gsz:(g + 1) * gsz]
        ms = jnp.mean(blk * blk, axis=-1, keepdims=True)
        outs.append(blk * lax.rsqrt(ms + RMS_EPS))
    y_ref[...] = jnp.concatenate(outs, axis=1) * nw_ref[...]


def _ssd(proj3, conv0, h0, cw, cb, dtb, aneg, dsk, nw, valid_len, name):
    b, length, _ = proj3.shape
    n_heads, p, n = h0.shape[1], h0.shape[2], h0.shape[3]
    d_inner = n_heads * p
    conv_dim = cw.shape[1]
    q = SSD_CHUNK
    z_blk = conv_dim // d_inner
    dt_blk = (proj3.shape[2] - LANES) // LANES
    fix2 = lambda i, c: (0, 0)
    return pl.pallas_call(
        functools.partial(_ssd_body, q=q, valid_len=valid_len, n_heads=n_heads, d_inner=d_inner),
        grid=(b, length // q),
        in_specs=[pl.BlockSpec((None, q, conv_dim), lambda i, c: (i, c, 0)),
                  pl.BlockSpec((None, q, d_inner), lambda i, c: (i, c, z_blk)),
                  pl.BlockSpec((None, q, LANES), lambda i, c: (i, c, dt_blk)),
                  pl.BlockSpec((None, CONV_W - 1, conv_dim), lambda i, c: (i, 0, 0)),
                  pl.BlockSpec((None, n_heads, p, n), lambda i, c: (i, 0, 0, 0)),
                  pl.BlockSpec((CONV_W, conv_dim), fix2),
                  pl.BlockSpec((1, conv_dim), fix2),
                  pl.BlockSpec((1, LANES), fix2),
                  pl.BlockSpec((1, LANES), fix2),
                  pl.BlockSpec((1, d_inner), fix2),
                  pl.BlockSpec((1, d_inner), fix2)],
        out_specs=[pl.BlockSpec((None, q, d_inner), lambda i, c: (i, c, 0)),
                   pl.BlockSpec((None, n_heads, p, n), lambda i, c: (i, 0, 0, 0))],
        out_shape=[jax.ShapeDtypeStruct((b, length, d_inner), F32),
                   jax.ShapeDtypeStruct((b, n_heads, p, n), F32)],
        scratch_shapes=[pltpu.VMEM((SUBLANES + q, conv_dim), F32)],
        compiler_params=_params("parallel", "arbitrary"),
        name=name,
    )(proj3, proj3, proj3, conv0, h0, cw, cb, dtb, aneg, dsk, nw)


SB_SEGMENTS = SUBLANES


def _sb_weights(z, carry, valid):
    tk, tq = z.shape
    seglen = tk // SB_SEGMENTS
    fail = 1.0 / (1.0 + jnp.exp(z))
    if valid is not None:
        fail = jnp.where(valid, fail, 1.0)
    fail3 = fail.reshape(seglen, SB_SEGMENTS, tq)
    prods = [None] * (seglen + 1)
    prods[seglen] = jnp.ones((SB_SEGMENTS, tq), F32)
    for v in reversed(range(seglen)):
        prods[v] = prods[v + 1] * fail3[v]
    offs = [None] * SB_SEGMENTS
    tot = carry
    for s in reversed(range(SB_SEGMENTS)):
        offs[s] = tot
        tot = tot * prods[0][s:s + 1, :]
    offs = jnp.concatenate(offs, axis=0)
    w3 = jnp.stack([prods[v + 1] - prods[v] for v in range(seglen)], axis=0) * offs[None]
    return w3.reshape(tk, tq), tot


def _sb_prompt_body(bias_ref, q_ref, k_ref, v_ref, o_ref, kp_ref, vt_ref, *, tq, hb, d):
    i = pl.program_id(2)
    h0 = pl.program_id(1) * hb
    seglen = tq // SB_SEGMENTS
    nkb = kp_ref.shape[0]
    r = lax.broadcasted_iota(I32, (tq, tq), 0)
    key_pos = (r % SB_SEGMENTS) * seglen + r // SB_SEGMENTS
    strictly_before = key_pos < lax.broadcasted_iota(I32, (tq, tq), 1)

    @pl.when(i == 0)
    def _():
        def permute(kb, c):
            base = pl.multiple_of(kb * tq, tq)
            kp = [k_ref[pl.ds(base + v, SB_SEGMENTS, stride=seglen), :] for v in range(seglen)]
            vp = [v_ref[pl.ds(base + v, SB_SEGMENTS, stride=seglen), :] for v in range(seglen)]
            kp_ref[kb] = jnp.concatenate(kp, axis=0).astype(BF16)
            vt_ref[kb] = jnp.concatenate(vp, axis=0).T.astype(BF16)
            return c

        lax.fori_loop(0, nkb, permute, 0)

    qt_all = (q_ref[...] * (d ** -0.5)).T
    head_of_row = lax.broadcasted_iota(I32, qt_all.shape, 0) // d
    qts = [jnp.where(head_of_row == hh, qt_all, 0.0).astype(BF16) for hh in range(hb)]

    def logits(hh, kb):
        return _dot(kp_ref[kb], qts[hh]) + bias_ref[h0 + hh]

    def values(hh, kb):
        return vt_ref[kb, hh * d:(hh + 1) * d, :]

    state = []
    for hh in range(hb):
        w, carry = _sb_weights(logits(hh, i), jnp.ones((1, tq), F32), strictly_before)
        state.append((logits(hh, jnp.maximum(i - 1, 0)), w.astype(BF16), carry, jnp.zeros((d, tq), F32)))

    def step(it, st):
        kb = i - 1 - it
        out = []
        for hh in range(hb):
            z, w_prev, carry, acc = st[hh]
            acc = acc + _dot(values(hh, kb + 1), w_prev)
            z_next = logits(hh, jnp.maximum(kb - 1, 0))
            w, carry = _sb_weights(z, carry, None)
            out.append((z_next, w.astype(BF16), carry, acc))
        return tuple(out)

    state = lax.fori_loop(0, i, step, tuple(state))
    out_t = [state[hh][3] + _dot(values(hh, 0), state[hh][1]) for hh in range(hb)]
    o_ref[...] = jnp.concatenate(out_t, axis=0).T


def _sb_prompt(bias, proj3, q_col, k_col, v_col, n_heads, d, tq, name):
    b, length, _ = proj3.shape
    hb = LANES // d
    nkb = length // tq
    qc, kc, vc = q_col // LANES, k_col // LANES, v_col // LANES
    return pl.pallas_call(
        functools.partial(_sb_prompt_body, tq=tq, hb=hb, d=d),
        grid=(b, n_heads // hb, nkb),
        in_specs=[pl.BlockSpec(memory_space=pltpu.SMEM),
                  pl.BlockSpec((None, tq, LANES), lambda bi, hi, i: (bi, i, qc + hi)),
                  pl.BlockSpec((None, length, LANES), lambda bi, hi, i: (bi, 0, kc + hi)),
                  pl.BlockSpec((None, length, LANES), lambda bi, hi, i: (bi, 0, vc + hi))],
        out_specs=pl.BlockSpec((None, tq, LANES), lambda bi, hi, i: (bi, i, hi)),
        out_shape=jax.ShapeDtypeStruct((b, length, n_heads * d), F32),
        scratch_shapes=[pltpu.VMEM((nkb, tq, LANES), BF16), pltpu.VMEM((nkb, LANES, tq), BF16)],
        compiler_params=_params("parallel", "parallel", "arbitrary"),
        name=name,
    )(bias, proj3, proj3, proj3)


def _sb_sample_body(pt_ref, qbd_ref, bias_ref, knew_ref, vnew_ref, *refs, pages_per_step, n_new):
    k_refs = refs[:pages_per_step]
    v_refs = refs[pages_per_step:2 * pages_per_step]
    o_ref = refs[2 * pages_per_step]
    carry_ref = refs[2 * pages_per_step + 1]
    j = pl.program_id(1)
    qbd = qbd_ref[...]
    bias = bias_ref[...]
    nr = qbd.shape[0]
    tk = knew_ref.shape[0]
    kk = lax.broadcasted_iota(I32, (tk, tk), 0)
    ss = lax.broadcasted_iota(I32, (tk, tk), 1)
    upper = jnp.where(kk > ss, 1.0, 0.0).astype(BF16)

    def block(k, v, valid):
        z = _dot_nt(qbd, k.astype(BF16)) + bias
        sp = jnp.maximum(z, 0.0) + jnp.log(1.0 + jnp.exp(-jnp.abs(z)))
        log_fail = -sp
        log_beta = z - sp
        if valid is not None:
            log_fail = jnp.where(valid, log_fail, 0.0)
        lf_hi = log_fail.astype(BF16)
        lf_lo = (log_fail - lf_hi.astype(F32)).astype(BF16)
        tail = _dot(lf_hi, upper) + _dot(lf_lo, upper) + carry_ref[...]
        w = jnp.exp(log_beta + tail)
        if valid is not None:
            w = jnp.where(valid, w, 0.0)
        carry_ref[...] += jnp.sum(log_fail, axis=-1, keepdims=True)
        o_ref[...] += _dot(w.astype(BF16), v.astype(BF16))

    @pl.when(j == 0)
    def _():
        carry_ref[...] = jnp.zeros_like(carry_ref)
        o_ref[...] = jnp.zeros_like(o_ref)
        qi = lax.broadcasted_iota(I32, (nr, tk), 0) % n_new
        jn = lax.broadcasted_iota(I32, (nr, tk), 1)
        block(knew_ref[...], vnew_ref[...], jn < qi)

    for p in range(pages_per_step):
        block(k_refs[p][...], v_refs[p][...], None)


def _sb_sample(page_table, qbd, bias_rows, knew, vnew, cache_k, cache_v, pages_per_step, n_new, name):
    n_seq, n_pages = page_table.shape
    nr, hd = qbd.shape[1], qbd.shape[2]
    page = cache_k.shape[1]
    steps = n_pages // pages_per_step

    def page_spec(p):
        return pl.BlockSpec((None, page, hd), lambda s, j, pt: (pt[s, n_pages - 1 - (j * pages_per_step + p)], 0, 0))

    seq3 = lambda s, j, pt: (s, 0, 0)
    grid_spec = pltpu.PrefetchScalarGridSpec(
        num_scalar_prefetch=1,
        grid=(n_seq, steps),
        in_specs=[pl.BlockSpec((None, nr, hd), seq3),
                  pl.BlockSpec((nr, 1), lambda s, j, pt: (0, 0)),
                  pl.BlockSpec((None, page, hd), seq3),
                  pl.BlockSpec((None, page, hd), seq3)]
                 + [page_spec(p) for p in range(pages_per_step)] * 2,
        out_specs=pl.BlockSpec((None, nr, hd), seq3),
        scratch_shapes=[pltpu.VMEM((nr, 1), F32)],
    )
    return pl.pallas_call(
        functools.partial(_sb_sample_body, pages_per_step=pages_per_step, n_new=n_new),
        grid_spec=grid_spec,
        out_shape=jax.ShapeDtypeStruct((n_seq, nr, hd), F32),
        compiler_params=_params("parallel", "arbitrary"),
        name=name,
    )(page_table, qbd, bias_rows, knew, vnew, *([cache_k] * pages_per_step), *([cache_v] * pages_per_step))


DIL_BLOCK = 128


def _dil_unit(start, prev_in_block, prev_ok, dil, q_ref, kc_ref, kp_ref, vc_ref, vp_ref, og_ref, lg_ref, d, scale):
    t = DIL_BLOCK
    rows = q_ref.shape[0]
    span = t * dil

    def idx(s0):
        return pl.ds(s0, t) if dil == 1 else pl.ds(s0, t, stride=dil)

    first_slot = lax.broadcasted_iota(I32, (t, LANES), 1) < d
    u = lax.broadcasted_iota(I32, (2 * t, 2 * t), 0) % t
    c = lax.broadcasted_iota(I32, (2 * t, 2 * t), 1)
    band_prev = jnp.logical_and(c < t, c >= u)
    band_cur = jnp.logical_and(c >= t, c - t <= u)
    if prev_ok is not None:
        band_prev = jnp.logical_and(band_prev, prev_ok)

    q2 = q_ref[idx(start), :] * scale
    if prev_in_block:
        k_prev, v_prev = kc_ref[idx(start - span), :], vc_ref[idx(start - span), :]
    else:
        k_prev, v_prev = kp_ref[idx(start + rows - span), :], vp_ref[idx(start + rows - span), :]
    qq = jnp.concatenate([jnp.where(first_slot, q2, 0.0), jnp.where(first_slot, 0.0, q2)], axis=0)
    kk = jnp.concatenate([k_prev, kc_ref[idx(start), :]], axis=0).astype(BF16)
    vv = jnp.concatenate([v_prev, vc_ref[idx(start), :]], axis=0).astype(BF16)
    s = jnp.where(jnp.logical_or(band_prev, band_cur), _dot_nt(qq.astype(BF16), kk), -jnp.inf)
    mx = jnp.max(s, axis=-1, keepdims=True)
    p = jnp.exp(s - mx)
    den = jnp.sum(p, axis=-1, keepdims=True)
    on = _dot(p.astype(BF16), vv) / den
    lse = jnp.broadcast_to(mx + jnp.log(den), on.shape)
    og_ref[idx(start), :] = jnp.where(first_slot, on[:t], on[t:])
    lg_ref[idx(start), :] = jnp.where(first_slot, lse[:t], lse[t:])


def _pairwise_loop(count, fn):
    def pair(m, carry):
        fn(2 * m)
        fn(2 * m + 1)
        return carry

    if count // 2:
        lax.fori_loop(0, count // 2, pair, 0)
    if count % 2:
        fn(count - 1)


def _dil_prompt_body(*refs, dilations, d, scale):
    n_g = len(dilations)
    ins = refs[:5 * n_g]
    o_ref = refs[5 * n_g]
    og_refs = refs[5 * n_g + 1:5 * n_g + 1 + n_g]
    lg_refs = refs[5 * n_g + 1 + n_g:]
    j = pl.program_id(2)
    rows = o_ref.shape[0]

    for g, dil in enumerate(dilations):
        unit = functools.partial(_dil_unit, dil=dil, q_ref=ins[5 * g], kc_ref=ins[5 * g + 1], kp_ref=ins[5 * g + 2],
                                 vc_ref=ins[5 * g + 3], vp_ref=ins[5 * g + 4], og_ref=og_refs[g],
                                 lg_ref=lg_refs[g], d=d, scale=scale)
        span = DIL_BLOCK * dil

        def first(r, unit=unit):
            unit(r, False, j > 0)

        def rest(n, unit=unit, dil=dil, span=span):
            start = n % dil + (n // dil + 1) * span
            if dil == 1:
                start = pl.multiple_of(start, DIL_BLOCK)
            unit(start, True, None)

        _pairwise_loop(dil, first)
        _pairwise_loop((rows // span - 1) * dil, rest)

    ls_ = [lg_ref[...] for lg_ref in lg_refs]
    mx = functools.reduce(jnp.maximum, ls_)
    es = [jnp.exp(l - mx) for l in ls_]
    num = functools.reduce(lambda a, b: a + b, [e * og_ref[...] for e, og_ref in zip(es, og_refs)])
    o_ref[...] = num / functools.reduce(lambda a, b: a + b, es)


def _dil_prompt(qkv3, dilations, n_slots, d, name):
    b, length, _ = qkv3.shape
    n_g = len(dilations)
    rows = DIL_BLOCK * max(dilations)
    sb = LANES // d
    gw = n_g * n_slots * d
    cur = lambda bi, si, j, c0: (bi, j, c0 + si)
    prev = lambda bi, si, j, c0: (bi, jnp.maximum(j - 1, 0), c0 + si)
    in_specs = []
    for g in range(n_g):
        qc, kc, vc = (g * n_slots * d) // LANES, (gw + g * n_slots * d) // LANES, (2 * gw + g * n_slots * d) // LANES
        for fn, c0 in ((cur, qc), (cur, kc), (prev, kc), (cur, vc), (prev, vc)):
            in_specs.append(pl.BlockSpec((None, rows, LANES), functools.partial(fn, c0=c0)))
    return pl.pallas_call(
        functools.partial(_dil_prompt_body, dilations=tuple(dilations), d=d, scale=d ** -0.5),
        grid=(b, n_slots // sb, length // rows),
        in_specs=in_specs,
        out_specs=pl.BlockSpec((None, rows, LANES), lambda bi, si, j: (bi, j, si)),
        out_shape=jax.ShapeDtypeStruct((b, length, n_slots * d), F32),
        scratch_shapes=[pltpu.VMEM((rows, LANES), F32)] * (2 * n_g),
        compiler_params=_params("parallel", "parallel", "arbitrary"),
        name=name,
    )(*([qkv3] * len(in_specs)))


def _dil_sample_body(qbd_ref, kbuf_ref, vbuf_ref, knew_ref, vnew_ref, o_ref, l_ref, kout_ref, vout_ref,
                     *, dil, n_new):
    w = kbuf_ref.shape[0]
    nr = qbd_ref.shape[0]
    pad = knew_ref.shape[0]
    kb = kbuf_ref[...]
    vb = vbuf_ref[...]
    kn = knew_ref[...]
    vn = vnew_ref[...]
    kout_ref[0:w - n_new, :] = kb[n_new:w, :]
    kout_ref[w - n_new:w, :] = kn[0:n_new, :]
    vout_ref[0:w - n_new, :] = vb[n_new:w, :]
    vout_ref[w - n_new:w, :] = vn[0:n_new, :]

    qbd = qbd_ref[...]
    qi = lax.broadcasted_iota(I32, (nr, w), 0) % n_new
    diff = w + qi - lax.broadcasted_iota(I32, (nr, w), 1)
    ok_buf = jnp.logical_and(diff % dil == 0, diff <= dil * DIL_BLOCK)
    qi_n = lax.broadcasted_iota(I32, (nr, pad), 0) % n_new
    jn = lax.broadcasted_iota(I32, (nr, pad), 1)
    diff_n = qi_n - jn
    ok_new = jnp.logical_and(jnp.logical_and(diff_n >= 0, diff_n % dil == 0), jn < n_new)
    s_buf = jnp.where(ok_buf, _dot_nt(qbd, kb.astype(BF16)), -jnp.inf)
    s_new = jnp.where(ok_new, _dot_nt(qbd, kn.astype(BF16)), -jnp.inf)
    mx = jnp.maximum(jnp.max(s_buf, axis=-1, keepdims=True), jnp.max(s_new, axis=-1, keepdims=True))
    p_buf = jnp.exp(s_buf - mx)
    p_new = jnp.exp(s_new - mx)
    den = jnp.sum(p_buf, axis=-1, keepdims=True) + jnp.sum(p_new, axis=-1, keepdims=True)
    acc = _dot(p_buf.astype(BF16), vb.astype(BF16)) + _dot(p_new.astype(BF16), vn.astype(BF16))
    o_ref[...] = acc / den
    l_ref[...] = jnp.broadcast_to(mx + jnp.log(den), l_ref.shape)


def _dil_sample(qbd, kbuf, vbuf, knew, vnew, dil, n_new, name):
    n_seq, w, hd = kbuf.shape
    nr = qbd.shape[1]
    pad = knew.shape[1]
    s3 = lambda s: (s, 0, 0)
    return pl.pallas_call(
        functools.partial(_dil_sample_body, dil=dil, n_new=n_new),
        grid=(n_seq,),
        in_specs=[pl.BlockSpec((None, nr, hd), s3), pl.BlockSpec((None, w, hd), s3), pl.BlockSpec((None, w, hd), s3),
                  pl.BlockSpec((None, pad, hd), s3), pl.BlockSpec((None, pad, hd), s3)],
        out_specs=[pl.BlockSpec((None, nr, hd), s3), pl.BlockSpec((None, nr, LANES), s3),
                   pl.BlockSpec((None, w, hd), s3), pl.BlockSpec((None, w, hd), s3)],
        out_shape=[jax.ShapeDtypeStruct((n_seq, nr, hd), F32), jax.ShapeDtypeStruct((n_seq, nr, LANES), F32),
                   jax.ShapeDtypeStruct((n_seq, w, hd), F32), jax.ShapeDtypeStruct((n_seq, w, hd), F32)],
        compiler_params=_params("parallel"),
        name=name,
    )(qbd, kbuf, vbuf, knew, vnew)


def _router_body(x_ref, w_ref, idx_ref, gate_ref, *, n_experts):
    xh, xm, _ = _split3(x_ref[...])
    wh, wm, _ = _split3(w_ref[...])
    logits = _dot(xh, wh) + _dot(xm, wh) + _dot(xh, wm)
    lane = lax.broadcasted_iota(I32, logits.shape, 1).astype(F32)
    logits = jnp.where(lane < n_experts, logits, -jnp.inf)
    m1 = jnp.max(logits, axis=-1, keepdims=True)
    i1 = jnp.min(jnp.where(logits == m1, lane, float(LANES)), axis=-1, keepdims=True)
    rest = jnp.where(lane == i1, -jnp.inf, logits)
    m2 = jnp.max(rest, axis=-1, keepdims=True)
    i2 = jnp.min(jnp.where(rest == m2, lane, float(LANES)), axis=-1, keepdims=True)
    e = jnp.exp(m2 - m1)
    g1 = 1.0 / (1.0 + e)
    idx_ref[...] = jnp.where(lane == 0.0, i1, jnp.where(lane == 1.0, i2, 0.0)).astype(I32)
    gate_ref[...] = jnp.where(lane == 0.0, g1, jnp.where(lane == 1.0, e * g1, 0.0))


def _router(x, w_pad, n_experts, tm, name):
    m, d = x.shape
    return pl.pallas_call(
        functools.partial(_router_body, n_experts=n_experts),
        grid=(m // tm,),
        in_specs=[pl.BlockSpec((tm, d), lambda i: (i, 0)), pl.BlockSpec((d, LANES), lambda i: (0, 0))],
        out_specs=[pl.BlockSpec((tm, LANES), lambda i: (i, 0))] * 2,
        out_shape=[jax.ShapeDtypeStruct((m, LANES), I32), jax.ShapeDtypeStruct((m, LANES), F32)],
        compiler_params=_params("parallel"),
        name=name,
    )(x, w_pad)


def _row_copy(src_hbm, row, dst, r, sem):
    return pltpu.make_async_copy(src_hbm.at[pl.ds(row, 1)], dst.at[pl.ds(r, 1)], sem)


def _gather_body(src_ref, x_hbm, o_ref, buf, sem, *, rows):
    i = pl.program_id(0)

    def copies(step, method):
        slot = step % 2

        def one(r, c):
            getattr(_row_copy(x_hbm, src_ref[step * rows + r], buf.at[slot], r, sem.at[slot]), method)()
            return c

        lax.fori_loop(0, rows, one, 0)

    @pl.when(i == 0)
    def _():
        copies(i, "start")

    @pl.when(i + 1 < pl.num_programs(0))
    def _():
        copies(i + 1, "start")

    copies(i, "wait")
    o_ref[...] = buf[i % 2].astype(BF16)


def _gather_rows(src, x, rows, name):
    n = src.shape[0]
    d = x.shape[1]
    grid_spec = pltpu.PrefetchScalarGridSpec(
        num_scalar_prefetch=1,
        grid=(n // rows,),
        in_specs=[pl.BlockSpec(memory_space=pl.ANY)],
        out_specs=pl.BlockSpec((rows, d), lambda i, s: (i, 0)),
        scratch_shapes=[pltpu.VMEM((2, rows, d), x.dtype), pltpu.SemaphoreType.DMA((2,))],
    )
    return pl.pallas_call(
        functools.partial(_gather_body, rows=rows),
        grid_spec=grid_spec,
        out_shape=jax.ShapeDtypeStruct((n, d), BF16),
        compiler_params=_params("arbitrary"),
        name=name,
    )(src, x)


def _experts_body(te_ref, tb_ref, nu_ref, x_ref, w1_ref, w3_ref, w2_ref, o_ref, acc_ref):
    i = pl.program_id(0)
    f = pl.program_id(1)

    @pl.when(i < nu_ref[0])
    def _():
        @pl.when(f == 0)
        def _():
            acc_ref[...] = jnp.zeros_like(acc_ref)

        x = x_ref[...]
        hid = _silu(_dot(x, w1_ref[...].astype(BF16))) * _dot(x, w3_ref[...].astype(BF16))
        acc_ref[...] += _dot(hid.astype(BF16), w2_ref[...].astype(BF16))

        @pl.when(f == pl.num_programs(1) - 1)
        def _():
            o_ref[...] = acc_ref[...]

    @pl.when(jnp.logical_and(i >= nu_ref[0], f == pl.num_programs(1) - 1))
    def _():
        o_ref[...] = jnp.zeros_like(o_ref)


def _experts(tile_expert, tile_block, n_used, xs, w1, w3, w2, tm, tf, name):
    n_tiles = tile_expert.shape[0]
    d = xs.shape[1]
    ff = w1.shape[2]
    nf = ff // tf

    def fcol(i, f, nu):
        return jnp.where(i < nu[0], f, nf - 1)

    grid_spec = pltpu.PrefetchScalarGridSpec(
        num_scalar_prefetch=3,
        grid=(n_tiles, nf),
        in_specs=[pl.BlockSpec((tm, d), lambda i, f, te, tb, nu: (tb[i], 0)),
                  pl.BlockSpec((None, d, tf), lambda i, f, te, tb, nu: (te[i], 0, fcol(i, f, nu))),
                  pl.BlockSpec((None, d, tf), lambda i, f, te, tb, nu: (te[i], 0, fcol(i, f, nu))),
                  pl.BlockSpec((None, tf, d), lambda i, f, te, tb, nu: (te[i], fcol(i, f, nu), 0))],
        out_specs=pl.BlockSpec((tm, d), lambda i, f, te, tb, nu: (i, 0)),
        scratch_shapes=[pltpu.VMEM((tm, d), F32)],
    )
    return pl.pallas_call(
        _experts_body,
        grid_spec=grid_spec,
        out_shape=jax.ShapeDtypeStruct((n_tiles * tm, d), F32),
        compiler_params=_params("arbitrary", "arbitrary"),
        name=name,
    )(tile_expert, tile_block, n_used, xs, w1, w3, w2)


def _combine_ln_body(slot_ref, y_hbm, x_ref, gate_ref, g_ref, b_ref, o_ref, buf1, buf2, sem, *, rows, t0, alpha):
    base = (pl.program_id(0) * rows + t0) * TOP_K

    def start(r, c):
        _row_copy(y_hbm, slot_ref[base + TOP_K * r], buf1, r, sem).start()
        _row_copy(y_hbm, slot_ref[base + TOP_K * r + 1], buf2, r, sem).start()
        return c

    def wait(r, c):
        _row_copy(y_hbm, slot_ref[base + TOP_K * r], buf1, r, sem).wait()
        _row_copy(y_hbm, slot_ref[base + TOP_K * r + 1], buf2, r, sem).wait()
        return c

    lax.fori_loop(0, rows, start, 0)
    lax.fori_loop(0, rows, wait, 0)
    gate = gate_ref[...]
    moe = gate[:, 0:1] * buf1[...] + gate[:, 1:2] * buf2[...]
    o_ref[...] = _layer_norm(alpha * x_ref[...] + moe, g_ref[...], b_ref[...])


def _combine_ln(slot, ys, x, gate, g, b, t0, alpha, rows, name):
    m, d = x.shape
    gb = t0 // rows
    grid_spec = pltpu.PrefetchScalarGridSpec(
        num_scalar_prefetch=1,
        grid=(m // rows,),
        in_specs=[pl.BlockSpec(memory_space=pl.ANY),
                  pl.BlockSpec((rows, d), lambda i, s: (i, 0)),
                  pl.BlockSpec((rows, LANES), lambda i, s: (i + gb, 0)),
                  pl.BlockSpec((1, d), lambda i, s: (0, 0)),
                  pl.BlockSpec((1, d), lambda i, s: (0, 0))],
        out_specs=pl.BlockSpec((rows, d), lambda i, s: (i, 0)),
        scratch_shapes=[pltpu.VMEM((rows, d), F32), pltpu.VMEM((rows, d), F32), pltpu.SemaphoreType.DMA(())],
    )
    return pl.pallas_call(
        functools.partial(_combine_ln_body, rows=rows, t0=t0, alpha=alpha),
        grid_spec=grid_spec,
        out_shape=jax.ShapeDtypeStruct((m, d), F32),
        compiler_params=_params("arbitrary"),
        name=name,
    )(slot, ys, x, gate, g, b)


def _moe_ln(xp, xs_, router_w, w1, w3, w2, g, b, alpha):
    n_experts = router_w.shape[1]
    d = xp.shape[1]
    tp, ts = xp.shape[0], xs_.shape[0]
    x_all = jnp.concatenate([xp, xs_], axis=0)
    t_all = tp + ts
    rw = jnp.pad(router_w, ((0, 0), (0, LANES - n_experts)))
    idx, gate = _router(x_all, rw, n_experts, ts, "moe_router")

    tm = 1024
    n_assign = t_all * TOP_K
    e_flat = idx[:, :TOP_K].reshape(n_assign)
    onehot = (e_flat[:, None] == jnp.arange(n_experts, dtype=I32)[None, :]).astype(I32)
    csum = jnp.cumsum(onehot, axis=0)
    pos = jnp.take_along_axis(csum, e_flat[:, None], axis=1)[:, 0] - 1
    counts = csum[-1]
    padded = ((counts + tm - 1) // tm) * tm
    ends = jnp.cumsum(padded)
    offs = ends - padded
    slot = (offs[e_flat] + pos).astype(I32)
    n_tiles = (n_assign + n_experts * (tm - 1)) // tm
    src = jnp.zeros((n_tiles * tm,), I32).at[slot].set(jnp.arange(n_assign, dtype=I32) // TOP_K)
    n_used = (ends[-1] // tm).astype(I32)
    tiles = jnp.arange(n_tiles, dtype=I32)
    t_exp = jnp.minimum(jnp.sum((tiles[:, None] * tm >= ends[None, :]).astype(I32), axis=1), n_experts - 1)
    last = jnp.maximum(n_used - 1, 0)
    tile_expert = jnp.where(tiles < n_used, t_exp, t_exp[last])
    tile_block = jnp.where(tiles < n_used, tiles, last)

    x_sorted = _gather_rows(src, x_all, 256, "moe_gather")
    y_sorted = _experts(tile_expert, tile_block, n_used.reshape(1), x_sorted, w1, w3, w2, tm, 512, "moe_experts")
    yp = _combine_ln(slot, y_sorted, xp, gate, g, b, 0, alpha, 256, "moe_combine_prompt")
    ys = _combine_ln(slot, y_sorted, xs_, gate, g, b, tp, alpha, 256, "moe_combine_sample")
    return yp, ys


def _rope_tables(pos):
    half = ROT_DIM // 2
    inv = ROPE_THETA ** (-jnp.arange(0, ROT_DIM, 2, dtype=F32) / ROT_DIM)
    ang = pos.astype(F32)[:, None] * inv[None, :]
    cos, sin = jnp.cos(ang), jnp.sin(ang)
    n = pos.shape[0]
    ones = jnp.ones((n, HEAD_DIM - ROT_DIM), F32)
    zeros = jnp.zeros((n, HEAD_DIM - ROT_DIM), F32)
    zh = jnp.zeros((n, half), F32)
    reps = LANES // HEAD_DIM
    c = jnp.tile(jnp.concatenate([cos, cos, ones], axis=1), (1, reps))
    sa = jnp.tile(jnp.concatenate([-sin, zh, zeros], axis=1), (1, reps))
    sb = jnp.tile(jnp.concatenate([zh, sin, zeros], axis=1), (1, reps))
    return c, sa, sb


def _block_diag_q(q, scale):
    n_seq, n_new, nh, d = q.shape
    eye = jnp.eye(nh, dtype=F32)
    out = jnp.einsum("sihd,hg->shigd", q * scale, eye)
    return out.reshape(n_seq, nh * n_new, nh * d).astype(BF16)


def _diag_blocks(o, n_new, nh, d):
    n_seq = o.shape[0]
    o5 = o.reshape(n_seq, nh, n_new, nh, d)
    sel = jnp.stack([o5[:, h, :, h, :] for h in range(nh)], axis=2)
    return sel.reshape(n_seq * n_new, nh * d)


def kernel(x_prompt, x_sample, cache_sb_k, cache_sb_v, page_table, state_ssm, state_conv, cache_c_w128_k, cache_c_w128_v, cache_c_w512_k, cache_c_w512_v, cache_c_w2048_k, cache_c_w2048_v, w_in_a, conv_w, conv_b, dt_bias, a_log, d_skip, ssm_norm_w, sb_bias, w_out_a, ln_a_mix_g, ln_a_mix_b, ffn_w1, ffn_w3, ffn_w2, ln_a_ffn_g, ln_a_ffn_b, w_in_c, w_out_c, ln_c_mix_g, ln_c_mix_b, router_w, moe_w1, moe_w3, moe_w2, ln_c_ffn_g, ln_c_ffn_b):
    n_pr, seq, d_model = x_prompt.shape
    n_dec, dec_seq, _ = x_sample.shape
    n_pages = page_table.shape[1]
    page = cache_sb_k.shape[2]
    past_len = n_pages * page
    ssm_heads, p, n_state = state_ssm.shape[2], state_ssm.shape[3], state_ssm.shape[4]
    d_inner = ssm_heads * p
    conv_dim = state_conv.shape[3]
    sb_heads = cache_sb_k.shape[3]
    sb_w = sb_heads * HEAD_DIM
    c_slots = cache_c_w128_k.shape[3]
    c_wd = c_slots * HEAD_DIM
    n_groups = len(C_GROUPS)
    depth = w_in_a.shape[0] + w_in_c.shape[0]
    alpha = (2 * depth) ** 0.25
    tp = n_pr * seq
    ts = n_dec * dec_seq
    win_k = (cache_c_w128_k, cache_c_w512_k, cache_c_w2048_k)
    win_v = (cache_c_w128_v, cache_c_w512_v, cache_c_w2048_v)
    row = lambda v: v.reshape(1, -1)

    yp = x_prompt.reshape(tp, d_model)
    ys = x_sample.reshape(ts, d_model)
    outs = {}

    i = 0
    w = w_in_a[i]
    c0, c1, c2 = d_inner, d_inner + conv_dim, d_inner + conv_dim + ssm_heads
    w_re = jnp.concatenate([w[:, c0:c1], w[:, :c0], w[:, c2:], w[:, c1:c2],
                            jnp.zeros((d_model, LANES - ssm_heads), F32)], axis=1).astype(BF16)
    o_q = conv_dim + d_inner
    proj_p = _matmul(yp, w_re, 512, "in_proj_a_prompt")
    proj_s = _matmul(ys, w_re, ts, "in_proj_a_sample")

    pad_l = LANES - ssm_heads
    ssd_par = (conv_w[i], row(conv_b[i]), row(jnp.pad(dt_bias[i], (0, pad_l))),
               row(jnp.pad(-jnp.exp(a_log[i]), (0, pad_l))), row(jnp.repeat(d_skip[i], p)), row(ssm_norm_w[i]))
    ya_p, ssm_p = _ssd(proj_p.reshape(n_pr, seq, -1), jnp.zeros((n_pr, CONV_W - 1, conv_dim), F32),
                       jnp.zeros((n_pr, ssm_heads, p, n_state), F32), *ssd_par, SSD_CHUNK, "ssd_prompt")
    proj_s3 = proj_s.reshape(n_dec, dec_seq, -1)
    proj_s_pad = jnp.pad(proj_s3, ((0, 0), (0, SSD_CHUNK - dec_seq), (0, 0)))
    ya_s, ssm_s = _ssd(proj_s_pad, state_conv[i], state_ssm[i], *ssd_par, dec_seq, "ssd_sample")
    ya_p = ya_p.reshape(tp, d_inner)
    ya_s = ya_s[:, :dec_seq].reshape(ts, d_inner)
    conv_p = proj_p.reshape(n_pr, seq, -1)[:, seq - (CONV_W - 1):, :conv_dim]
    conv_s = jnp.concatenate([state_conv[i], proj_s3[:, :, :conv_dim]], axis=1)[:, dec_seq:]

    k_p = proj_p[:, o_q + sb_w:o_q + 2 * sb_w].reshape(n_pr, seq, sb_heads, HEAD_DIM)
    v_p = proj_p[:, o_q + 2 * sb_w:o_q + 3 * sb_w].reshape(n_pr, seq, sb_heads, HEAD_DIM)
    o_p = _sb_prompt(sb_bias[i], proj_p.reshape(n_pr, seq, -1), o_q, o_q + sb_w, o_q + 2 * sb_w,
                     sb_heads, HEAD_DIM, 256, "sb_prompt").reshape(tp, sb_w)

    q_s = proj_s[:, o_q:o_q + sb_w].reshape(n_dec, dec_seq, sb_heads, HEAD_DIM)
    k_s = proj_s[:, o_q + sb_w:o_q + 2 * sb_w].reshape(n_dec, dec_seq, sb_w)
    v_s = proj_s[:, o_q + 2 * sb_w:o_q + 3 * sb_w].reshape(n_dec, dec_seq, sb_w)
    qbd = _block_diag_q(q_s, HEAD_DIM ** -0.5)
    bias_rows = jnp.repeat(sb_bias[i], dec_seq).reshape(sb_heads * dec_seq, 1)
    pad_new = ((0, 0), (0, page - dec_seq), (0, 0))
    o_s = _sb_sample(page_table, qbd, bias_rows, jnp.pad(k_s, pad_new), jnp.pad(v_s, pad_new),
                     cache_sb_k[i].astype(BF16).reshape(-1, page, sb_w),
                     cache_sb_v[i].astype(BF16).reshape(-1, page, sb_w),
                     16, dec_seq, "sb_sample")
    o_s = _diag_blocks(o_s, dec_seq, sb_heads, HEAD_DIM)

    wo = w_out_a[i].astype(BF16)
    ln_g, ln_b = row(ln_a_mix_g[i]), row(ln_a_mix_b[i])
    wo2 = [wo[:d_inner], wo[d_inner:]]
    yp = _proj_ln([ya_p, o_p], wo2, yp, ln_g, ln_b, alpha, 512, "out_proj_a_prompt")
    ys = _proj_ln([ya_s, o_s], wo2, ys, ln_g, ln_b, alpha, ts, "out_proj_a_sample")
    f1, f3, f2 = ffn_w1[i].astype(BF16), ffn_w3[i].astype(BF16), ffn_w2[i].astype(BF16)
    ln_g, ln_b = row(ln_a_ffn_g[i]), row(ln_a_ffn_b[i])
    d_ff = f1.shape[1]
    tf = 2 * LANES if d_ff % (2 * LANES) == 0 else LANES
    yp = _ffn_ln(yp, f1, f3, f2, ln_g, ln_b, alpha, 1024, tf, "ffn_prompt")
    ys = _ffn_ln(ys, f1, f3, f2, ln_g, ln_b, alpha, ts, tf, "ffn_sample")

    outs["sb_kp"] = k_p[None]
    outs["sb_vp"] = v_p[None]
    outs["sb_ks"] = k_s.reshape(1, n_dec, dec_seq, sb_heads, HEAD_DIM)
    outs["sb_vs"] = v_s.reshape(1, n_dec, dec_seq, sb_heads, HEAD_DIM)
    outs["ssm_p"], outs["ssm_s"], outs["conv_p"], outs["conv_s"] = ssm_p[None], ssm_s[None], conv_p[None], conv_s[None]

    wc = w_in_c[i].astype(BF16)
    gw = n_groups * c_wd
    cos_p, sa_p, sb_p = _rope_tables(jnp.arange(seq))
    cos_s, sa_s, sb_s = _rope_tables(past_len + jnp.arange(ts) % dec_seq)
    qkv_p = _matmul_rope(yp, wc, cos_p, sa_p, sb_p, 512, "in_proj_c_prompt")
    qkv_s = _matmul_rope(ys, wc, cos_s, sa_s, sb_s, ts, "in_proj_c_sample")

    scale = HEAD_DIM ** -0.5
    o_c_p = _dil_prompt(qkv_p.reshape(n_pr, seq, -1), [dil for _, dil in C_GROUPS], c_slots, HEAD_DIM,
                        "dilated_prompt").reshape(tp, c_wd)
    o_groups_s, l_groups_s = [], []
    for g, (win, dil) in enumerate(C_GROUPS):
        cq, ck, cv = g * c_wd, gw + g * c_wd, 2 * gw + g * c_wd
        kg_p = qkv_p[:, ck:ck + c_wd]
        vg_p = qkv_p[:, cv:cv + c_wd]
        keep = min(win, seq)
        outs[f"c_kp{g}"] = kg_p.reshape(n_pr, seq, c_slots, HEAD_DIM)[:, seq - keep:][None]
        outs[f"c_vp{g}"] = vg_p.reshape(n_pr, seq, c_slots, HEAD_DIM)[:, seq - keep:][None]

        qg_s = qkv_s[:, cq:cq + c_wd].reshape(n_dec, dec_seq, c_slots, HEAD_DIM)
        pad_rows = ((0, 0), (0, DIL_BLOCK - dec_seq), (0, 0))
        kn = jnp.pad(qkv_s[:, ck:ck + c_wd].reshape(n_dec, dec_seq, c_wd), pad_rows)
        vn = jnp.pad(qkv_s[:, cv:cv + c_wd].reshape(n_dec, dec_seq, c_wd), pad_rows)
        wlen = win_k[g].shape[2]
        o_g, l_g, k_new, v_new = _dil_sample(_block_diag_q(qg_s, scale), win_k[g][i].reshape(n_dec, wlen, c_wd),
                                             win_v[g][i].reshape(n_dec, wlen, c_wd), kn, vn, dil, dec_seq,
                                             f"dilated_sample_{g}")
        o_groups_s.append(_diag_blocks(o_g, dec_seq, c_slots, HEAD_DIM))
        l_s = l_g[:, :, 0].reshape(n_dec, c_slots, dec_seq).transpose(0, 2, 1)
        l_groups_s.append(jnp.repeat(l_s.reshape(ts, c_slots), HEAD_DIM, axis=1))
        outs[f"c_ks{g}"] = k_new.reshape(1, n_dec, wlen, c_slots, HEAD_DIM)
        outs[f"c_vs{g}"] = v_new.reshape(1, n_dec, wlen, c_slots, HEAD_DIM)

    woc = w_out_c[i].astype(BF16)
    ln_g, ln_b = row(ln_c_mix_g[i]), row(ln_c_mix_b[i])
    yp = _proj_ln([o_c_p], [woc], yp, ln_g, ln_b, alpha, 512, "out_proj_c_prompt")
    ys = _merge_proj_ln(o_groups_s, l_groups_s, woc, ys, ln_g, ln_b, alpha, ts, "out_proj_c_sample")
    yp, ys = _moe_ln(yp, ys, router_w[i], moe_w1[i], moe_w3[i], moe_w2[i],
                     row(ln_c_ffn_g[i]), row(ln_c_ffn_b[i]), alpha)

    res = [yp.reshape(n_pr, seq, d_model), ys.reshape(n_dec, dec_seq, d_model),
           outs["sb_kp"], outs["sb_vp"], outs["sb_ks"], outs["sb_vs"],
           outs["ssm_p"], outs["ssm_s"], outs["conv_p"], outs["conv_s"]]
    for g in range(n_groups):
        res += [outs[f"c_kp{g}"], outs[f"c_vp{g}"], outs[f"c_ks{g}"], outs[f"c_vs{g}"]]
    return tuple(res)
```

```python
import functools
import math

import numpy as np
import jax
import jax.numpy as jnp
from jax import lax
from jax.experimental import pallas as pl
from jax.experimental.pallas import tpu as pltpu

F32 = jnp.float32
BF16 = jnp.bfloat16
I32 = jnp.int32

HEAD_DIM = 64
SSM_HEAD_DIM = 64
SSM_GROUPS = 2
SSM_STATE = 128
CONV_W = 4
SSD_CHUNK = 128
C_GROUPS = ((128, 1), (512, 4), (2048, 16))
ROT_DIM = HEAD_DIM // 4
ROPE_THETA = 500000.0
TOP_K = 2
LN_EPS = 1e-5
RMS_EPS = 1e-6

LANES = 128
SUBLANES = 8
VMEM_LIMIT_BYTES = 56 * 1024 * 1024


def _params(*semantics):
    return pltpu.CompilerParams(dimension_semantics=semantics, vmem_limit_bytes=VMEM_LIMIT_BYTES)


def _dot(a, b):
    return jnp.dot(a, b, preferred_element_type=F32)


def _dot_nt(a, b):
    return lax.dot_general(a, b, (((1,), (1,)), ((), ())), preferred_element_type=F32)


def _layer_norm(y, g, b):
    mu = jnp.mean(y, axis=-1, keepdims=True)
    d = y - mu
    var = jnp.mean(d * d, axis=-1, keepdims=True)
    return d * lax.rsqrt(var + LN_EPS) * g + b


def _silu(a):
    return a * jax.nn.sigmoid(a)


def _split3(a):
    hi = a.astype(BF16)
    r = a - hi.astype(F32)
    mid = r.astype(BF16)
    lo = (r - mid.astype(F32)).astype(BF16)
    return hi, mid, lo


def _mm_body(x_ref, w_ref, o_ref):
    o_ref[...] = _dot(x_ref[...].astype(BF16), w_ref[...])


def _matmul(x, w, tm, name):
    m, k = x.shape
    n = w.shape[1]
    return pl.pallas_call(
        _mm_body,
        grid=(m // tm,),
        in_specs=[pl.BlockSpec((tm, k), lambda i: (i, 0)), pl.BlockSpec((k, n), lambda i: (0, 0))],
        out_specs=pl.BlockSpec((tm, n), lambda i: (i, 0)),
        out_shape=jax.ShapeDtypeStruct((m, n), F32),
        compiler_params=_params("parallel"),
        name=name,
    )(x, w)


def _mm_rope_body(x_ref, w_ref, cos_ref, sa_ref, sb_ref, o_ref, *, n_rot_tiles):
    y = _dot(x_ref[...].astype(BF16), w_ref[...])
    j = pl.program_id(1)

    @pl.when(j < n_rot_tiles)
    def _():
        cos = cos_ref[...]
        sa = sa_ref[...]
        sb = sb_ref[...]
        for c in range(y.shape[1] // LANES):
            yc = y[:, c * LANES:(c + 1) * LANES]
            o_ref[:, c * LANES:(c + 1) * LANES] = (
                yc * cos + pltpu.roll(yc, LANES - ROT_DIM // 2, 1) * sa + pltpu.roll(yc, ROT_DIM // 2, 1) * sb)

    @pl.when(j >= n_rot_tiles)
    def _():
        o_ref[...] = y


def _matmul_rope(x, w, cos, sa, sb, tm, name):
    m, k = x.shape
    n = w.shape[1]
    tn = n // 3
    nb = cos.shape[0] // tm
    return pl.pallas_call(
        functools.partial(_mm_rope_body, n_rot_tiles=2),
        grid=(m // tm, 3),
        in_specs=[pl.BlockSpec((tm, k), lambda i, j: (i, 0)),
                  pl.BlockSpec((k, tn), lambda i, j: (0, j)),
                  pl.BlockSpec((tm, LANES), lambda i, j: (i % nb, 0)),
                  pl.BlockSpec((tm, LANES), lambda i, j: (i % nb, 0)),
                  pl.BlockSpec((tm, LANES), lambda i, j: (i % nb, 0))],
        out_specs=pl.BlockSpec((tm, tn), lambda i, j: (i, j)),
        out_shape=jax.ShapeDtypeStruct((m, n), F32),
        compiler_params=_params("parallel", "arbitrary"),
        name=name,
    )(x, w, cos, sa, sb)


def _proj_ln_body(*refs, alpha, n_in):
    h_refs, w_refs = refs[:n_in], refs[n_in:2 * n_in]
    r_ref, g_ref, b_ref, o_ref = refs[2 * n_in:]
    m = _dot(h_refs[0][...].astype(BF16), w_refs[0][...])
    for h_ref, w_ref in zip(h_refs[1:], w_refs[1:]):
        m = m + _dot(h_ref[...].astype(BF16), w_ref[...])
    o_ref[...] = _layer_norm(alpha * r_ref[...] + m, g_ref[...], b_ref[...])


def _proj_ln(hs, ws, resid, g, b, alpha, tm, name):
    m, d = resid.shape
    row = lambda i: (i, 0)
    fix = lambda i: (0, 0)
    return pl.pallas_call(
        functools.partial(_proj_ln_body, alpha=alpha, n_in=len(hs)),
        grid=(m // tm,),
        in_specs=[pl.BlockSpec((tm, h.shape[1]), row) for h in hs]
                 + [pl.BlockSpec((w.shape[0], d), fix) for w in ws]
                 + [pl.BlockSpec((tm, d), row), pl.BlockSpec((1, d), fix), pl.BlockSpec((1, d), fix)],
        out_specs=pl.BlockSpec((tm, d), row),
        out_shape=jax.ShapeDtypeStruct((m, d), F32),
        compiler_params=_params("parallel"),
        name=name,
    )(*hs, *ws, resid, g, b)


def _merge_proj_ln_body(o0_ref, o1_ref, o2_ref, l0_ref, l1_ref, l2_ref, w_ref, r_ref, g_ref, b_ref, o_ref, *, alpha):
    l0, l1, l2 = l0_ref[...], l1_ref[...], l2_ref[...]
    mx = jnp.maximum(jnp.maximum(l0, l1), l2)
    e0, e1, e2 = jnp.exp(l0 - mx), jnp.exp(l1 - mx), jnp.exp(l2 - mx)
    o = (e0 * o0_ref[...] + e1 * o1_ref[...] + e2 * o2_ref[...]) / (e0 + e1 + e2)
    m = _dot(o.astype(BF16), w_ref[...])
    o_ref[...] = _layer_norm(alpha * r_ref[...] + m, g_ref[...], b_ref[...])


def _merge_proj_ln(os_, ls_, w, resid, g, b, alpha, tm, name):
    m, d = resid.shape
    k = w.shape[0]
    row = lambda i: (i, 0)
    fix = lambda i: (0, 0)
    return pl.pallas_call(
        functools.partial(_merge_proj_ln_body, alpha=alpha),
        grid=(m // tm,),
        in_specs=[pl.BlockSpec((tm, k), row)] * 6 + [pl.BlockSpec((k, d), fix), pl.BlockSpec((tm, d), row),
                                                     pl.BlockSpec((1, d), fix), pl.BlockSpec((1, d), fix)],
        out_specs=pl.BlockSpec((tm, d), row),
        out_shape=jax.ShapeDtypeStruct((m, d), F32),
        compiler_params=_params("parallel"),
        name=name,
    )(*os_, *ls_, w, resid, g, b)


def _ffn_ln_body(x_ref, w1_ref, w3_ref, w2_ref, g_ref, b_ref, o_ref, xb_ref, acc_ref, *, alpha):
    f = pl.program_id(1)

    @pl.when(f == 0)
    def _():
        xb_ref[...] = x_ref[...].astype(BF16)
        acc_ref[...] = jnp.zeros_like(acc_ref)

    xb = xb_ref[...]
    hid = _silu(_dot(xb, w1_ref[...])) * _dot(xb, w3_ref[...])
    acc_ref[...] += _dot(hid.astype(BF16), w2_ref[...])

    @pl.when(f == pl.num_programs(1) - 1)
    def _():
        o_ref[...] = _layer_norm(alpha * x_ref[...] + acc_ref[...], g_ref[...], b_ref[...])


def _ffn_ln(x, w1, w3, w2, g, b, alpha, tm, tf, name):
    m, d = x.shape
    ff = w1.shape[1]
    return pl.pallas_call(
        functools.partial(_ffn_ln_body, alpha=alpha),
        grid=(m // tm, ff // tf),
        in_specs=[pl.BlockSpec((tm, d), lambda i, f: (i, 0)),
                  pl.BlockSpec((d, tf), lambda i, f: (0, f)),
                  pl.BlockSpec((d, tf), lambda i, f: (0, f)),
                  pl.BlockSpec((tf, d), lambda i, f: (f, 0)),
                  pl.BlockSpec((1, d), lambda i, f: (0, 0)),
                  pl.BlockSpec((1, d), lambda i, f: (0, 0))],
        out_specs=pl.BlockSpec((tm, d), lambda i, f: (i, 0)),
        out_shape=jax.ShapeDtypeStruct((m, d), F32),
        scratch_shapes=[pltpu.VMEM((tm, d), BF16), pltpu.VMEM((tm, d), F32)],
        compiler_params=_params("parallel", "arbitrary"),
        name=name,
    )(x, w1, w3, w2, g, b)


def _ssd_body(xbc_ref, z_ref, dt_ref, conv0_ref, h0_ref, cw_ref, cb_ref, dtb_ref, aneg_ref, dsk_ref, nw_ref,
              y_ref, hn_ref, xpad_ref, *, q, valid_len, n_heads, d_inner):
    c = pl.program_id(1)
    conv_dim = xpad_ref.shape[1]
    gn = SSM_GROUPS * SSM_STATE
    heads_per_group = n_heads // SSM_GROUPS
    p = SSM_HEAD_DIM

    @pl.when(c == 0)
    def _():
        xpad_ref[0:SUBLANES, :] = jnp.zeros((SUBLANES, conv_dim), F32)
        xpad_ref[SUBLANES - (CONV_W - 1):SUBLANES, :] = conv0_ref[...]
        hn_ref[...] = h0_ref[...]

    x_raw = xbc_ref[...]
    xpad_ref[SUBLANES:SUBLANES + q, :] = x_raw
    cw = cw_ref[...]
    acc = cb_ref[...] + x_raw * cw[CONV_W - 1:CONV_W]
    for j in range(CONV_W - 1):
        off = SUBLANES - (CONV_W - 1) + j
        acc = acc + xpad_ref[off:off + q, :] * cw[j:j + 1]
    xpad_ref[SUBLANES - (CONV_W - 1):SUBLANES, :] = x_raw[q - (CONV_W - 1):q, :]
    xc = _silu(acc)
    xs = xc[:, :d_inner]
    bm = xc[:, d_inner:d_inner + gn].astype(BF16)
    cm = xc[:, d_inner + gn:d_inner + 2 * gn].astype(BF16)

    t = dt_ref[...] + dtb_ref[...]
    dtv = jnp.maximum(t, 0.0) + jnp.log(1.0 + jnp.exp(-jnp.abs(t)))
    rows = lax.broadcasted_iota(I32, (q, q), 0)
    cols = lax.broadcasted_iota(I32, (q, q), 1)
    if valid_len < q:
        dtv = jnp.where(lax.broadcasted_iota(I32, dtv.shape, 0) < valid_len, dtv, 0.0)
    a = dtv * aneg_ref[...]
    causal = cols <= rows
    tri = jnp.where(causal, 1.0, 0.0).astype(BF16)
    a_hi, a_mid, a_lo = _split3(a)
    a_cum = _dot(tri, a_hi) + _dot(tri, a_mid) + _dot(tri, a_lo)
    a_cum_t = a_cum.T
    a_tot = a_cum[q - 1:q, :]

    ys = []
    xws = []
    for g in range(SSM_GROUPS):
        cg = cm[:, g * SSM_STATE:(g + 1) * SSM_STATE]
        bg = bm[:, g * SSM_STATE:(g + 1) * SSM_STATE]
        scores = _dot_nt(cg, bg)
        for r in range(heads_per_group):
            h = g * heads_per_group + r
            col = a_cum[:, h:h + 1]
            row = a_cum_t[h:h + 1, :]
            decay_ls = jnp.where(causal, jnp.exp(col - row), 0.0)
            w_diag = (scores * decay_ls).astype(BF16)
            xs_h = xs[:, h * p:(h + 1) * p]
            xdt = xs_h * dtv[:, h:h + 1]
            y_diag = _dot(w_diag, xdt.astype(BF16))
            y_off = _dot_nt(cg, hn_ref[h].astype(BF16)) * jnp.exp(col)
            ys.append(y_diag + y_off)
            xws.append(xdt * jnp.exp(a_tot[:, h:h + 1] - col))
    y = jnp.concatenate(ys, axis=1) + dsk_ref[...] * xs
    xw = jnp.concatenate(xws, axis=1)

    heads_per_tile = LANES // p
    for j in range(d_inner // LANES):
        xw_t = xw[:, j * LANES:(j + 1) * LANES].T.astype(BF16)
        for hh in range(heads_per_tile):
            h = j * heads_per_tile + hh
            g = h // heads_per_group
            bg = bm[:, g * SSM_STATE:(g + 1) * SSM_STATE]
            st = _dot(xw_t[hh * p:(hh + 1) * p, :], bg)
            hn_ref[h] = jnp.exp(a_tot[:, h:h + 1]) * hn_ref[h] + st

    zz = z_ref[...]
    yg = y * _silu(zz)
    gsz = d_inner // SSM_GROUPS
    outs = []
    for g in range(SSM_GROUPS):
        blk = yg[:, g * gsz:(g + 1) * gsz]
        ms = jnp.mean(blk * blk, axis=-1, keepdims=True)
        outs.append(blk * lax.rsqrt(ms + RMS_EPS))
    y_ref[...] = jnp.concatenate(outs, axis=1) * nw_ref[...]


def _ssd(proj3, conv0, h0, cw, cb, dtb, aneg, dsk, nw, valid_len, name):
    b, length, _ = proj3.shape
    n_heads, p, n = h0.shape[1], h0.shape[2], h0.shape[3]
    d_inner = n_heads * p
    conv_dim = cw.shape[1]
    q = SSD_CHUNK
    z_blk = conv_dim // d_inner
    dt_blk = (proj3.shape[2] - LANES) // LANES
    fix2 = lambda i, c: (0, 0)
    return pl.pallas_call(
        functools.partial(_ssd_body, q=q, valid_len=valid_len, n_heads=n_heads, d_inner=d_inner),
        grid=(b, length // q),
        in_specs=[pl.BlockSpec((None, q, conv_dim), lambda i, c: (i, c, 0)),
                  pl.BlockSpec((None, q, d_inner), lambda i, c: (i, c, z_blk)),
                  pl.BlockSpec((None, q, LANES), lambda i, c: (i, c, dt_blk)),
                  pl.BlockSpec((None, CONV_W - 1, conv_dim), lambda i, c: (i, 0, 0)),
                  pl.BlockSpec((None, n_heads, p, n), lambda i, c: (i, 0, 0, 0)),
                  pl.BlockSpec((CONV_W, conv_dim), fix2),
                  pl.BlockSpec((1, conv_dim), fix2),
                  pl.BlockSpec((1, LANES), fix2),
                  pl.BlockSpec((1, LANES), fix2),
                  pl.BlockSpec((1, d_inner), fix2),
                  pl.BlockSpec((1, d_inner), fix2)],
        out_specs=[pl.BlockSpec((None, q, d_inner), lambda i, c: (i, c, 0)),
                   pl.BlockSpec((None, n_heads, p, n), lambda i, c: (i, 0, 0, 0))],
        out_shape=[jax.ShapeDtypeStruct((b, length, d_inner), F32),
                   jax.ShapeDtypeStruct((b, n_heads, p, n), F32)],
        scratch_shapes=[pltpu.VMEM((SUBLANES + q, conv_dim), F32)],
        compiler_params=_params("parallel", "arbitrary"),
        name=name,
    )(proj3, proj3, proj3, conv0, h0, cw, cb, dtb, aneg, dsk, nw)


SB_SEGMENTS = SUBLANES


def _sb_weights(z, carry, valid):
    tk, tq = z.shape
    seglen = tk // SB_SEGMENTS
    fail = 1.0 / (1.0 + jnp.exp(z))
    if valid is not None:
        fail = jnp.where(valid, fail, 1.0)
    fail3 = fail.reshape(seglen, SB_SEGMENTS, tq)
    prods = [None] * (seglen + 1)
    prods[seglen] = jnp.ones((SB_SEGMENTS, tq), F32)
    for v in reversed(range(seglen)):
        prods[v] = prods[v + 1] * fail3[v]
    offs = [None] * SB_SEGMENTS
    tot = carry
    for s in reversed(range(SB_SEGMENTS)):
        offs[s] = tot
        tot = tot * prods[0][s:s + 1, :]
    offs = jnp.concatenate(offs, axis=0)
    w3 = jnp.stack([prods[v + 1] - prods[v] for v in range(seglen)], axis=0) * offs[None]
    return w3.reshape(tk, tq), tot


def _sb_prompt_body(bias_ref, q_ref, k_ref, v_ref, o_ref, kp_ref, vt_ref, *, tq, hb, d):
    i = pl.program_id(2)
    h0 = pl.program_id(1) * hb
    seglen = tq // SB_SEGMENTS
    nkb = kp_ref.shape[0]
    r = lax.broadcasted_iota(I32, (tq, tq), 0)
    key_pos = (r % SB_SEGMENTS) * seglen + r // SB_SEGMENTS
    strictly_before = key_pos < lax.broadcasted_iota(I32, (tq, tq), 1)

    @pl.when(i == 0)
    def _():
        def permute(kb, c):
            base = pl.multiple_of(kb * tq, tq)
            kp = [k_ref[pl.ds(base + v, SB_SEGMENTS, stride=seglen), :] for v in range(seglen)]
            vp = [v_ref[pl.ds(base + v, SB_SEGMENTS, stride=seglen), :] for v in range(seglen)]
            kp_ref[kb] = jnp.concatenate(kp, axis=0).astype(BF16)
            vt_ref[kb] = jnp.concatenate(vp, axis=0).T.astype(BF16)
            return c

        lax.fori_loop(0, nkb, permute, 0)

    qt_all = (q_ref[...] * (d ** -0.5)).T
    head_of_row = lax.broadcasted_iota(I32, qt_all.shape, 0) // d
    qts = [jnp.where(head_of_row == hh, qt_all, 0.0).astype(BF16) for hh in range(hb)]

    def logits(hh, kb):
        return _dot(kp_ref[kb], qts[hh]) + bias_ref[h0 + hh]

    def values(hh, kb):
        return vt_ref[kb, hh * d:(hh + 1) * d, :]

    state = []
    for hh in range(hb):
        w, carry = _sb_weights(logits(hh, i), jnp.ones((1, tq), F32), strictly_before)
        state.append((logits(hh, jnp.maximum(i - 1, 0)), w.astype(BF16), carry, jnp.zeros((d, tq), F32)))

    def step(it, st):
        kb = i - 1 - it
        out = []
        for hh in range(hb):
            z, w_prev, carry, acc = st[hh]
            acc = acc + _dot(values(hh, kb + 1), w_prev)
            z_next = logits(hh, jnp.maximum(kb - 1, 0))
            w, carry = _sb_weights(z, carry, None)
            out.append((z_next, w.astype(BF16), carry, acc))
        return tuple(out)

    state = lax.fori_loop(0, i, step, tuple(state))
    out_t = [state[hh][3] + _dot(values(hh, 0), state[hh][1]) for hh in range(hb)]
    o_ref[...] = jnp.concatenate(out_t, axis=0).T


def _sb_prompt(bias, proj3, q_col, k_col, v_col, n_heads, d, tq, name):
    b, length, _ = proj3.shape
    hb = LANES // d
    nkb = length // tq
    qc, kc, vc = q_col // LANES, k_col // LANES, v_col // LANES
    return pl.pallas_call(
        functools.partial(_sb_prompt_body, tq=tq, hb=hb, d=d),
        grid=(b, n_heads // hb, nkb),
        in_specs=[pl.BlockSpec(memory_space=pltpu.SMEM),
                  pl.BlockSpec((None, tq, LANES), lambda bi, hi, i: (bi, i, qc + hi)),
                  pl.BlockSpec((None, length, LANES), lambda bi, hi, i: (bi, 0, kc + hi)),
                  pl.BlockSpec((None, length, LANES), lambda bi, hi, i: (bi, 0, vc + hi))],
        out_specs=pl.BlockSpec((None, tq, LANES), lambda bi, hi, i: (bi, i, hi)),
        out_shape=jax.ShapeDtypeStruct((b, length, n_heads * d), F32),
        scratch_shapes=[pltpu.VMEM((nkb, tq, LANES), BF16), pltpu.VMEM((nkb, LANES, tq), BF16)],
        compiler_params=_params("parallel", "parallel", "arbitrary"),
        name=name,
    )(bias, proj3, proj3, proj3)


SB_CHUNK = 2 * LANES


def _sb_sample_body(pt_ref, q_ref, bias_ref, knew_ref, vnew_ref, *refs, pages_per_step, n_new):
    k_refs = refs[:pages_per_step]
    v_refs = refs[pages_per_step:2 * pages_per_step]
    o_ref = refs[2 * pages_per_step]
    carry_ref = refs[2 * pages_per_step + 1]
    j = pl.program_id(1)
    q = q_ref[...]
    bias = bias_ref[...]
    nr, d = q.shape
    page, n_heads = knew_ref.shape[0], knew_ref.shape[1]
    ncol = page * n_heads
    nch = ncol // SB_CHUNK
    col = lax.broadcasted_iota(I32, (nr, ncol), 1)
    row = lax.broadcasted_iota(I32, (nr, ncol), 0)
    same_head = col % n_heads == row // n_new
    kk = lax.broadcasted_iota(I32, (SB_CHUNK, SB_CHUNK), 0)
    ss = lax.broadcasted_iota(I32, (SB_CHUNK, SB_CHUNK), 1)
    later = jnp.where(kk > ss, 1.0, 0.0).astype(BF16)

    def block(k3, v3, valid):
        k2 = k3.reshape(ncol, d).astype(BF16)
        v2 = v3.reshape(ncol, d).astype(BF16)
        z = _dot_nt(q, k2) + bias
        sp = jnp.maximum(z, 0.0) + jnp.log(1.0 + jnp.exp(-jnp.abs(z)))
        log_fail = jnp.where(valid, -sp, 0.0)
        log_beta = z - sp
        lf_hi = log_fail.astype(BF16)
        lf_lo = (log_fail - lf_hi.astype(F32)).astype(BF16)
        chunks = [a[:, c * SB_CHUNK:(c + 1) * SB_CHUNK] for a in (lf_hi, lf_lo) for c in range(nch)]
        inner = _dot(jnp.concatenate(chunks, axis=0), later)
        carry = carry_ref[...]
        tails = [None] * nch
        for c in reversed(range(nch)):
            tails[c] = inner[c * nr:(c + 1) * nr] + inner[(nch + c) * nr:(nch + c + 1) * nr] + carry
            carry = carry + jnp.sum(log_fail[:, c * SB_CHUNK:(c + 1) * SB_CHUNK], axis=-1, keepdims=True)
        carry_ref[...] = carry
        w = jnp.where(valid, jnp.exp(log_beta + jnp.concatenate(tails, axis=1)), 0.0)
        o_ref[...] += _dot(w.astype(BF16), v2)

    @pl.when(j == 0)
    def _():
        carry_ref[...] = jnp.zeros_like(carry_ref)
        o_ref[...] = jnp.zeros_like(o_ref)
        block(knew_ref[...], vnew_ref[...], jnp.logical_and(same_head, col // n_heads < row % n_new))

    for p in range(pages_per_step):
        block(k_refs[p][...], v_refs[p][...], same_head)


def _sb_sample(page_table, q_rows, bias_rows, knew, vnew, cache_k, cache_v, pages_per_step, n_new, name):
    n_seq, n_pages = page_table.shape
    nr, d = q_rows.shape[1], q_rows.shape[2]
    page, n_heads = cache_k.shape[1], cache_k.shape[2]
    steps = n_pages // pages_per_step

    def page_spec(p):
        return pl.BlockSpec((None, page, n_heads, d),
                            lambda s, j, pt: (pt[s, n_pages - 1 - (j * pages_per_step + p)], 0, 0, 0))

    seq3 = lambda s, j, pt: (s, 0, 0)
    seq4 = lambda s, j, pt: (s, 0, 0, 0)
    grid_spec = pltpu.PrefetchScalarGridSpec(
        num_scalar_prefetch=1,
        grid=(n_seq, steps),
        in_specs=[pl.BlockSpec((None, nr, d), seq3),
                  pl.BlockSpec((nr, 1), lambda s, j, pt: (0, 0)),
                  pl.BlockSpec((None, page, n_heads, d), seq4),
                  pl.BlockSpec((None, page, n_heads, d), seq4)]
                 + [page_spec(p) for p in range(pages_per_step)] * 2,
        out_specs=pl.BlockSpec((None, nr, d), seq3),
        scratch_shapes=[pltpu.VMEM((nr, 1), F32)],
    )
    return pl.pallas_call(
        functools.partial(_sb_sample_body, pages_per_step=pages_per_step, n_new=n_new),
        grid_spec=grid_spec,
        out_shape=jax.ShapeDtypeStruct((n_seq, nr, d), F32),
        compiler_params=_params("parallel", "arbitrary"),
        name=name,
    )(page_table, q_rows, bias_rows, knew, vnew, *([cache_k] * pages_per_step), *([cache_v] * pages_per_step))


DIL_BLOCK = 128


def _dil_unit(start, prev_in_block, prev_ok, dil, q_ref, kc_ref, kp_ref, vc_ref, vp_ref, og_ref, lg_ref, d, scale):
    t = DIL_BLOCK
    rows = q_ref.shape[0]
    span = t * dil

    def idx(s0):
        return pl.ds(s0, t) if dil == 1 else pl.ds(s0, t, stride=dil)

    first_slot = lax.broadcasted_iota(I32, (t, LANES), 1) < d
    u = lax.broadcasted_iota(I32, (2 * t, 2 * t), 0) % t
    c = lax.broadcasted_iota(I32, (2 * t, 2 * t), 1)
    band_prev = jnp.logical_and(c < t, c >= u)
    band_cur = jnp.logical_and(c >= t, c - t <= u)
    if prev_ok is not None:
        band_prev = jnp.logical_and(band_prev, prev_ok)

    q2 = q_ref[idx(start), :] * scale
    if prev_in_block:
        k_prev, v_prev = kc_ref[idx(start - span), :], vc_ref[idx(start - span), :]
    else:
        k_prev, v_prev = kp_ref[idx(start + rows - span), :], vp_ref[idx(start + rows - span), :]
    qq = jnp.concatenate([jnp.where(first_slot, q2, 0.0), jnp.where(first_slot, 0.0, q2)], axis=0)
    kk = jnp.concatenate([k_prev, kc_ref[idx(start), :]], axis=0).astype(BF16)
    vv = jnp.concatenate([v_prev, vc_ref[idx(start), :]], axis=0).astype(BF16)
    s = jnp.where(jnp.logical_or(band_prev, band_cur), _dot_nt(qq.astype(BF16), kk), -jnp.inf)
    mx = jnp.max(s, axis=-1, keepdims=True)
    p = jnp.exp(s - mx)
    den = jnp.sum(p, axis=-1, keepdims=True)
    on = _dot(p.astype(BF16), vv) / den
    lse = jnp.broadcast_to(mx + jnp.log(den), on.shape)
    og_ref[idx(start), :] = jnp.where(first_slot, on[:t], on[t:])
    lg_ref[idx(start), :] = jnp.where(first_slot, lse[:t], lse[t:])


def _pairwise_loop(count, fn):
    def pair(m, carry):
        fn(2 * m)
        fn(2 * m + 1)
        return carry

    if count // 2:
        lax.fori_loop(0, count // 2, pair, 0)
    if count % 2:
        fn(count - 1)


def _dil_prompt_body(*refs, dilations, d, scale):
    n_g = len(dilations)
    ins = refs[:5 * n_g]
    o_ref = refs[5 * n_g]
    og_refs = refs[5 * n_g + 1:5 * n_g + 1 + n_g]
    lg_refs = refs[5 * n_g + 1 + n_g:]
    j = pl.program_id(2)
    rows = o_ref.shape[0]

    for g, dil in enumerate(dilations):
        unit = functools.partial(_dil_unit, dil=dil, q_ref=ins[5 * g], kc_ref=ins[5 * g + 1], kp_ref=ins[5 * g + 2],
                                 vc_ref=ins[5 * g + 3], vp_ref=ins[5 * g + 4], og_ref=og_refs[g],
                                 lg_ref=lg_refs[g], d=d, scale=scale)
        span = DIL_BLOCK * dil

        def first(r, unit=unit):
            unit(r, False, j > 0)

        def rest(n, unit=unit, dil=dil, span=span):
            start = n % dil + (n // dil + 1) * span
            if dil == 1:
                start = pl.multiple_of(start, DIL_BLOCK)
            unit(start, True, None)

        _pairwise_loop(dil, first)
        _pairwise_loop((rows // span - 1) * dil, rest)

    ls_ = [lg_ref[...] for lg_ref in lg_refs]
    mx = functools.reduce(jnp.maximum, ls_)
    es = [jnp.exp(l - mx) for l in ls_]
    num = functools.reduce(lambda a, b: a + b, [e * og_ref[...] for e, og_ref in zip(es, og_refs)])
    o_ref[...] = num / functools.reduce(lambda a, b: a + b, es)


def _dil_prompt(qkv3, dilations, n_slots, d, name):
    b, length, _ = qkv3.shape
    n_g = len(dilations)
    rows = DIL_BLOCK * max(dilations)
    sb = LANES // d
    gw = n_g * n_slots * d
    cur = lambda bi, si, j, c0: (bi, j, c0 + si)
    prev = lambda bi, si, j, c0: (bi, jnp.maximum(j - 1, 0), c0 + si)
    in_specs = []
    for g in range(n_g):
        qc, kc, vc = (g * n_slots * d) // LANES, (gw + g * n_slots * d) // LANES, (2 * gw + g * n_slots * d) // LANES
        for fn, c0 in ((cur, qc), (cur, kc), (prev, kc), (cur, vc), (prev, vc)):
            in_specs.append(pl.BlockSpec((None, rows, LANES), functools.partial(fn, c0=c0)))
    return pl.pallas_call(
        functools.partial(_dil_prompt_body, dilations=tuple(dilations), d=d, scale=d ** -0.5),
        grid=(b, n_slots // sb, length // rows),
        in_specs=in_specs,
        out_specs=pl.BlockSpec((None, rows, LANES), lambda bi, si, j: (bi, j, si)),
        out_shape=jax.ShapeDtypeStruct((b, length, n_slots * d), F32),
        scratch_shapes=[pltpu.VMEM((rows, LANES), F32)] * (2 * n_g),
        compiler_params=_params("parallel", "parallel", "arbitrary"),
        name=name,
    )(*([qkv3] * len(in_specs)))


def _dil_sample_body(qbd_ref, kbuf_ref, vbuf_ref, knew_ref, vnew_ref, o_ref, l_ref, kout_ref, vout_ref,
                     *, dil, n_new):
    w = kbuf_ref.shape[0]
    nr = qbd_ref.shape[0]
    pad = knew_ref.shape[0]
    kb = kbuf_ref[...]
    vb = vbuf_ref[...]
    kn = knew_ref[...]
    vn = vnew_ref[...]
    kout_ref[0:w - n_new, :] = kb[n_new:w, :]
    kout_ref[w - n_new:w, :] = kn[0:n_new, :]
    vout_ref[0:w - n_new, :] = vb[n_new:w, :]
    vout_ref[w - n_new:w, :] = vn[0:n_new, :]

    qbd = qbd_ref[...]
    qi = lax.broadcasted_iota(I32, (nr, w), 0) % n_new
    diff = w + qi - lax.broadcasted_iota(I32, (nr, w), 1)
    ok_buf = jnp.logical_and(diff % dil == 0, diff <= dil * DIL_BLOCK)
    qi_n = lax.broadcasted_iota(I32, (nr, pad), 0) % n_new
    jn = lax.broadcasted_iota(I32, (nr, pad), 1)
    diff_n = qi_n - jn
    ok_new = jnp.logical_and(jnp.logical_and(diff_n >= 0, diff_n % dil == 0), jn < n_new)
    s_buf = jnp.where(ok_buf, _dot_nt(qbd, kb.astype(BF16)), -jnp.inf)
    s_new = jnp.where(ok_new, _dot_nt(qbd, kn.astype(BF16)), -jnp.inf)
    mx = jnp.maximum(jnp.max(s_buf, axis=-1, keepdims=True), jnp.max(s_new, axis=-1, keepdims=True))
    p_buf = jnp.exp(s_buf - mx)
    p_new = jnp.exp(s_new - mx)
    den = jnp.sum(p_buf, axis=-1, keepdims=True) + jnp.sum(p_new, axis=-1, keepdims=True)
    acc = _dot(p_buf.astype(BF16), vb.astype(BF16)) + _dot(p_new.astype(BF16), vn.astype(BF16))
    o_ref[...] = acc / den
    l_ref[...] = jnp.broadcast_to(mx + jnp.log(den), l_ref.shape)


def _dil_sample(qbd, kbuf, vbuf, knew, vnew, dil, n_new, name):
    n_seq, w, hd = kbuf.shape
    nr = qbd.shape[1]
    pad = knew.shape[1]
    s3 = lambda s: (s, 0, 0)
    return pl.pallas_call(
        functools.partial(_dil_sample_body, dil=dil, n_new=n_new),
        grid=(n_seq,),
        in_specs=[pl.BlockSpec((None, nr, hd), s3), pl.BlockSpec((None, w, hd), s3), pl.BlockSpec((None, w, hd), s3),
                  pl.BlockSpec((None, pad, hd), s3), pl.BlockSpec((None, pad, hd), s3)],
        out_specs=[pl.BlockSpec((None, nr, hd), s3), pl.BlockSpec((None, nr, LANES), s3),
                   pl.BlockSpec((None, w, hd), s3), pl.BlockSpec((None, w, hd), s3)],
        out_shape=[jax.ShapeDtypeStruct((n_seq, nr, hd), F32), jax.ShapeDtypeStruct((n_seq, nr, LANES), F32),
                   jax.ShapeDtypeStruct((n_seq, w, hd), F32), jax.ShapeDtypeStruct((n_seq, w, hd), F32)],
        compiler_params=_params("parallel"),
        name=name,
    )(qbd, kbuf, vbuf, knew, vnew)


def _router_body(x_ref, w_ref, idx_ref, gate_ref, *, n_experts):
    xh, xm, _ = _split3(x_ref[...])
    wh, wm, _ = _split3(w_ref[...])
    logits = _dot(xh, wh) + _dot(xm, wh) + _dot(xh, wm)
    lane = lax.broadcasted_iota(I32, logits.shape, 1).astype(F32)
    logits = jnp.where(lane < n_experts, logits, -jnp.inf)
    m1 = jnp.max(logits, axis=-1, keepdims=True)
    i1 = jnp.min(jnp.where(logits == m1, lane, float(LANES)), axis=-1, keepdims=True)
    rest = jnp.where(lane == i1, -jnp.inf, logits)
    m2 = jnp.max(rest, axis=-1, keepdims=True)
    i2 = jnp.min(jnp.where(rest == m2, lane, float(LANES)), axis=-1, keepdims=True)
    e = jnp.exp(m2 - m1)
    g1 = 1.0 / (1.0 + e)
    idx_ref[...] = jnp.where(lane == 0.0, i1, jnp.where(lane == 1.0, i2, 0.0)).astype(I32)
    gate_ref[...] = jnp.where(lane == 0.0, g1, jnp.where(lane == 1.0, e * g1, 0.0))


def _router(x, w_pad, n_experts, tm, name):
    m, d = x.shape
    return pl.pallas_call(
        functools.partial(_router_body, n_experts=n_experts),
        grid=(m // tm,),
        in_specs=[pl.BlockSpec((tm, d), lambda i: (i, 0)), pl.BlockSpec((d, LANES), lambda i: (0, 0))],
        out_specs=[pl.BlockSpec((tm, LANES), lambda i: (i, 0))] * 2,
        out_shape=[jax.ShapeDtypeStruct((m, LANES), I32), jax.ShapeDtypeStruct((m, LANES), F32)],
        compiler_params=_params("parallel"),
        name=name,
    )(x, w_pad)


def _row_copy(src_hbm, row, dst, r, sem):
    return pltpu.make_async_copy(src_hbm.at[pl.ds(row, 1)], dst.at[pl.ds(r, 1)], sem)


def _gather_body(src_ref, x_hbm, o_ref, buf, sem, *, rows):
    i = pl.program_id(0)

    def start(step):
        slot = step % 2

        def one(r, c):
            _row_copy(x_hbm, src_ref[step * rows + r], buf.at[slot], r, sem.at[slot]).start()
            return c

        lax.fori_loop(0, rows, one, 0)

    @pl.when(i == 0)
    def _():
        start(i)

    @pl.when(i + 1 < pl.num_programs(0))
    def _():
        start(i + 1)

    slot = i % 2
    pltpu.make_async_copy(x_hbm.at[pl.ds(0, rows)], buf.at[slot], sem.at[slot]).wait()
    o_ref[...] = buf[slot].astype(BF16)


def _gather_rows(src, x, rows, name):
    n = src.shape[0]
    d = x.shape[1]
    grid_spec = pltpu.PrefetchScalarGridSpec(
        num_scalar_prefetch=1,
        grid=(n // rows,),
        in_specs=[pl.BlockSpec(memory_space=pl.ANY)],
        out_specs=pl.BlockSpec((rows, d), lambda i, s: (i, 0)),
        scratch_shapes=[pltpu.VMEM((2, rows, d), x.dtype), pltpu.SemaphoreType.DMA((2,))],
    )
    return pl.pallas_call(
        functools.partial(_gather_body, rows=rows),
        grid_spec=grid_spec,
        out_shape=jax.ShapeDtypeStruct((n, d), BF16),
        compiler_params=_params("arbitrary"),
        name=name,
    )(src, x)


def _experts_body(te_ref, tb_ref, nu_ref, x_ref, w1_ref, w3_ref, w2_ref, o_ref, acc_ref):
    i = pl.program_id(0)
    f = pl.program_id(1)

    @pl.when(i < nu_ref[0])
    def _():
        @pl.when(f == 0)
        def _():
            acc_ref[...] = jnp.zeros_like(acc_ref)

        x = x_ref[...]
        hid = _silu(_dot(x, w1_ref[...].astype(BF16))) * _dot(x, w3_ref[...].astype(BF16))
        acc_ref[...] += _dot(hid.astype(BF16), w2_ref[...].astype(BF16))

        @pl.when(f == pl.num_programs(1) - 1)
        def _():
            o_ref[...] = acc_ref[...]

    @pl.when(jnp.logical_and(i >= nu_ref[0], f == pl.num_programs(1) - 1))
    def _():
        o_ref[...] = jnp.zeros_like(o_ref)


def _experts(tile_expert, tile_block, n_used, xs, w1, w3, w2, tm, tf, name):
    n_tiles = tile_expert.shape[0]
    d = xs.shape[1]
    ff = w1.shape[2]
    nf = ff // tf

    def fcol(i, f, nu):
        return jnp.where(i < nu[0], f, nf - 1)

    grid_spec = pltpu.PrefetchScalarGridSpec(
        num_scalar_prefetch=3,
        grid=(n_tiles, nf),
        in_specs=[pl.BlockSpec((tm, d), lambda i, f, te, tb, nu: (tb[i], 0)),
                  pl.BlockSpec((None, d, tf), lambda i, f, te, tb, nu: (te[i], 0, fcol(i, f, nu))),
                  pl.BlockSpec((None, d, tf), lambda i, f, te, tb, nu: (te[i], 0, fcol(i, f, nu))),
                  pl.BlockSpec((None, tf, d), lambda i, f, te, tb, nu: (te[i], fcol(i, f, nu), 0))],
        out_specs=pl.BlockSpec((tm, d), lambda i, f, te, tb, nu: (i, 0)),
        scratch_shapes=[pltpu.VMEM((tm, d), F32)],
    )
    return pl.pallas_call(
        _experts_body,
        grid_spec=grid_spec,
        out_shape=jax.ShapeDtypeStruct((n_tiles * tm, d), F32),
        compiler_params=_params("arbitrary", "arbitrary"),
        name=name,
    )(tile_expert, tile_block, n_used, xs, w1, w3, w2)


def _combine_ln_body(slot_ref, y_hbm, x_ref, gate_ref, g_ref, b_ref, o_ref, buf1, buf2, sem, *, rows, t0, alpha):
    base = (pl.program_id(0) * rows + t0) * TOP_K

    def start(r, c):
        _row_copy(y_hbm, slot_ref[base + TOP_K * r], buf1, r, sem).start()
        _row_copy(y_hbm, slot_ref[base + TOP_K * r + 1], buf2, r, sem).start()
        return c

    lax.fori_loop(0, rows, start, 0)
    pltpu.make_async_copy(y_hbm.at[pl.ds(0, rows)], buf1, sem).wait()
    pltpu.make_async_copy(y_hbm.at[pl.ds(0, rows)], buf2, sem).wait()
    gate = gate_ref[...]
    moe = gate[:, 0:1] * buf1[...] + gate[:, 1:2] * buf2[...]
    o_ref[...] = _layer_norm(alpha * x_ref[...] + moe, g_ref[...], b_ref[...])


def _combine_ln(slot, ys, x, gate, g, b, t0, alpha, rows, name):
    m, d = x.shape
    gb = t0 // rows
    grid_spec = pltpu.PrefetchScalarGridSpec(
        num_scalar_prefetch=1,
        grid=(m // rows,),
        in_specs=[pl.BlockSpec(memory_space=pl.ANY),
                  pl.BlockSpec((rows, d), lambda i, s: (i, 0)),
                  pl.BlockSpec((rows, LANES), lambda i, s: (i + gb, 0)),
                  pl.BlockSpec((1, d), lambda i, s: (0, 0)),
                  pl.BlockSpec((1, d), lambda i, s: (0, 0))],
        out_specs=pl.BlockSpec((rows, d), lambda i, s: (i, 0)),
        scratch_shapes=[pltpu.VMEM((rows, d), F32), pltpu.VMEM((rows, d), F32), pltpu.SemaphoreType.DMA(())],
    )
    return pl.pallas_call(
        functools.partial(_combine_ln_body, rows=rows, t0=t0, alpha=alpha),
        grid_spec=grid_spec,
        out_shape=jax.ShapeDtypeStruct((m, d), F32),
        compiler_params=_params("arbitrary"),
        name=name,
    )(slot, ys, x, gate, g, b)


def _moe_ln(xp, xs_, router_w, w1, w3, w2, g, b, alpha):
    n_experts = router_w.shape[1]
    d = xp.shape[1]
    tp, ts = xp.shape[0], xs_.shape[0]
    x_all = jnp.concatenate([xp, xs_], axis=0)
    t_all = tp + ts
    rw = jnp.pad(router_w, ((0, 0), (0, LANES - n_experts)))
    idx, gate = _router(x_all, rw, n_experts, ts, "moe_router")

    tm = 1024
    n_assign = t_all * TOP_K
    e_flat = idx[:, :TOP_K].reshape(n_assign)
    onehot = (e_flat[:, None] == jnp.arange(n_experts, dtype=I32)[None, :]).astype(I32)
    csum = jnp.cumsum(onehot, axis=0)
    pos = jnp.take_along_axis(csum, e_flat[:, None], axis=1)[:, 0] - 1
    counts = csum[-1]
    padded = ((counts + tm - 1) // tm) * tm
    ends = jnp.cumsum(padded)
    offs = ends - padded
    slot = (offs[e_flat] + pos).astype(I32)
    n_tiles = (n_assign + n_experts * (tm - 1)) // tm
    src = jnp.zeros((n_tiles * tm,), I32).at[slot].set(jnp.arange(n_assign, dtype=I32) // TOP_K)
    n_used = (ends[-1] // tm).astype(I32)
    tiles = jnp.arange(n_tiles, dtype=I32)
    t_exp = jnp.minimum(jnp.sum((tiles[:, None] * tm >= ends[None, :]).astype(I32), axis=1), n_experts - 1)
    last = jnp.maximum(n_used - 1, 0)
    tile_expert = jnp.where(tiles < n_used, t_exp, t_exp[last])
    tile_block = jnp.where(tiles < n_used, tiles, last)

    x_sorted = _gather_rows(src, x_all, 256, "moe_gather")
    y_sorted = _experts(tile_expert, tile_block, n_used.reshape(1), x_sorted, w1, w3, w2, tm, 512, "moe_experts")
    yp = _combine_ln(slot, y_sorted, xp, gate, g, b, 0, alpha, 256, "moe_combine_prompt")
    ys = _combine_ln(slot, y_sorted, xs_, gate, g, b, tp, alpha, 256, "moe_combine_sample")
    return yp, ys


def _rope_tables(pos):
    half = ROT_DIM // 2
    inv = ROPE_THETA ** (-jnp.arange(0, ROT_DIM, 2, dtype=F32) / ROT_DIM)
    ang = pos.astype(F32)[:, None] * inv[None, :]
    cos, sin = jnp.cos(ang), jnp.sin(ang)
    n = pos.shape[0]
    ones = jnp.ones((n, HEAD_DIM - ROT_DIM), F32)
    zeros = jnp.zeros((n, HEAD_DIM - ROT_DIM), F32)
    zh = jnp.zeros((n, half), F32)
    reps = LANES // HEAD_DIM
    c = jnp.tile(jnp.concatenate([cos, cos, ones], axis=1), (1, reps))
    sa = jnp.tile(jnp.concatenate([-sin, zh, zeros], axis=1), (1, reps))
    sb = jnp.tile(jnp.concatenate([zh, sin, zeros], axis=1), (1, reps))
    return c, sa, sb


def _block_diag_q(q, scale):
    n_seq, n_new, nh, d = q.shape
    eye = jnp.eye(nh, dtype=F32)
    out = jnp.einsum("sihd,hg->shigd", q * scale, eye)
    return out.reshape(n_seq, nh * n_new, nh * d).astype(BF16)


def _diag_blocks(o, n_new, nh, d):
    n_seq = o.shape[0]
    o5 = o.reshape(n_seq, nh, n_new, nh, d)
    sel = jnp.stack([o5[:, h, :, h, :] for h in range(nh)], axis=2)
    return sel.reshape(n_seq * n_new, nh * d)


def kernel(x_prompt, x_sample, cache_sb_k, cache_sb_v, page_table, state_ssm, state_conv, cache_c_w128_k, cache_c_w128_v, cache_c_w512_k, cache_c_w512_v, cache_c_w2048_k, cache_c_w2048_v, w_in_a, conv_w, conv_b, dt_bias, a_log, d_skip, ssm_norm_w, sb_bias, w_out_a, ln_a_mix_g, ln_a_mix_b, ffn_w1, ffn_w3, ffn_w2, ln_a_ffn_g, ln_a_ffn_b, w_in_c, w_out_c, ln_c_mix_g, ln_c_mix_b, router_w, moe_w1, moe_w3, moe_w2, ln_c_ffn_g, ln_c_ffn_b):
    n_pr, seq, d_model = x_prompt.shape
    n_dec, dec_seq, _ = x_sample.shape
    n_pages = page_table.shape[1]
    page = cache_sb_k.shape[2]
    past_len = n_pages * page
    ssm_heads, p, n_state = state_ssm.shape[2], state_ssm.shape[3], state_ssm.shape[4]
    d_inner = ssm_heads * p
    conv_dim = state_conv.shape[3]
    sb_heads = cache_sb_k.shape[3]
    sb_w = sb_heads * HEAD_DIM
    c_slots = cache_c_w128_k.shape[3]
    c_wd = c_slots * HEAD_DIM
    n_groups = len(C_GROUPS)
    depth = w_in_a.shape[0] + w_in_c.shape[0]
    alpha = (2 * depth) ** 0.25
    tp = n_pr * seq
    ts = n_dec * dec_seq
    win_k = (cache_c_w128_k, cache_c_w512_k, cache_c_w2048_k)
    win_v = (cache_c_w128_v, cache_c_w512_v, cache_c_w2048_v)
    row = lambda v: v.reshape(1, -1)

    yp = x_prompt.reshape(tp, d_model)
    ys = x_sample.reshape(ts, d_model)
    outs = {}

    i = 0
    w = w_in_a[i]
    c0, c1, c2 = d_inner, d_inner + conv_dim, d_inner + conv_dim + ssm_heads
    w_re = jnp.concatenate([w[:, c0:c1], w[:, :c0], w[:, c2:], w[:, c1:c2],
                            jnp.zeros((d_model, LANES - ssm_heads), F32)], axis=1).astype(BF16)
    o_q = conv_dim + d_inner
    proj_p = _matmul(yp, w_re, 512, "in_proj_a_prompt")
    proj_s = _matmul(ys, w_re, ts, "in_proj_a_sample")

    pad_l = LANES - ssm_heads
    ssd_par = (conv_w[i], row(conv_b[i]), row(jnp.pad(dt_bias[i], (0, pad_l))),
               row(jnp.pad(-jnp.exp(a_log[i]), (0, pad_l))), row(jnp.repeat(d_skip[i], p)), row(ssm_norm_w[i]))
    ya_p, ssm_p = _ssd(proj_p.reshape(n_pr, seq, -1), jnp.zeros((n_pr, CONV_W - 1, conv_dim), F32),
                       jnp.zeros((n_pr, ssm_heads, p, n_state), F32), *ssd_par, SSD_CHUNK, "ssd_prompt")
    proj_s3 = proj_s.reshape(n_dec, dec_seq, -1)
    proj_s_pad = jnp.pad(proj_s3, ((0, 0), (0, SSD_CHUNK - dec_seq), (0, 0)))
    ya_s, ssm_s = _ssd(proj_s_pad, state_conv[i], state_ssm[i], *ssd_par, dec_seq, "ssd_sample")
    ya_p = ya_p.reshape(tp, d_inner)
    ya_s = ya_s[:, :dec_seq].reshape(ts, d_inner)
    conv_p = proj_p.reshape(n_pr, seq, -1)[:, seq - (CONV_W - 1):, :conv_dim]
    conv_s = jnp.concatenate([state_conv[i], proj_s3[:, :, :conv_dim]], axis=1)[:, dec_seq:]

    k_p = proj_p[:, o_q + sb_w:o_q + 2 * sb_w].reshape(n_pr, seq, sb_heads, HEAD_DIM)
    v_p = proj_p[:, o_q + 2 * sb_w:o_q + 3 * sb_w].reshape(n_pr, seq, sb_heads, HEAD_DIM)
    o_p = _sb_prompt(sb_bias[i], proj_p.reshape(n_pr, seq, -1), o_q, o_q + sb_w, o_q + 2 * sb_w,
                     sb_heads, HEAD_DIM, 256, "sb_prompt").reshape(tp, sb_w)

    q_s = proj_s[:, o_q:o_q + sb_w].reshape(n_dec, dec_seq, sb_heads, HEAD_DIM)
    k_s = proj_s[:, o_q + sb_w:o_q + 2 * sb_w].reshape(n_dec, dec_seq, sb_w)
    v_s = proj_s[:, o_q + 2 * sb_w:o_q + 3 * sb_w].reshape(n_dec, dec_seq, sb_w)
    q_rows = ((q_s * (HEAD_DIM ** -0.5)).astype(BF16).transpose(0, 2, 1, 3)
              .reshape(n_dec, sb_heads * dec_seq, HEAD_DIM))
    bias_rows = jnp.repeat(sb_bias[i], dec_seq).reshape(sb_heads * dec_seq, 1)
    pad_new = ((0, 0), (0, page - dec_seq), (0, 0), (0, 0))
    o_s = _sb_sample(page_table, q_rows, bias_rows,
                     jnp.pad(k_s.reshape(n_dec, dec_seq, sb_heads, HEAD_DIM), pad_new),
                     jnp.pad(v_s.reshape(n_dec, dec_seq, sb_heads, HEAD_DIM), pad_new),
                     cache_sb_k[i], cache_sb_v[i], 8, dec_seq, "sb_sample")
    o_s = o_s.reshape(n_dec, sb_heads, dec_seq, HEAD_DIM).transpose(0, 2, 1, 3).reshape(ts, sb_w)

    wo = w_out_a[i].astype(BF16)
    ln_g, ln_b = row(ln_a_mix_g[i]), row(ln_a_mix_b[i])
    wo2 = [wo[:d_inner], wo[d_inner:]]
    yp = _proj_ln([ya_p, o_p], wo2, yp, ln_g, ln_b, alpha, 512, "out_proj_a_prompt")
    ys = _proj_ln([ya_s, o_s], wo2, ys, ln_g, ln_b, alpha, ts, "out_proj_a_sample")
    f1, f3, f2 = ffn_w1[i].astype(BF16), ffn_w3[i].astype(BF16), ffn_w2[i].astype(BF16)
    ln_g, ln_b = row(ln_a_ffn_g[i]), row(ln_a_ffn_b[i])
    d_ff = f1.shape[1]
    tf = 2 * LANES if d_ff % (2 * LANES) == 0 else LANES
    yp = _ffn_ln(yp, f1, f3, f2, ln_g, ln_b, alpha, 1024, tf, "ffn_prompt")
    ys = _ffn_ln(ys, f1, f3, f2, ln_g, ln_b, alpha, ts, tf, "ffn_sample")

    outs["sb_kp"] = k_p[None]
    outs["sb_vp"] = v_p[None]
    outs["sb_ks"] = k_s.reshape(1, n_dec, dec_seq, sb_heads, HEAD_DIM)
    outs["sb_vs"] = v_s.reshape(1, n_dec, dec_seq, sb_heads, HEAD_DIM)
    outs["ssm_p"], outs["ssm_s"], outs["conv_p"], outs["conv_s"] = ssm_p[None], ssm_s[None], conv_p[None], conv_s[None]

    wc = w_in_c[i].astype(BF16)
    gw = n_groups * c_wd
    cos_p, sa_p, sb_p = _rope_tables(jnp.arange(seq))
    cos_s, sa_s, sb_s = _rope_tables(past_len + jnp.arange(ts) % dec_seq)
    qkv_p = _matmul_rope(yp, wc, cos_p, sa_p, sb_p, 512, "in_proj_c_prompt")
    qkv_s = _matmul_rope(ys, wc, cos_s, sa_s, sb_s, ts, "in_proj_c_sample")

    scale = HEAD_DIM ** -0.5
    o_c_p = _dil_prompt(qkv_p.reshape(n_pr, seq, -1), [dil for _, dil in C_GROUPS], c_slots, HEAD_DIM,
                        "dilated_prompt").reshape(tp, c_wd)
    o_groups_s, l_groups_s = [], []
    for g, (win, dil) in enumerate(C_GROUPS):
        cq, ck, cv = g * c_wd, gw + g * c_wd, 2 * gw + g * c_wd
        kg_p = qkv_p[:, ck:ck + c_wd]
        vg_p = qkv_p[:, cv:cv + c_wd]
        keep = min(win, seq)
        outs[f"c_kp{g}"] = kg_p.reshape(n_pr, seq, c_slots, HEAD_DIM)[:, seq - keep:][None]
        outs[f"c_vp{g}"] = vg_p.reshape(n_pr, seq, c_slots, HEAD_DIM)[:, seq - keep:][None]

        qg_s = qkv_s[:, cq:cq + c_wd].reshape(n_dec, dec_seq, c_slots, HEAD_DIM)
        pad_rows = ((0, 0), (0, DIL_BLOCK - dec_seq), (0, 0))
        kn = jnp.pad(qkv_s[:, ck:ck + c_wd].reshape(n_dec, dec_seq, c_wd), pad_rows)
        vn = jnp.pad(qkv_s[:, cv:cv + c_wd].reshape(n_dec, dec_seq, c_wd), pad_rows)
        wlen = win_k[g].shape[2]
        o_g, l_g, k_new, v_new = _dil_sample(_block_diag_q(qg_s, scale), win_k[g][i].reshape(n_dec, wlen, c_wd),
                                             win_v[g][i].reshape(n_dec, wlen, c_wd), kn, vn, dil, dec_seq,
                                             f"dilated_sample_{g}")
        o_groups_s.append(_diag_blocks(o_g, dec_seq, c_slots, HEAD_DIM))
        l_s = l_g[:, :, 0].reshape(n_dec, c_slots, dec_seq).transpose(0, 2, 1)
        l_groups_s.append(jnp.repeat(l_s.reshape(ts, c_slots), HEAD_DIM, axis=1))
        outs[f"c_ks{g}"] = k_new.reshape(1, n_dec, wlen, c_slots, HEAD_DIM)
        outs[f"c_vs{g}"] = v_new.reshape(1, n_dec, wlen, c_slots, HEAD_DIM)

    woc = w_out_c[i].astype(BF16)
    ln_g, ln_b = row(ln_c_mix_g[i]), row(ln_c_mix_b[i])
    yp = _proj_ln([o_c_p], [woc], yp, ln_g, ln_b, alpha, 512, "out_proj_c_prompt")
    ys = _merge_proj_ln(o_groups_s, l_groups_s, woc, ys, ln_g, ln_b, alpha, ts, "out_proj_c_sample")
    yp, ys = _moe_ln(yp, ys, router_w[i], moe_w1[i], moe_w3[i], moe_w2[i],
                     row(ln_c_ffn_g[i]), row(ln_c_ffn_b[i]), alpha)

    res = [yp.reshape(n_pr, seq, d_model), ys.reshape(n_dec, dec_seq, d_model),
           outs["sb_kp"], outs["sb_vp"], outs["sb_ks"], outs["sb_vs"],
           outs["ssm_p"], outs["ssm_s"], outs["conv_p"], outs["conv_s"]]
    for g in range(n_groups):
        res += [outs[f"c_kp{g}"], outs[f"c_vp{g}"], outs[f"c_ks{g}"], outs[f"c_vs{g}"]]
    return tuple(res)
```

```python
import functools
import math

import numpy as np
import jax
import jax.numpy as jnp
from jax import lax
from jax.experimental import pallas as pl
from jax.experimental.pallas import tpu as pltpu

F32 = jnp.float32
BF16 = jnp.bfloat16
I32 = jnp.int32

HEAD_DIM = 64
SSM_HEAD_DIM = 64
SSM_GROUPS = 2
SSM_STATE = 128
CONV_W = 4
SSD_CHUNK = 128
C_GROUPS = ((128, 1), (512, 4), (2048, 16))
ROT_DIM = HEAD_DIM // 4
ROPE_THETA = 500000.0
TOP_K = 2
LN_EPS = 1e-5
RMS_EPS = 1e-6

LANES = 128
SUBLANES = 8
VMEM_LIMIT_BYTES = 56 * 1024 * 1024


def _params(*semantics):
    return pltpu.CompilerParams(dimension_semantics=semantics, vmem_limit_bytes=VMEM_LIMIT_BYTES)


def _dot(a, b):
    return jnp.dot(a, b, preferred_element_type=F32)


def _dot_nt(a, b):
    return lax.dot_general(a, b, (((1,), (1,)), ((), ())), preferred_element_type=F32)


def _layer_norm(y, g, b):
    mu = jnp.mean(y, axis=-1, keepdims=True)
    d = y - mu
    var = jnp.mean(d * d, axis=-1, keepdims=True)
    return d * lax.rsqrt(var + LN_EPS) * g + b


def _silu(a):
    return a * jax.nn.sigmoid(a)


def _split3(a):
    hi = a.astype(BF16)
    r = a - hi.astype(F32)
    mid = r.astype(BF16)
    lo = (r - mid.astype(F32)).astype(BF16)
    return hi, mid, lo


def _mm_body(x_ref, w_ref, o_ref):
    o_ref[...] = _dot(x_ref[...].astype(BF16), w_ref[...])


def _matmul(x, w, tm, name):
    m, k = x.shape
    n = w.shape[1]
    return pl.pallas_call(
        _mm_body,
        grid=(m // tm,),
        in_specs=[pl.BlockSpec((tm, k), lambda i: (i, 0)), pl.BlockSpec((k, n), lambda i: (0, 0))],
        out_specs=pl.BlockSpec((tm, n), lambda i: (i, 0)),
        out_shape=jax.ShapeDtypeStruct((m, n), F32),
        compiler_params=_params("parallel"),
        name=name,
    )(x, w)


def _mm_rope_body(x_ref, w_ref, cos_ref, sa_ref, sb_ref, o_ref, *, n_rot_tiles):
    y = _dot(x_ref[...].astype(BF16), w_ref[...])
    j = pl.program_id(1)

    @pl.when(j < n_rot_tiles)
    def _():
        cos = cos_ref[...]
        sa = sa_ref[...]
        sb = sb_ref[...]
        for c in range(y.shape[1] // LANES):
            yc = y[:, c * LANES:(c + 1) * LANES]
            o_ref[:, c * LANES:(c + 1) * LANES] = (
                yc * cos + pltpu.roll(yc, LANES - ROT_DIM // 2, 1) * sa + pltpu.roll(yc, ROT_DIM // 2, 1) * sb)

    @pl.when(j >= n_rot_tiles)
    def _():
        o_ref[...] = y


def _matmul_rope(x, w, cos, sa, sb, tm, name):
    m, k = x.shape
    n = w.shape[1]
    tn = n // 3
    nb = cos.shape[0] // tm
    return pl.pallas_call(
        functools.partial(_mm_rope_body, n_rot_tiles=2),
        grid=(m // tm, 3),
        in_specs=[pl.BlockSpec((tm, k), lambda i, j: (i, 0)),
                  pl.BlockSpec((k, tn), lambda i, j: (0, j)),
                  pl.BlockSpec((tm, LANES), lambda i, j: (i % nb, 0)),
                  pl.BlockSpec((tm, LANES), lambda i, j: (i % nb, 0)),
                  pl.BlockSpec((tm, LANES), lambda i, j: (i % nb, 0))],
        out_specs=pl.BlockSpec((tm, tn), lambda i, j: (i, j)),
        out_shape=jax.ShapeDtypeStruct((m, n), F32),
        compiler_params=_params("parallel", "arbitrary"),
        name=name,
    )(x, w, cos, sa, sb)


def _proj_ln_body(*refs, alpha, n_in):
    h_refs, w_refs = refs[:n_in], refs[n_in:2 * n_in]
    r_ref, g_ref, b_ref, o_ref = refs[2 * n_in:]
    m = _dot(h_refs[0][...].astype(BF16), w_refs[0][...])
    for h_ref, w_ref in zip(h_refs[1:], w_refs[1:]):
        m = m + _dot(h_ref[...].astype(BF16), w_ref[...])
    o_ref[...] = _layer_norm(alpha * r_ref[...] + m, g_ref[...], b_ref[...])


def _proj_ln(hs, ws, resid, g, b, alpha, tm, name):
    m, d = resid.shape
    row = lambda i: (i, 0)
    fix = lambda i: (0, 0)
    return pl.pallas_call(
        functools.partial(_proj_ln_body, alpha=alpha, n_in=len(hs)),
        grid=(m // tm,),
        in_specs=[pl.BlockSpec((tm, h.shape[1]), row) for h in hs]
                 + [pl.BlockSpec((w.shape[0], d), fix) for w in ws]
                 + [pl.BlockSpec((tm, d), row), pl.BlockSpec((1, d), fix), pl.BlockSpec((1, d), fix)],
        out_specs=pl.BlockSpec((tm, d), row),
        out_shape=jax.ShapeDtypeStruct((m, d), F32),
        compiler_params=_params("parallel"),
        name=name,
    )(*hs, *ws, resid, g, b)


def _merge_proj_ln_body(o0_ref, o1_ref, o2_ref, l0_ref, l1_ref, l2_ref, w_ref, r_ref, g_ref, b_ref, o_ref, *, alpha):
    l0, l1, l2 = l0_ref[...], l1_ref[...], l2_ref[...]
    mx = jnp.maximum(jnp.maximum(l0, l1), l2)
    e0, e1, e2 = jnp.exp(l0 - mx), jnp.exp(l1 - mx), jnp.exp(l2 - mx)
    o = (e0 * o0_ref[...] + e1 * o1_ref[...] + e2 * o2_ref[...]) / (e0 + e1 + e2)
    m = _dot(o.astype(BF16), w_ref[...])
    o_ref[...] = _layer_norm(alpha * r_ref[...] + m, g_ref[...], b_ref[...])


def _merge_proj_ln(os_, ls_, w, resid, g, b, alpha, tm, name):
    m, d = resid.shape
    k = w.shape[0]
    row = lambda i: (i, 0)
    fix = lambda i: (0, 0)
    return pl.pallas_call(
        functools.partial(_merge_proj_ln_body, alpha=alpha),
        grid=(m // tm,),
        in_specs=[pl.BlockSpec((tm, k), row)] * 6 + [pl.BlockSpec((k, d), fix), pl.BlockSpec((tm, d), row),
                                                     pl.BlockSpec((1, d), fix), pl.BlockSpec((1, d), fix)],
        out_specs=pl.BlockSpec((tm, d), row),
        out_shape=jax.ShapeDtypeStruct((m, d), F32),
        compiler_params=_params("parallel"),
        name=name,
    )(*os_, *ls_, w, resid, g, b)


def _ffn_ln_body(x_ref, w1_ref, w3_ref, w2_ref, g_ref, b_ref, o_ref, xb_ref, acc_ref, *, alpha):
    f = pl.program_id(1)

    @pl.when(f == 0)
    def _():
        xb_ref[...] = x_ref[...].astype(BF16)
        acc_ref[...] = jnp.zeros_like(acc_ref)

    xb = xb_ref[...]
    hid = _silu(_dot(xb, w1_ref[...])) * _dot(xb, w3_ref[...])
    acc_ref[...] += _dot(hid.astype(BF16), w2_ref[...])

    @pl.when(f == pl.num_programs(1) - 1)
    def _():
        o_ref[...] = _layer_norm(alpha * x_ref[...] + acc_ref[...], g_ref[...], b_ref[...])


def _ffn_ln(x, w1, w3, w2, g, b, alpha, tm, tf, name):
    m, d = x.shape
    ff = w1.shape[1]
    return pl.pallas_call(
        functools.partial(_ffn_ln_body, alpha=alpha),
        grid=(m // tm, ff // tf),
        in_specs=[pl.BlockSpec((tm, d), lambda i, f: (i, 0)),
                  pl.BlockSpec((d, tf), lambda i, f: (0, f)),
                  pl.BlockSpec((d, tf), lambda i, f: (0, f)),
                  pl.BlockSpec((tf, d), lambda i, f: (f, 0)),
                  pl.BlockSpec((1, d), lambda i, f: (0, 0)),
                  pl.BlockSpec((1, d), lambda i, f: (0, 0))],
        out_specs=pl.BlockSpec((tm, d), lambda i, f: (i, 0)),
        out_shape=jax.ShapeDtypeStruct((m, d), F32),
        scratch_shapes=[pltpu.VMEM((tm, d), BF16), pltpu.VMEM((tm, d), F32)],
        compiler_params=_params("parallel", "arbitrary"),
        name=name,
    )(x, w1, w3, w2, g, b)


def _ssd_body(xbc_ref, z_ref, dt_ref, conv0_ref, h0_ref, cw_ref, cb_ref, dtb_ref, aneg_ref, dsk_ref, nw_ref,
              y_ref, hn_ref, xpad_ref, *, q, valid_len, n_heads, d_inner):
    c = pl.program_id(1)
    conv_dim = xpad_ref.shape[1]
    gn = SSM_GROUPS * SSM_STATE
    heads_per_group = n_heads // SSM_GROUPS
    p = SSM_HEAD_DIM

    @pl.when(c == 0)
    def _():
        xpad_ref[0:SUBLANES, :] = jnp.zeros((SUBLANES, conv_dim), F32)
        xpad_ref[SUBLANES - (CONV_W - 1):SUBLANES, :] = conv0_ref[...]
        hn_ref[...] = h0_ref[...]

    x_raw = xbc_ref[...]
    xpad_ref[SUBLANES:SUBLANES + q, :] = x_raw
    cw = cw_ref[...]
    acc = cb_ref[...] + x_raw * cw[CONV_W - 1:CONV_W]
    for j in range(CONV_W - 1):
        off = SUBLANES - (CONV_W - 1) + j
        acc = acc + xpad_ref[off:off + q, :] * cw[j:j + 1]
    xpad_ref[SUBLANES - (CONV_W - 1):SUBLANES, :] = x_raw[q - (CONV_W - 1):q, :]
    xc = _silu(acc)
    xs = xc[:, :d_inner]
    bm = xc[:, d_inner:d_inner + gn].astype(BF16)
    cm = xc[:, d_inner + gn:d_inner + 2 * gn].astype(BF16)

    t = dt_ref[...] + dtb_ref[...]
    dtv = jnp.maximum(t, 0.0) + jnp.log(1.0 + jnp.exp(-jnp.abs(t)))
    rows = lax.broadcasted_iota(I32, (q, q), 0)
    cols = lax.broadcasted_iota(I32, (q, q), 1)
    if valid_len < q:
        dtv = jnp.where(lax.broadcasted_iota(I32, dtv.shape, 0) < valid_len, dtv, 0.0)
    a = dtv * aneg_ref[...]
    causal = cols <= rows
    tri = jnp.where(causal, 1.0, 0.0).astype(BF16)
    a_hi, a_mid, a_lo = _split3(a)
    a_cum = _dot(tri, a_hi) + _dot(tri, a_mid) + _dot(tri, a_lo)
    a_cum_t = a_cum.T
    a_tot = a_cum[q - 1:q, :]

    ys = []
    xws = []
    for g in range(SSM_GROUPS):
        cg = cm[:, g * SSM_STATE:(g + 1) * SSM_STATE]
        bg = bm[:, g * SSM_STATE:(g + 1) * SSM_STATE]
        scores = _dot_nt(cg, bg)
        for r in range(heads_per_group):
            h = g * heads_per_group + r
            col = a_cum[:, h:h + 1]
            row = a_cum_t[h:h + 1, :]
            decay_ls = jnp.where(causal, jnp.exp(col - row), 0.0)
            w_diag = (scores * decay_ls).astype(BF16)
            xs_h = xs[:, h * p:(h + 1) * p]
            xdt = xs_h * dtv[:, h:h + 1]
            y_diag = _dot(w_diag, xdt.astype(BF16))
            y_off = _dot_nt(cg, hn_ref[h].astype(BF16)) * jnp.exp(col)
            ys.append(y_diag + y_off)
            xws.append(xdt * jnp.exp(a_tot[:, h:h + 1] - col))
    y = jnp.concatenate(ys, axis=1) + dsk_ref[...] * xs
    xw = jnp.concatenate(xws, axis=1)

    heads_per_tile = LANES // p
    for j in range(d_inner // LANES):
        xw_t = xw[:, j * LANES:(j + 1) * LANES].T.astype(BF16)
        for hh in range(heads_per_tile):
            h = j * heads_per_tile + hh
            g = h // heads_per_group
            bg = bm[:, g * SSM_STATE:(g + 1) * SSM_STATE]
            st = _dot(xw_t[hh * p:(hh + 1) * p, :], bg)
            hn_ref[h] = jnp.exp(a_tot[:, h:h + 1]) * hn_ref[h] + st

    zz = z_ref[...]
    yg = y * _silu(zz)
    gsz = d_inner // SSM_GROUPS
    outs = []
    for g in range(SSM_GROUPS):
        blk = yg[:, g * gsz:(g + 1) * gsz]
        ms = jnp.mean(blk * blk, axis=-1, keepdims=True)
        outs.append(blk * lax.rsqrt(ms + RMS_EPS))
    y_ref[...] = jnp.concatenate(outs, axis=1) * nw_ref[...]


def _ssd(proj3, conv0, h0, cw, cb, dtb, aneg, dsk, nw, valid_len, name):
    b, length, _ = proj3.shape
    n_heads, p, n = h0.shape[1], h0.shape[2], h0.shape[3]
    d_inner = n_heads * p
    conv_dim = cw.shape[1]
    q = SSD_CHUNK
    z_blk = conv_dim // d_inner
    dt_blk = (proj3.shape[2] - LANES) // LANES
    fix2 = lambda i, c: (0, 0)
    return pl.pallas_call(
        functools.partial(_ssd_body, q=q, valid_len=valid_len, n_heads=n_heads, d_inner=d_inner),
        grid=(b, length // q),
        in_specs=[pl.BlockSpec((None, q, conv_dim), lambda i, c: (i, c, 0)),
                  pl.BlockSpec((None, q, d_inner), lambda i, c: (i, c, z_blk)),
                  pl.BlockSpec((None, q, LANES), lambda i, c: (i, c, dt_blk)),
                  pl.BlockSpec((None, CONV_W - 1, conv_dim), lambda i, c: (i, 0, 0)),
                  pl.BlockSpec((None, n_heads, p, n), lambda i, c: (i, 0, 0, 0)),
                  pl.BlockSpec((CONV_W, conv_dim), fix2),
                  pl.BlockSpec((1, conv_dim), fix2),
                  pl.BlockSpec((1, LANES), fix2),
                  pl.BlockSpec((1, LANES), fix2),
                  pl.BlockSpec((1, d_inner), fix2),
                  pl.BlockSpec((1, d_inner), fix2)],
        out_specs=[pl.BlockSpec((None, q, d_inner), lambda i, c: (i, c, 0)),
                   pl.BlockSpec((None, n_heads, p, n), lambda i, c: (i, 0, 0, 0))],
        out_shape=[jax.ShapeDtypeStruct((b, length, d_inner), F32),
                   jax.ShapeDtypeStruct((b, n_heads, p, n), F32)],
        scratch_shapes=[pltpu.VMEM((SUBLANES + q, conv_dim), F32)],
        compiler_params=_params("parallel", "arbitrary"),
        name=name,
    )(proj3, proj3, proj3, conv0, h0, cw, cb, dtb, aneg, dsk, nw)


SB_SEGMENTS = SUBLANES


def _sb_weights(z, carry, valid):
    tk, tq = z.shape
    seglen = tk // SB_SEGMENTS
    fail = 1.0 / (1.0 + jnp.exp(z))
    if valid is not None:
        fail = jnp.where(valid, fail, 1.0)
    fail3 = fail.reshape(seglen, SB_SEGMENTS, tq)
    prods = [None] * (seglen + 1)
    prods[seglen] = jnp.ones((SB_SEGMENTS, tq), F32)
    for v in reversed(range(seglen)):
        prods[v] = prods[v + 1] * fail3[v]
    offs = [None] * SB_SEGMENTS
    tot = carry
    for s in reversed(range(SB_SEGMENTS)):
        offs[s] = tot
        tot = tot * prods[0][s:s + 1, :]
    offs = jnp.concatenate(offs, axis=0)
    w3 = jnp.stack([prods[v + 1] - prods[v] for v in range(seglen)], axis=0) * offs[None]
    return w3.reshape(tk, tq), tot


def _sb_prompt_body(bias_ref, q_ref, k_ref, v_ref, o_ref, kp_ref, vt_ref, *, tq, hb, d):
    i = pl.program_id(2)
    h0 = pl.program_id(1) * hb
    seglen = tq // SB_SEGMENTS
    nkb = kp_ref.shape[0]
    r = lax.broadcasted_iota(I32, (tq, tq), 0)
    key_pos = (r % SB_SEGMENTS) * seglen + r // SB_SEGMENTS
    strictly_before = key_pos < lax.broadcasted_iota(I32, (tq, tq), 1)

    @pl.when(i == 0)
    def _():
        def permute(kb, c):
            base = pl.multiple_of(kb * tq, tq)
            kp = [k_ref[pl.ds(base + v, SB_SEGMENTS, stride=seglen), :] for v in range(seglen)]
            vp = [v_ref[pl.ds(base + v, SB_SEGMENTS, stride=seglen), :] for v in range(seglen)]
            kp_ref[kb] = jnp.concatenate(kp, axis=0).astype(BF16)
            vt_ref[kb] = jnp.concatenate(vp, axis=0).T.astype(BF16)
            return c

        lax.fori_loop(0, nkb, permute, 0)

    qt_all = (q_ref[...] * (d ** -0.5)).T
    head_of_row = lax.broadcasted_iota(I32, qt_all.shape, 0) // d
    qts = [jnp.where(head_of_row == hh, qt_all, 0.0).astype(BF16) for hh in range(hb)]

    def logits(hh, kb):
        return _dot(kp_ref[kb], qts[hh]) + bias_ref[h0 + hh]

    def values(hh, kb):
        return vt_ref[kb, hh * d:(hh + 1) * d, :]

    state = []
    for hh in range(hb):
        w, carry = _sb_weights(logits(hh, i), jnp.ones((1, tq), F32), strictly_before)
        state.append((logits(hh, jnp.maximum(i - 1, 0)), w.astype(BF16), carry, jnp.zeros((d, tq), F32)))

    def step(it, st):
        kb = i - 1 - it
        out = []
        for hh in range(hb):
            z, w_prev, carry, acc = st[hh]
            acc = acc + _dot(values(hh, kb + 1), w_prev)
            z_next = logits(hh, jnp.maximum(kb - 1, 0))
            w, carry = _sb_weights(z, carry, None)
            out.append((z_next, w.astype(BF16), carry, acc))
        return tuple(out)

    state = lax.fori_loop(0, i, step, tuple(state))
    out_t = [state[hh][3] + _dot(values(hh, 0), state[hh][1]) for hh in range(hb)]
    o_ref[...] = jnp.concatenate(out_t, axis=0).T


def _sb_prompt(bias, proj3, q_col, k_col, v_col, n_heads, d, tq, name):
    b, length, _ = proj3.shape
    hb = LANES // d
    nkb = length // tq
    qc, kc, vc = q_col // LANES, k_col // LANES, v_col // LANES
    return pl.pallas_call(
        functools.partial(_sb_prompt_body, tq=tq, hb=hb, d=d),
        grid=(b, n_heads // hb, nkb),
        in_specs=[pl.BlockSpec(memory_space=pltpu.SMEM),
                  pl.BlockSpec((None, tq, LANES), lambda bi, hi, i: (bi, i, qc + hi)),
                  pl.BlockSpec((None, length, LANES), lambda bi, hi, i: (bi, 0, kc + hi)),
                  pl.BlockSpec((None, length, LANES), lambda bi, hi, i: (bi, 0, vc + hi))],
        out_specs=pl.BlockSpec((None, tq, LANES), lambda bi, hi, i: (bi, i, hi)),
        out_shape=jax.ShapeDtypeStruct((b, length, n_heads * d), F32),
        scratch_shapes=[pltpu.VMEM((nkb, tq, LANES), BF16), pltpu.VMEM((nkb, LANES, tq), BF16)],
        compiler_params=_params("parallel", "parallel", "arbitrary"),
        name=name,
    )(bias, proj3, proj3, proj3)


def _sb_sample_body(pt_ref, qbd_ref, bias_ref, knew_ref, vnew_ref, *refs, pages_per_step, n_new):
    k_refs = refs[:pages_per_step]
    v_refs = refs[pages_per_step:2 * pages_per_step]
    o_ref = refs[2 * pages_per_step]
    carry_ref = refs[2 * pages_per_step + 1]
    j = pl.program_id(1)
    qbd = qbd_ref[...]
    bias = bias_ref[...]
    nr, hd = qbd.shape
    tk = knew_ref.shape[-1]
    kk = lax.broadcasted_iota(I32, (tk, tk), 0)
    ss = lax.broadcasted_iota(I32, (tk, tk), 1)
    later = jnp.where(kk > ss, 1.0, 0.0).astype(BF16)

    def blocks(kts, vts, valid):
        n = len(kts)
        kcat = jnp.concatenate([kt.reshape(hd, tk).astype(BF16) for kt in kts], axis=1)
        vcat = jnp.concatenate([vt.reshape(hd, tk).astype(BF16) for vt in vts], axis=1)
        z = _dot(qbd, kcat) + bias
        sp = jnp.maximum(z, 0.0) + jnp.log(1.0 + jnp.exp(-jnp.abs(z)))
        log_fail = -sp
        log_beta = z - sp
        if valid is not None:
            log_fail = jnp.where(valid, log_fail, 0.0)
        lf_hi = log_fail.astype(BF16)
        lf_lo = (log_fail - lf_hi.astype(F32)).astype(BF16)
        stack = lambda a: jnp.concatenate([a[:, p * tk:(p + 1) * tk] for p in range(n)], axis=0)
        inner = _dot(stack(lf_hi), later) + _dot(stack(lf_lo), later)
        carry = carry_ref[...]
        tails = []
        for p in range(n):
            tails.append(inner[p * nr:(p + 1) * nr] + carry)
            carry = carry + jnp.sum(log_fail[:, p * tk:(p + 1) * tk], axis=-1, keepdims=True)
        carry_ref[...] = carry
        w = jnp.exp(log_beta + jnp.concatenate(tails, axis=1))
        if valid is not None:
            w = jnp.where(valid, w, 0.0)
        o_ref[...] += _dot_nt(w.astype(BF16), vcat)

    @pl.when(j == 0)
    def _():
        carry_ref[...] = jnp.zeros_like(carry_ref)
        o_ref[...] = jnp.zeros_like(o_ref)
        qi = lax.broadcasted_iota(I32, (nr, tk), 0) % n_new
        jn = lax.broadcasted_iota(I32, (nr, tk), 1)
        blocks([knew_ref[...]], [vnew_ref[...]], jn < qi)

    blocks([k_ref[...] for k_ref in k_refs], [v_ref[...] for v_ref in v_refs], None)


def _sb_sample(page_table, qbd, bias_rows, knew_t, vnew_t, cache_kt, cache_vt, pages_per_step, n_new, name):
    n_seq, n_pages = page_table.shape
    nr, hd = qbd.shape[1], qbd.shape[2]
    n_heads, d, page = cache_kt.shape[1], cache_kt.shape[2], cache_kt.shape[3]
    steps = n_pages // pages_per_step

    def page_spec(p):
        return pl.BlockSpec((None, n_heads, d, page),
                            lambda s, j, pt: (pt[s, n_pages - 1 - (j * pages_per_step + p)], 0, 0, 0))

    seq3 = lambda s, j, pt: (s, 0, 0)
    seq4 = lambda s, j, pt: (s, 0, 0, 0)
    grid_spec = pltpu.PrefetchScalarGridSpec(
        num_scalar_prefetch=1,
        grid=(n_seq, steps),
        in_specs=[pl.BlockSpec((None, nr, hd), seq3),
                  pl.BlockSpec((nr, 1), lambda s, j, pt: (0, 0)),
                  pl.BlockSpec((None, n_heads, d, page), seq4),
                  pl.BlockSpec((None, n_heads, d, page), seq4)]
                 + [page_spec(p) for p in range(pages_per_step)] * 2,
        out_specs=pl.BlockSpec((None, nr, hd), seq3),
        scratch_shapes=[pltpu.VMEM((nr, 1), F32)],
    )
    return pl.pallas_call(
        functools.partial(_sb_sample_body, pages_per_step=pages_per_step, n_new=n_new),
        grid_spec=grid_spec,
        out_shape=jax.ShapeDtypeStruct((n_seq, nr, hd), F32),
        compiler_params=_params("parallel", "arbitrary"),
        name=name,
    )(page_table, qbd, bias_rows, knew_t, vnew_t, *([cache_kt] * pages_per_step), *([cache_vt] * pages_per_step))


DIL_BLOCK = 128


def _dil_unit(start, prev_in_block, prev_ok, dil, q_ref, kc_ref, kp_ref, vc_ref, vp_ref, og_ref, lg_ref, d, scale):
    t = DIL_BLOCK
    rows = q_ref.shape[0]
    span = t * dil

    def idx(s0):
        return pl.ds(s0, t) if dil == 1 else pl.ds(s0, t, stride=dil)

    first_slot = lax.broadcasted_iota(I32, (t, LANES), 1) < d
    u = lax.broadcasted_iota(I32, (2 * t, 2 * t), 0) % t
    c = lax.broadcasted_iota(I32, (2 * t, 2 * t), 1)
    band_prev = jnp.logical_and(c < t, c >= u)
    band_cur = jnp.logical_and(c >= t, c - t <= u)
    if prev_ok is not None:
        band_prev = jnp.logical_and(band_prev, prev_ok)

    q2 = q_ref[idx(start), :] * scale
    if prev_in_block:
        k_prev, v_prev = kc_ref[idx(start - span), :], vc_ref[idx(start - span), :]
    else:
        k_prev, v_prev = kp_ref[idx(start + rows - span), :], vp_ref[idx(start + rows - span), :]
    qq = jnp.concatenate([jnp.where(first_slot, q2, 0.0), jnp.where(first_slot, 0.0, q2)], axis=0)
    kk = jnp.concatenate([k_prev, kc_ref[idx(start), :]], axis=0).astype(BF16)
    vv = jnp.concatenate([v_prev, vc_ref[idx(start), :]], axis=0).astype(BF16)
    s = jnp.where(jnp.logical_or(band_prev, band_cur), _dot_nt(qq.astype(BF16), kk), -jnp.inf)
    mx = jnp.max(s, axis=-1, keepdims=True)
    p = jnp.exp(s - mx)
    den = jnp.sum(p, axis=-1, keepdims=True)
    on = _dot(p.astype(BF16), vv) / den
    lse = jnp.broadcast_to(mx + jnp.log(den), on.shape)
    og_ref[idx(start), :] = jnp.where(first_slot, on[:t], on[t:])
    lg_ref[idx(start), :] = jnp.where(first_slot, lse[:t], lse[t:])


def _pairwise_loop(count, fn):
    def pair(m, carry):
        fn(2 * m)
        fn(2 * m + 1)
        return carry

    if count // 2:
        lax.fori_loop(0, count // 2, pair, 0)
    if count % 2:
        fn(count - 1)


def _dil_prompt_body(*refs, dilations, d, scale):
    n_g = len(dilations)
    ins = refs[:5 * n_g]
    o_ref = refs[5 * n_g]
    og_refs = refs[5 * n_g + 1:5 * n_g + 1 + n_g]
    lg_refs = refs[5 * n_g + 1 + n_g:]
    j = pl.program_id(2)
    rows = o_ref.shape[0]

    for g, dil in enumerate(dilations):
        unit = functools.partial(_dil_unit, dil=dil, q_ref=ins[5 * g], kc_ref=ins[5 * g + 1], kp_ref=ins[5 * g + 2],
                                 vc_ref=ins[5 * g + 3], vp_ref=ins[5 * g + 4], og_ref=og_refs[g],
                                 lg_ref=lg_refs[g], d=d, scale=scale)
        span = DIL_BLOCK * dil

        def first(r, unit=unit):
            unit(r, False, j > 0)

        def rest(n, unit=unit, dil=dil, span=span):
            start = n % dil + (n // dil + 1) * span
            if dil == 1:
                start = pl.multiple_of(start, DIL_BLOCK)
            unit(start, True, None)

        _pairwise_loop(dil, first)
        _pairwise_loop((rows // span - 1) * dil, rest)

    ls_ = [lg_ref[...] for lg_ref in lg_refs]
    mx = functools.reduce(jnp.maximum, ls_)
    es = [jnp.exp(l - mx) for l in ls_]
    num = functools.reduce(lambda a, b: a + b, [e * og_ref[...] for e, og_ref in zip(es, og_refs)])
    o_ref[...] = num / functools.reduce(lambda a, b: a + b, es)


def _dil_prompt(qkv3, dilations, n_slots, d, name):
    b, length, _ = qkv3.shape
    n_g = len(dilations)
    rows = DIL_BLOCK * max(dilations)
    sb = LANES // d
    gw = n_g * n_slots * d
    cur = lambda bi, si, j, c0: (bi, j, c0 + si)
    prev = lambda bi, si, j, c0: (bi, jnp.maximum(j - 1, 0), c0 + si)
    in_specs = []
    for g in range(n_g):
        qc, kc, vc = (g * n_slots * d) // LANES, (gw + g * n_slots * d) // LANES, (2 * gw + g * n_slots * d) // LANES
        for fn, c0 in ((cur, qc), (cur, kc), (prev, kc), (cur, vc), (prev, vc)):
            in_specs.append(pl.BlockSpec((None, rows, LANES), functools.partial(fn, c0=c0)))
    return pl.pallas_call(
        functools.partial(_dil_prompt_body, dilations=tuple(dilations), d=d, scale=d ** -0.5),
        grid=(b, n_slots // sb, length // rows),
        in_specs=in_specs,
        out_specs=pl.BlockSpec((None, rows, LANES), lambda bi, si, j: (bi, j, si)),
        out_shape=jax.ShapeDtypeStruct((b, length, n_slots * d), F32),
        scratch_shapes=[pltpu.VMEM((rows, LANES), F32)] * (2 * n_g),
        compiler_params=_params("parallel", "parallel", "arbitrary"),
        name=name,
    )(*([qkv3] * len(in_specs)))


def _dil_sample_body(qbd_ref, kbuf_ref, vbuf_ref, knew_ref, vnew_ref, o_ref, l_ref, kout_ref, vout_ref,
                     *, dil, n_new):
    n_slots, d, w = kbuf_ref.shape
    hd = n_slots * d
    nr = qbd_ref.shape[0]
    pad = knew_ref.shape[1]
    kb = kbuf_ref[...].reshape(hd, w)
    vb = vbuf_ref[...].reshape(hd, w)
    kn = knew_ref[...]
    vn = vnew_ref[...]
    tail_lane = lax.broadcasted_iota(I32, (hd, pad), 1) >= pad - n_new
    for src, new, out_ref in ((kb, kn, kout_ref), (vb, vn, vout_ref)):
        slid = pltpu.roll(src, w - n_new, 1)
        last = jnp.where(tail_lane, pltpu.roll(new, pad - n_new, 1), slid[:, w - pad:])
        if w > pad:
            out_ref[:, :, 0:w - pad] = slid[:, 0:w - pad].reshape(n_slots, d, w - pad)
        out_ref[:, :, w - pad:w] = last.reshape(n_slots, d, pad)

    qbd = qbd_ref[...]
    qi = lax.broadcasted_iota(I32, (nr, w), 0) % n_new
    diff = w + qi - lax.broadcasted_iota(I32, (nr, w), 1)
    ok_buf = jnp.logical_and(diff % dil == 0, diff <= dil * DIL_BLOCK)
    qi_n = lax.broadcasted_iota(I32, (nr, pad), 0) % n_new
    jn = lax.broadcasted_iota(I32, (nr, pad), 1)
    diff_n = qi_n - jn
    ok_new = jnp.logical_and(jnp.logical_and(diff_n >= 0, diff_n % dil == 0), jn < n_new)
    s_buf = jnp.where(ok_buf, _dot(qbd, kb.astype(BF16)), -jnp.inf)
    s_new = jnp.where(ok_new, _dot(qbd, kn.astype(BF16)), -jnp.inf)
    mx = jnp.maximum(jnp.max(s_buf, axis=-1, keepdims=True), jnp.max(s_new, axis=-1, keepdims=True))
    p_buf = jnp.exp(s_buf - mx)
    p_new = jnp.exp(s_new - mx)
    den = jnp.sum(p_buf, axis=-1, keepdims=True) + jnp.sum(p_new, axis=-1, keepdims=True)
    acc = _dot_nt(p_buf.astype(BF16), vb.astype(BF16)) + _dot_nt(p_new.astype(BF16), vn.astype(BF16))
    o_ref[...] = acc / den
    l_ref[...] = jnp.broadcast_to(mx + jnp.log(den), l_ref.shape)


def _dil_sample(qbd, kbuf_t, vbuf_t, knew_t, vnew_t, dil, n_new, name):
    n_seq, n_slots, d, w = kbuf_t.shape
    nr, hd = qbd.shape[1], qbd.shape[2]
    pad = knew_t.shape[2]
    s3 = lambda s: (s, 0, 0)
    s4 = lambda s: (s, 0, 0, 0)
    buf_spec = pl.BlockSpec((None, n_slots, d, w), s4)
    return pl.pallas_call(
        functools.partial(_dil_sample_body, dil=dil, n_new=n_new),
        grid=(n_seq,),
        in_specs=[pl.BlockSpec((None, nr, hd), s3), buf_spec, buf_spec,
                  pl.BlockSpec((None, hd, pad), s3), pl.BlockSpec((None, hd, pad), s3)],
        out_specs=[pl.BlockSpec((None, nr, hd), s3), pl.BlockSpec((None, nr, LANES), s3), buf_spec, buf_spec],
        out_shape=[jax.ShapeDtypeStruct((n_seq, nr, hd), F32), jax.ShapeDtypeStruct((n_seq, nr, LANES), F32),
                   jax.ShapeDtypeStruct(kbuf_t.shape, F32), jax.ShapeDtypeStruct(kbuf_t.shape, F32)],
        compiler_params=_params("parallel"),
        name=name,
    )(qbd, kbuf_t, vbuf_t, knew_t, vnew_t)


def _router_body(x_ref, w_ref, idx_ref, gate_ref, *, n_experts):
    xh, xm, _ = _split3(x_ref[...])
    wh, wm, _ = _split3(w_ref[...])
    logits = _dot(xh, wh) + _dot(xm, wh) + _dot(xh, wm)
    lane = lax.broadcasted_iota(I32, logits.shape, 1).astype(F32)
    logits = jnp.where(lane < n_experts, logits, -jnp.inf)
    m1 = jnp.max(logits, axis=-1, keepdims=True)
    i1 = jnp.min(jnp.where(logits == m1, lane, float(LANES)), axis=-1, keepdims=True)
    rest = jnp.where(lane == i1, -jnp.inf, logits)
    m2 = jnp.max(rest, axis=-1, keepdims=True)
    i2 = jnp.min(jnp.where(rest == m2, lane, float(LANES)), axis=-1, keepdims=True)
    e = jnp.exp(m2 - m1)
    g1 = 1.0 / (1.0 + e)
    idx_ref[...] = jnp.where(lane == 0.0, i1, jnp.where(lane == 1.0, i2, 0.0)).astype(I32)
    gate_ref[...] = jnp.where(lane == 0.0, g1, jnp.where(lane == 1.0, e * g1, 0.0))


def _router(x, w_pad, n_experts, tm, name):
    m, d = x.shape
    return pl.pallas_call(
        functools.partial(_router_body, n_experts=n_experts),
        grid=(m // tm,),
        in_specs=[pl.BlockSpec((tm, d), lambda i: (i, 0)), pl.BlockSpec((d, LANES), lambda i: (0, 0))],
        out_specs=[pl.BlockSpec((tm, LANES), lambda i: (i, 0))] * 2,
        out_shape=[jax.ShapeDtypeStruct((m, LANES), I32), jax.ShapeDtypeStruct((m, LANES), F32)],
        compiler_params=_params("parallel"),
        name=name,
    )(x, w_pad)


def _row_copy(src_hbm, row, dst, r, sem):
    return pltpu.make_async_copy(src_hbm.at[pl.ds(row, 1)], dst.at[pl.ds(r, 1)], sem)


def _gather_body(src_ref, x_hbm, o_ref, buf, sem, *, rows):
    i = pl.program_id(0)

    def start(step):
        slot = step % 2

        def one(r, c):
            _row_copy(x_hbm, src_ref[step * rows + r], buf.at[slot], r, sem.at[slot]).start()
            return c

        lax.fori_loop(0, rows, one, 0)

    @pl.when(i == 0)
    def _():
        start(i)

    @pl.when(i + 1 < pl.num_programs(0))
    def _():
        start(i + 1)

    slot = i % 2
    pltpu.make_async_copy(x_hbm.at[pl.ds(0, rows)], buf.at[slot], sem.at[slot]).wait()
    o_ref[...] = buf[slot].astype(BF16)


def _gather_rows(src, x, rows, name):
    n = src.shape[0]
    d = x.shape[1]
    grid_spec = pltpu.PrefetchScalarGridSpec(
        num_scalar_prefetch=1,
        grid=(n // rows,),
        in_specs=[pl.BlockSpec(memory_space=pl.ANY)],
        out_specs=pl.BlockSpec((rows, d), lambda i, s: (i, 0)),
        scratch_shapes=[pltpu.VMEM((2, rows, d), x.dtype), pltpu.SemaphoreType.DMA((2,))],
    )
    return pl.pallas_call(
        functools.partial(_gather_body, rows=rows),
        grid_spec=grid_spec,
        out_shape=jax.ShapeDtypeStruct((n, d), BF16),
        compiler_params=_params("arbitrary"),
        name=name,
    )(src, x)


def _experts_body(te_ref, tb_ref, nu_ref, x_ref, w1_ref, w3_ref, w2_ref, o_ref, acc_ref):
    i = pl.program_id(0)
    f = pl.program_id(1)

    @pl.when(i < nu_ref[0])
    def _():
        @pl.when(f == 0)
        def _():
            acc_ref[...] = jnp.zeros_like(acc_ref)

        x = x_ref[...]
        hid = _silu(_dot(x, w1_ref[...].astype(BF16))) * _dot(x, w3_ref[...].astype(BF16))
        acc_ref[...] += _dot(hid.astype(BF16), w2_ref[...].astype(BF16))

        @pl.when(f == pl.num_programs(1) - 1)
        def _():
            o_ref[...] = acc_ref[...]

    @pl.when(jnp.logical_and(i >= nu_ref[0], f == pl.num_programs(1) - 1))
    def _():
        o_ref[...] = jnp.zeros_like(o_ref)


def _experts(tile_expert, tile_block, n_used, xs, w1, w3, w2, tm, tf, name):
    n_tiles = tile_expert.shape[0]
    d = xs.shape[1]
    ff = w1.shape[2]
    nf = ff // tf

    def fcol(i, f, nu):
        return jnp.where(i < nu[0], f, nf - 1)

    grid_spec = pltpu.PrefetchScalarGridSpec(
        num_scalar_prefetch=3,
        grid=(n_tiles, nf),
        in_specs=[pl.BlockSpec((tm, d), lambda i, f, te, tb, nu: (tb[i], 0)),
                  pl.BlockSpec((None, d, tf), lambda i, f, te, tb, nu: (te[i], 0, fcol(i, f, nu))),
                  pl.BlockSpec((None, d, tf), lambda i, f, te, tb, nu: (te[i], 0, fcol(i, f, nu))),
                  pl.BlockSpec((None, tf, d), lambda i, f, te, tb, nu: (te[i], fcol(i, f, nu), 0))],
        out_specs=pl.BlockSpec((tm, d), lambda i, f, te, tb, nu: (i, 0)),
        scratch_shapes=[pltpu.VMEM((tm, d), F32)],
    )
    return pl.pallas_call(
        _experts_body,
        grid_spec=grid_spec,
        out_shape=jax.ShapeDtypeStruct((n_tiles * tm, d), F32),
        compiler_params=_params("arbitrary", "arbitrary"),
        name=name,
    )(tile_expert, tile_block, n_used, xs, w1, w3, w2)


def _combine_ln_body(slot_ref, y_hbm, x_ref, gate_ref, g_ref, b_ref, o_ref, buf1, buf2, sem, *, rows, t0, alpha):
    base = (pl.program_id(0) * rows + t0) * TOP_K

    def start(r, c):
        _row_copy(y_hbm, slot_ref[base + TOP_K * r], buf1, r, sem).start()
        _row_copy(y_hbm, slot_ref[base + TOP_K * r + 1], buf2, r, sem).start()
        return c

    lax.fori_loop(0, rows, start, 0)
    pltpu.make_async_copy(y_hbm.at[pl.ds(0, rows)], buf1, sem).wait()
    pltpu.make_async_copy(y_hbm.at[pl.ds(0, rows)], buf2, sem).wait()
    gate = gate_ref[...]
    moe = gate[:, 0:1] * buf1[...] + gate[:, 1:2] * buf2[...]
    o_ref[...] = _layer_norm(alpha * x_ref[...] + moe, g_ref[...], b_ref[...])


def _combine_ln(slot, ys, x, gate, g, b, t0, alpha, rows, name):
    m, d = x.shape
    gb = t0 // rows
    grid_spec = pltpu.PrefetchScalarGridSpec(
        num_scalar_prefetch=1,
        grid=(m // rows,),
        in_specs=[pl.BlockSpec(memory_space=pl.ANY),
                  pl.BlockSpec((rows, d), lambda i, s: (i, 0)),
                  pl.BlockSpec((rows, LANES), lambda i, s: (i + gb, 0)),
                  pl.BlockSpec((1, d), lambda i, s: (0, 0)),
                  pl.BlockSpec((1, d), lambda i, s: (0, 0))],
        out_specs=pl.BlockSpec((rows, d), lambda i, s: (i, 0)),
        scratch_shapes=[pltpu.VMEM((rows, d), F32), pltpu.VMEM((rows, d), F32), pltpu.SemaphoreType.DMA(())],
    )
    return pl.pallas_call(
        functools.partial(_combine_ln_body, rows=rows, t0=t0, alpha=alpha),
        grid_spec=grid_spec,
        out_shape=jax.ShapeDtypeStruct((m, d), F32),
        compiler_params=_params("arbitrary"),
        name=name,
    )(slot, ys, x, gate, g, b)


def _moe_ln(xp, xs_, router_w, w1, w3, w2, g, b, alpha):
    n_experts = router_w.shape[1]
    d = xp.shape[1]
    tp, ts = xp.shape[0], xs_.shape[0]
    x_all = jnp.concatenate([xp, xs_], axis=0)
    t_all = tp + ts
    rw = jnp.pad(router_w, ((0, 0), (0, LANES - n_experts)))
    idx, gate = _router(x_all, rw, n_experts, ts, "moe_router")

    tm = 1024
    n_assign = t_all * TOP_K
    e_flat = idx[:, :TOP_K].reshape(n_assign)
    onehot = (e_flat[:, None] == jnp.arange(n_experts, dtype=I32)[None, :]).astype(I32)
    csum = jnp.cumsum(onehot, axis=0)
    pos = jnp.take_along_axis(csum, e_flat[:, None], axis=1)[:, 0] - 1
    counts = csum[-1]
    padded = ((counts + tm - 1) // tm) * tm
    ends = jnp.cumsum(padded)
    offs = ends - padded
    slot = (offs[e_flat] + pos).astype(I32)
    n_tiles = (n_assign + n_experts * (tm - 1)) // tm
    src = jnp.zeros((n_tiles * tm,), I32).at[slot].set(jnp.arange(n_assign, dtype=I32) // TOP_K)
    n_used = (ends[-1] // tm).astype(I32)
    tiles = jnp.arange(n_tiles, dtype=I32)
    t_exp = jnp.minimum(jnp.sum((tiles[:, None] * tm >= ends[None, :]).astype(I32), axis=1), n_experts - 1)
    last = jnp.maximum(n_used - 1, 0)
    tile_expert = jnp.where(tiles < n_used, t_exp, t_exp[last])
    tile_block = jnp.where(tiles < n_used, tiles, last)

    x_sorted = _gather_rows(src, x_all, 256, "moe_gather")
    y_sorted = _experts(tile_expert, tile_block, n_used.reshape(1), x_sorted, w1, w3, w2, tm, 512, "moe_experts")
    yp = _combine_ln(slot, y_sorted, xp, gate, g, b, 0, alpha, 256, "moe_combine_prompt")
    ys = _combine_ln(slot, y_sorted, xs_, gate, g, b, tp, alpha, 256, "moe_combine_sample")
    return yp, ys


def _rope_tables(pos):
    half = ROT_DIM // 2
    inv = ROPE_THETA ** (-jnp.arange(0, ROT_DIM, 2, dtype=F32) / ROT_DIM)
    ang = pos.astype(F32)[:, None] * inv[None, :]
    cos, sin = jnp.cos(ang), jnp.sin(ang)
    n = pos.shape[0]
    ones = jnp.ones((n, HEAD_DIM - ROT_DIM), F32)
    zeros = jnp.zeros((n, HEAD_DIM - ROT_DIM), F32)
    zh = jnp.zeros((n, half), F32)
    reps = LANES // HEAD_DIM
    c = jnp.tile(jnp.concatenate([cos, cos, ones], axis=1), (1, reps))
    sa = jnp.tile(jnp.concatenate([-sin, zh, zeros], axis=1), (1, reps))
    sb = jnp.tile(jnp.concatenate([zh, sin, zeros], axis=1), (1, reps))
    return c, sa, sb


def _block_diag_q(q, scale):
    n_seq, n_new, nh, d = q.shape
    eye = jnp.eye(nh, dtype=F32)
    out = jnp.einsum("sihd,hg->shigd", q * scale, eye)
    return out.reshape(n_seq, nh * n_new, nh * d).astype(BF16)


def _diag_blocks(o, n_new, nh, d):
    n_seq = o.shape[0]
    o5 = o.reshape(n_seq, nh, n_new, nh, d)
    sel = jnp.stack([o5[:, h, :, h, :] for h in range(nh)], axis=2)
    return sel.reshape(n_seq * n_new, nh * d)


def kernel(x_prompt, x_sample, cache_sb_k, cache_sb_v, page_table, state_ssm, state_conv, cache_c_w128_k, cache_c_w128_v, cache_c_w512_k, cache_c_w512_v, cache_c_w2048_k, cache_c_w2048_v, w_in_a, conv_w, conv_b, dt_bias, a_log, d_skip, ssm_norm_w, sb_bias, w_out_a, ln_a_mix_g, ln_a_mix_b, ffn_w1, ffn_w3, ffn_w2, ln_a_ffn_g, ln_a_ffn_b, w_in_c, w_out_c, ln_c_mix_g, ln_c_mix_b, router_w, moe_w1, moe_w3, moe_w2, ln_c_ffn_g, ln_c_ffn_b):
    n_pr, seq, d_model = x_prompt.shape
    n_dec, dec_seq, _ = x_sample.shape
    n_pages = page_table.shape[1]
    page = cache_sb_k.shape[2]
    past_len = n_pages * page
    ssm_heads, p, n_state = state_ssm.shape[2], state_ssm.shape[3], state_ssm.shape[4]
    d_inner = ssm_heads * p
    conv_dim = state_conv.shape[3]
    sb_heads = cache_sb_k.shape[3]
    sb_w = sb_heads * HEAD_DIM
    c_slots = cache_c_w128_k.shape[3]
    c_wd = c_slots * HEAD_DIM
    n_groups = len(C_GROUPS)
    depth = w_in_a.shape[0] + w_in_c.shape[0]
    alpha = (2 * depth) ** 0.25
    tp = n_pr * seq
    ts = n_dec * dec_seq
    win_k = (cache_c_w128_k, cache_c_w512_k, cache_c_w2048_k)
    win_v = (cache_c_w128_v, cache_c_w512_v, cache_c_w2048_v)
    row = lambda v: v.reshape(1, -1)

    yp = x_prompt.reshape(tp, d_model)
    ys = x_sample.reshape(ts, d_model)
    outs = {}

    i = 0
    w = w_in_a[i]
    c0, c1, c2 = d_inner, d_inner + conv_dim, d_inner + conv_dim + ssm_heads
    w_re = jnp.concatenate([w[:, c0:c1], w[:, :c0], w[:, c2:], w[:, c1:c2],
                            jnp.zeros((d_model, LANES - ssm_heads), F32)], axis=1).astype(BF16)
    o_q = conv_dim + d_inner
    proj_p = _matmul(yp, w_re, 512, "in_proj_a_prompt")
    proj_s = _matmul(ys, w_re, ts, "in_proj_a_sample")

    pad_l = LANES - ssm_heads
    ssd_par = (conv_w[i], row(conv_b[i]), row(jnp.pad(dt_bias[i], (0, pad_l))),
               row(jnp.pad(-jnp.exp(a_log[i]), (0, pad_l))), row(jnp.repeat(d_skip[i], p)), row(ssm_norm_w[i]))
    ya_p, ssm_p = _ssd(proj_p.reshape(n_pr, seq, -1), jnp.zeros((n_pr, CONV_W - 1, conv_dim), F32),
                       jnp.zeros((n_pr, ssm_heads, p, n_state), F32), *ssd_par, SSD_CHUNK, "ssd_prompt")
    proj_s3 = proj_s.reshape(n_dec, dec_seq, -1)
    proj_s_pad = jnp.pad(proj_s3, ((0, 0), (0, SSD_CHUNK - dec_seq), (0, 0)))
    ya_s, ssm_s = _ssd(proj_s_pad, state_conv[i], state_ssm[i], *ssd_par, dec_seq, "ssd_sample")
    ya_p = ya_p.reshape(tp, d_inner)
    ya_s = ya_s[:, :dec_seq].reshape(ts, d_inner)
    conv_p = proj_p.reshape(n_pr, seq, -1)[:, seq - (CONV_W - 1):, :conv_dim]
    conv_s = jnp.concatenate([state_conv[i], proj_s3[:, :, :conv_dim]], axis=1)[:, dec_seq:]

    k_p = proj_p[:, o_q + sb_w:o_q + 2 * sb_w].reshape(n_pr, seq, sb_heads, HEAD_DIM)
    v_p = proj_p[:, o_q + 2 * sb_w:o_q + 3 * sb_w].reshape(n_pr, seq, sb_heads, HEAD_DIM)
    o_p = _sb_prompt(sb_bias[i], proj_p.reshape(n_pr, seq, -1), o_q, o_q + sb_w, o_q + 2 * sb_w,
                     sb_heads, HEAD_DIM, 256, "sb_prompt").reshape(tp, sb_w)

    q_s = proj_s[:, o_q:o_q + sb_w].reshape(n_dec, dec_seq, sb_heads, HEAD_DIM)
    k_s = proj_s[:, o_q + sb_w:o_q + 2 * sb_w].reshape(n_dec, dec_seq, sb_w)
    v_s = proj_s[:, o_q + 2 * sb_w:o_q + 3 * sb_w].reshape(n_dec, dec_seq, sb_w)
    qbd = _block_diag_q(q_s, HEAD_DIM ** -0.5)
    bias_rows = jnp.repeat(sb_bias[i], dec_seq).reshape(sb_heads * dec_seq, 1)

    def keys_minor(a):
        return a.transpose(0, 2, 3, 1)

    pad_new = ((0, 0), (0, 0), (0, 0), (0, page - dec_seq))
    o_s = _sb_sample(page_table, qbd, bias_rows,
                     jnp.pad(keys_minor(k_s.reshape(n_dec, dec_seq, sb_heads, HEAD_DIM)), pad_new),
                     jnp.pad(keys_minor(v_s.reshape(n_dec, dec_seq, sb_heads, HEAD_DIM)), pad_new),
                     keys_minor(cache_sb_k[i]), keys_minor(cache_sb_v[i]), 16, dec_seq, "sb_sample")
    o_s = _diag_blocks(o_s, dec_seq, sb_heads, HEAD_DIM)

    wo = w_out_a[i].astype(BF16)
    ln_g, ln_b = row(ln_a_mix_g[i]), row(ln_a_mix_b[i])
    wo2 = [wo[:d_inner], wo[d_inner:]]
    yp = _proj_ln([ya_p, o_p], wo2, yp, ln_g, ln_b, alpha, 512, "out_proj_a_prompt")
    ys = _proj_ln([ya_s, o_s], wo2, ys, ln_g, ln_b, alpha, ts, "out_proj_a_sample")
    f1, f3, f2 = ffn_w1[i].astype(BF16), ffn_w3[i].astype(BF16), ffn_w2[i].astype(BF16)
    ln_g, ln_b = row(ln_a_ffn_g[i]), row(ln_a_ffn_b[i])
    d_ff = f1.shape[1]
    tf = 2 * LANES if d_ff % (2 * LANES) == 0 else LANES
    yp = _ffn_ln(yp, f1, f3, f2, ln_g, ln_b, alpha, 1024, tf, "ffn_prompt")
    ys = _ffn_ln(ys, f1, f3, f2, ln_g, ln_b, alpha, ts, tf, "ffn_sample")

    outs["sb_kp"] = k_p[None]
    outs["sb_vp"] = v_p[None]
    outs["sb_ks"] = k_s.reshape(1, n_dec, dec_seq, sb_heads, HEAD_DIM)
    outs["sb_vs"] = v_s.reshape(1, n_dec, dec_seq, sb_heads, HEAD_DIM)
    outs["ssm_p"], outs["ssm_s"], outs["conv_p"], outs["conv_s"] = ssm_p[None], ssm_s[None], conv_p[None], conv_s[None]

    wc = w_in_c[i].astype(BF16)
    gw = n_groups * c_wd
    cos_p, sa_p, sb_p = _rope_tables(jnp.arange(seq))
    cos_s, sa_s, sb_s = _rope_tables(past_len + jnp.arange(ts) % dec_seq)
    qkv_p = _matmul_rope(yp, wc, cos_p, sa_p, sb_p, 512, "in_proj_c_prompt")
    qkv_s = _matmul_rope(ys, wc, cos_s, sa_s, sb_s, ts, "in_proj_c_sample")

    scale = HEAD_DIM ** -0.5
    o_c_p = _dil_prompt(qkv_p.reshape(n_pr, seq, -1), [dil for _, dil in C_GROUPS], c_slots, HEAD_DIM,
                        "dilated_prompt").reshape(tp, c_wd)
    o_groups_s, l_groups_s = [], []
    for g, (win, dil) in enumerate(C_GROUPS):
        cq, ck, cv = g * c_wd, gw + g * c_wd, 2 * gw + g * c_wd
        kg_p = qkv_p[:, ck:ck + c_wd]
        vg_p = qkv_p[:, cv:cv + c_wd]
        keep = min(win, seq)
        outs[f"c_kp{g}"] = kg_p.reshape(n_pr, seq, c_slots, HEAD_DIM)[:, seq - keep:][None]
        outs[f"c_vp{g}"] = vg_p.reshape(n_pr, seq, c_slots, HEAD_DIM)[:, seq - keep:][None]

        qg_s = qkv_s[:, cq:cq + c_wd].reshape(n_dec, dec_seq, c_slots, HEAD_DIM)
        pad_keys = ((0, 0), (0, 0), (0, DIL_BLOCK - dec_seq))
        kn = jnp.pad(qkv_s[:, ck:ck + c_wd].reshape(n_dec, dec_seq, c_wd).transpose(0, 2, 1), pad_keys)
        vn = jnp.pad(qkv_s[:, cv:cv + c_wd].reshape(n_dec, dec_seq, c_wd).transpose(0, 2, 1), pad_keys)
        o_g, l_g, k_new, v_new = _dil_sample(_block_diag_q(qg_s, scale), keys_minor(win_k[g][i]),
                                             keys_minor(win_v[g][i]), kn, vn, dil, dec_seq,
                                             f"dilated_sample_{g}")
        o_groups_s.append(_diag_blocks(o_g, dec_seq, c_slots, HEAD_DIM))
        l_s = l_g[:, :, 0].reshape(n_dec, c_slots, dec_seq).transpose(0, 2, 1)
        l_groups_s.append(jnp.repeat(l_s.reshape(ts, c_slots), HEAD_DIM, axis=1))
        outs[f"c_ks{g}"] = k_new.transpose(0, 3, 1, 2)[None]
        outs[f"c_vs{g}"] = v_new.transpose(0, 3, 1, 2)[None]

    woc = w_out_c[i].astype(BF16)
    ln_g, ln_b = row(ln_c_mix_g[i]), row(ln_c_mix_b[i])
    yp = _proj_ln([o_c_p], [woc], yp, ln_g, ln_b, alpha, 512, "out_proj_c_prompt")
    ys = _merge_proj_ln(o_groups_s, l_groups_s, woc, ys, ln_g, ln_b, alpha, ts, "out_proj_c_sample")
    yp, ys = _moe_ln(yp, ys, router_w[i], moe_w1[i], moe_w3[i], moe_w2[i],
                     row(ln_c_ffn_g[i]), row(ln_c_ffn_b[i]), alpha)

    res = [yp.reshape(n_pr, seq, d_model), ys.reshape(n_dec, dec_seq, d_model),
           outs["sb_kp"], outs["sb_vp"], outs["sb_ks"], outs["sb_vs"],
           outs["ssm_p"], outs["ssm_s"], outs["conv_p"], outs["conv_s"]]
    for g in range(n_groups):
        res += [outs[f"c_kp{g}"], outs[f"c_vp{g}"], outs[f"c_ks{g}"], outs[f"c_vs{g}"]]
    return tuple(res)
```

```python
import functools
import math

import numpy as np
import jax
import jax.numpy as jnp
from jax import lax
from jax.experimental import pallas as pl
from jax.experimental.pallas import tpu as pltpu

F32 = jnp.float32
BF16 = jnp.bfloat16
I32 = jnp.int32

HEAD_DIM = 64
SSM_HEAD_DIM = 64
SSM_GROUPS = 2
SSM_STATE = 128
CONV_W = 4
SSD_CHUNK = 128
C_GROUPS = ((128, 1), (512, 4), (2048, 16))
ROT_DIM = HEAD_DIM // 4
ROPE_THETA = 500000.0
TOP_K = 2
LN_EPS = 1e-5
RMS_EPS = 1e-6

LANES = 128
SUBLANES = 8
VMEM_LIMIT_BYTES = 56 * 1024 * 1024


def _params(*semantics):
    return pltpu.CompilerParams(dimension_semantics=semantics, vmem_limit_bytes=VMEM_LIMIT_BYTES)


def _dot(a, b):
    return jnp.dot(a, b, preferred_element_type=F32)


def _dot_nt(a, b):
    return lax.dot_general(a, b, (((1,), (1,)), ((), ())), preferred_element_type=F32)


def _layer_norm(y, g, b):
    mu = jnp.mean(y, axis=-1, keepdims=True)
    d = y - mu
    var = jnp.mean(d * d, axis=-1, keepdims=True)
    return d * lax.rsqrt(var + LN_EPS) * g + b


def _silu(a):
    return a * jax.nn.sigmoid(a)


def _split3(a):
    hi = a.astype(BF16)
    r = a - hi.astype(F32)
    mid = r.astype(BF16)
    lo = (r - mid.astype(F32)).astype(BF16)
    return hi, mid, lo


def _mm_body(x_ref, w_ref, o_ref):
    o_ref[...] = _dot(x_ref[...].astype(BF16), w_ref[...])


def _matmul(x, w, tm, name):
    m, k = x.shape
    n = w.shape[1]
    return pl.pallas_call(
        _mm_body,
        grid=(m // tm,),
        in_specs=[pl.BlockSpec((tm, k), lambda i: (i, 0)), pl.BlockSpec((k, n), lambda i: (0, 0))],
        out_specs=pl.BlockSpec((tm, n), lambda i: (i, 0)),
        out_shape=jax.ShapeDtypeStruct((m, n), F32),
        compiler_params=_params("parallel"),
        name=name,
    )(x, w)


def _mm_rope_body(x_ref, w_ref, cos_ref, sa_ref, sb_ref, o_ref, *, n_rot_tiles):
    y = _dot(x_ref[...].astype(BF16), w_ref[...])
    j = pl.program_id(1)

    @pl.when(j < n_rot_tiles)
    def _():
        cos = cos_ref[...]
        sa = sa_ref[...]
        sb = sb_ref[...]
        for c in range(y.shape[1] // LANES):
            yc = y[:, c * LANES:(c + 1) * LANES]
            o_ref[:, c * LANES:(c + 1) * LANES] = (
                yc * cos + pltpu.roll(yc, LANES - ROT_DIM // 2, 1) * sa + pltpu.roll(yc, ROT_DIM // 2, 1) * sb)

    @pl.when(j >= n_rot_tiles)
    def _():
        o_ref[...] = y


def _matmul_rope(x, w, cos, sa, sb, tm, name):
    m, k = x.shape
    n = w.shape[1]
    tn = n // 3
    nb = cos.shape[0] // tm
    return pl.pallas_call(
        functools.partial(_mm_rope_body, n_rot_tiles=2),
        grid=(m // tm, 3),
        in_specs=[pl.BlockSpec((tm, k), lambda i, j: (i, 0)),
                  pl.BlockSpec((k, tn), lambda i, j: (0, j)),
                  pl.BlockSpec((tm, LANES), lambda i, j: (i % nb, 0)),
                  pl.BlockSpec((tm, LANES), lambda i, j: (i % nb, 0)),
                  pl.BlockSpec((tm, LANES), lambda i, j: (i % nb, 0))],
        out_specs=pl.BlockSpec((tm, tn), lambda i, j: (i, j)),
        out_shape=jax.ShapeDtypeStruct((m, n), F32),
        compiler_params=_params("parallel", "arbitrary"),
        name=name,
    )(x, w, cos, sa, sb)


def _proj_ln_body(*refs, alpha, n_in):
    h_refs, w_refs = refs[:n_in], refs[n_in:2 * n_in]
    r_ref, g_ref, b_ref, o_ref = refs[2 * n_in:]
    m = _dot(h_refs[0][...].astype(BF16), w_refs[0][...])
    for h_ref, w_ref in zip(h_refs[1:], w_refs[1:]):
        m = m + _dot(h_ref[...].astype(BF16), w_ref[...])
    o_ref[...] = _layer_norm(alpha * r_ref[...] + m, g_ref[...], b_ref[...])


def _proj_ln(hs, ws, resid, g, b, alpha, tm, name):
    m, d = resid.shape
    row = lambda i: (i, 0)
    fix = lambda i: (0, 0)
    return pl.pallas_call(
        functools.partial(_proj_ln_body, alpha=alpha, n_in=len(hs)),
        grid=(m // tm,),
        in_specs=[pl.BlockSpec((tm, h.shape[1]), row) for h in hs]
                 + [pl.BlockSpec((w.shape[0], d), fix) for w in ws]
                 + [pl.BlockSpec((tm, d), row), pl.BlockSpec((1, d), fix), pl.BlockSpec((1, d), fix)],
        out_specs=pl.BlockSpec((tm, d), row),
        out_shape=jax.ShapeDtypeStruct((m, d), F32),
        compiler_params=_params("parallel"),
        name=name,
    )(*hs, *ws, resid, g, b)


def _merge_proj_ln_body(o0_ref, o1_ref, o2_ref, l0_ref, l1_ref, l2_ref, w_ref, r_ref, g_ref, b_ref, o_ref, *, alpha):
    l0, l1, l2 = l0_ref[...], l1_ref[...], l2_ref[...]
    mx = jnp.maximum(jnp.maximum(l0, l1), l2)
    e0, e1, e2 = jnp.exp(l0 - mx), jnp.exp(l1 - mx), jnp.exp(l2 - mx)
    o = (e0 * o0_ref[...] + e1 * o1_ref[...] + e2 * o2_ref[...]) / (e0 + e1 + e2)
    m = _dot(o.astype(BF16), w_ref[...])
    o_ref[...] = _layer_norm(alpha * r_ref[...] + m, g_ref[...], b_ref[...])


def _merge_proj_ln(os_, ls_, w, resid, g, b, alpha, tm, name):
    m, d = resid.shape
    k = w.shape[0]
    row = lambda i: (i, 0)
    fix = lambda i: (0, 0)
    return pl.pallas_call(
        functools.partial(_merge_proj_ln_body, alpha=alpha),
        grid=(m // tm,),
        in_specs=[pl.BlockSpec((tm, k), row)] * 6 + [pl.BlockSpec((k, d), fix), pl.BlockSpec((tm, d), row),
                                                     pl.BlockSpec((1, d), fix), pl.BlockSpec((1, d), fix)],
        out_specs=pl.BlockSpec((tm, d), row),
        out_shape=jax.ShapeDtypeStruct((m, d), F32),
        compiler_params=_params("parallel"),
        name=name,
    )(*os_, *ls_, w, resid, g, b)


def _ffn_ln_body(x_ref, w1_ref, w3_ref, w2_ref, g_ref, b_ref, o_ref, xb_ref, acc_ref, *, alpha):
    f = pl.program_id(1)

    @pl.when(f == 0)
    def _():
        xb_ref[...] = x_ref[...].astype(BF16)
        acc_ref[...] = jnp.zeros_like(acc_ref)

    xb = xb_ref[...]
    hid = _silu(_dot(xb, w1_ref[...])) * _dot(xb, w3_ref[...])
    acc_ref[...] += _dot(hid.astype(BF16), w2_ref[...])

    @pl.when(f == pl.num_programs(1) - 1)
    def _():
        o_ref[...] = _layer_norm(alpha * x_ref[...] + acc_ref[...], g_ref[...], b_ref[...])


def _ffn_ln(x, w1, w3, w2, g, b, alpha, tm, tf, name):
    m, d = x.shape
    ff = w1.shape[1]
    return pl.pallas_call(
        functools.partial(_ffn_ln_body, alpha=alpha),
        grid=(m // tm, ff // tf),
        in_specs=[pl.BlockSpec((tm, d), lambda i, f: (i, 0)),
                  pl.BlockSpec((d, tf), lambda i, f: (0, f)),
                  pl.BlockSpec((d, tf), lambda i, f: (0, f)),
                  pl.BlockSpec((tf, d), lambda i, f: (f, 0)),
                  pl.BlockSpec((1, d), lambda i, f: (0, 0)),
                  pl.BlockSpec((1, d), lambda i, f: (0, 0))],
        out_specs=pl.BlockSpec((tm, d), lambda i, f: (i, 0)),
        out_shape=jax.ShapeDtypeStruct((m, d), F32),
        scratch_shapes=[pltpu.VMEM((tm, d), BF16), pltpu.VMEM((tm, d), F32)],
        compiler_params=_params("parallel", "arbitrary"),
        name=name,
    )(x, w1, w3, w2, g, b)


def _ssd_body(xbc_ref, z_ref, dt_ref, conv0_ref, h0_ref, cw_ref, cb_ref, dtb_ref, aneg_ref, dsk_ref, nw_ref,
              y_ref, hn_ref, xpad_ref, *, q, valid_len, n_heads, d_inner):
    c = pl.program_id(1)
    conv_dim = xpad_ref.shape[1]
    gn = SSM_GROUPS * SSM_STATE
    heads_per_group = n_heads // SSM_GROUPS
    p = SSM_HEAD_DIM

    @pl.when(c == 0)
    def _():
        xpad_ref[0:SUBLANES, :] = jnp.zeros((SUBLANES, conv_dim), F32)
        xpad_ref[SUBLANES - (CONV_W - 1):SUBLANES, :] = conv0_ref[...]
        hn_ref[...] = h0_ref[...]

    x_raw = xbc_ref[...]
    xpad_ref[SUBLANES:SUBLANES + q, :] = x_raw
    cw = cw_ref[...]
    acc = cb_ref[...] + x_raw * cw[CONV_W - 1:CONV_W]
    for j in range(CONV_W - 1):
        off = SUBLANES - (CONV_W - 1) + j
        acc = acc + xpad_ref[off:off + q, :] * cw[j:j + 1]
    xpad_ref[SUBLANES - (CONV_W - 1):SUBLANES, :] = x_raw[q - (CONV_W - 1):q, :]
    xc = _silu(acc)
    xs = xc[:, :d_inner]
    bm = xc[:, d_inner:d_inner + gn].astype(BF16)
    cm = xc[:, d_inner + gn:d_inner + 2 * gn].astype(BF16)

    t = dt_ref[...] + dtb_ref[...]
    dtv = jnp.maximum(t, 0.0) + jnp.log(1.0 + jnp.exp(-jnp.abs(t)))
    rows = lax.broadcasted_iota(I32, (q, q), 0)
    cols = lax.broadcasted_iota(I32, (q, q), 1)
    if valid_len < q:
        dtv = jnp.where(lax.broadcasted_iota(I32, dtv.shape, 0) < valid_len, dtv, 0.0)
    a = dtv * aneg_ref[...]
    causal = cols <= rows
    tri = jnp.where(causal, 1.0, 0.0).astype(BF16)
    a_hi, a_mid, a_lo = _split3(a)
    a_cum = _dot(tri, a_hi) + _dot(tri, a_mid) + _dot(tri, a_lo)
    a_cum_t = a_cum.T
    a_tot = a_cum[q - 1:q, :]

    ys = []
    xws = []
    for g in range(SSM_GROUPS):
        cg = cm[:, g * SSM_STATE:(g + 1) * SSM_STATE]
        bg = bm[:, g * SSM_STATE:(g + 1) * SSM_STATE]
        scores = _dot_nt(cg, bg)
        for r in range(heads_per_group):
            h = g * heads_per_group + r
            col = a_cum[:, h:h + 1]
            row = a_cum_t[h:h + 1, :]
            decay_ls = jnp.where(causal, jnp.exp(col - row), 0.0)
            w_diag = (scores * decay_ls).astype(BF16)
            xs_h = xs[:, h * p:(h + 1) * p]
            xdt = xs_h * dtv[:, h:h + 1]
            y_diag = _dot(w_diag, xdt.astype(BF16))
            y_off = _dot_nt(cg, hn_ref[h].astype(BF16)) * jnp.exp(col)
            ys.append(y_diag + y_off)
            xws.append(xdt * jnp.exp(a_tot[:, h:h + 1] - col))
    y = jnp.concatenate(ys, axis=1) + dsk_ref[...] * xs
    xw = jnp.concatenate(xws, axis=1)

    heads_per_tile = LANES // p
    for j in range(d_inner // LANES):
        xw_t = xw[:, j * LANES:(j + 1) * LANES].T.astype(BF16)
        for hh in range(heads_per_tile):
            h = j * heads_per_tile + hh
            g = h // heads_per_group
            bg = bm[:, g * SSM_STATE:(g + 1) * SSM_STATE]
            st = _dot(xw_t[hh * p:(hh + 1) * p, :], bg)
            hn_ref[h] = jnp.exp(a_tot[:, h:h + 1]) * hn_ref[h] + st

    zz = z_ref[...]
    yg = y * _silu(zz)
    gsz = d_inner // SSM_GROUPS
    outs = []
    for g in range(SSM_GROUPS):
        blk = yg[:, g * gsz:(g + 1) * gsz]
        ms = jnp.mean(blk * blk, axis=-1, keepdims=True)
        outs.append(blk * lax.rsqrt(ms + RMS_EPS))
    y_ref[...] = jnp.concatenate(outs, axis=1) * nw_ref[...]


def _ssd(proj3, conv0, h0, cw, cb, dtb, aneg, dsk, nw, valid_len, name):
    b, length, _ = proj3.shape
    n_heads, p, n = h0.shape[1], h0.shape[2], h0.shape[3]
    d_inner = n_heads * p
    conv_dim = cw.shape[1]
    q = SSD_CHUNK
    z_blk = conv_dim // d_inner
    dt_blk = (proj3.shape[2] - LANES) // LANES
    fix2 = lambda i, c: (0, 0)
    return pl.pallas_call(
        functools.partial(_ssd_body, q=q, valid_len=valid_len, n_heads=n_heads, d_inner=d_inner),
        grid=(b, length // q),
        in_specs=[pl.BlockSpec((None, q, conv_dim), lambda i, c: (i, c, 0)),
                  pl.BlockSpec((None, q, d_inner), lambda i, c: (i, c, z_blk)),
                  pl.BlockSpec((None, q, LANES), lambda i, c: (i, c, dt_blk)),
                  pl.BlockSpec((None, CONV_W - 1, conv_dim), lambda i, c: (i, 0, 0)),
                  pl.BlockSpec((None, n_heads, p, n), lambda i, c: (i, 0, 0, 0)),
                  pl.BlockSpec((CONV_W, conv_dim), fix2),
                  pl.BlockSpec((1, conv_dim), fix2),
                  pl.BlockSpec((1, LANES), fix2),
                  pl.BlockSpec((1, LANES), fix2),
                  pl.BlockSpec((1, d_inner), fix2),
                  pl.BlockSpec((1, d_inner), fix2)],
        out_specs=[pl.BlockSpec((None, q, d_inner), lambda i, c: (i, c, 0)),
                   pl.BlockSpec((None, n_heads, p, n), lambda i, c: (i, 0, 0, 0))],
        out_shape=[jax.ShapeDtypeStruct((b, length, d_inner), F32),
                   jax.ShapeDtypeStruct((b, n_heads, p, n), F32)],
        scratch_shapes=[pltpu.VMEM((SUBLANES + q, conv_dim), F32)],
        compiler_params=_params("parallel", "arbitrary"),
        name=name,
    )(proj3, proj3, proj3, conv0, h0, cw, cb, dtb, aneg, dsk, nw)


SB_SEGMENTS = SUBLANES


def _sb_weights(z, carry, valid, f_ref):
    tk, tq = z.shape
    seglen = tk // SB_SEGMENTS
    fail = 1.0 / (1.0 + jnp.exp(z))
    if valid is not None:
        fail = jnp.where(valid, fail, 1.0)
    f_ref[...] = fail
    seg_tot = jnp.ones((SB_SEGMENTS, tq), F32)
    for v in range(seglen):
        seg_tot = seg_tot * f_ref[v * SB_SEGMENTS:(v + 1) * SB_SEGMENTS, :]
    offs = [None] * SB_SEGMENTS
    tot = carry
    for s in reversed(range(SB_SEGMENTS)):
        offs[s] = tot
        tot = tot * seg_tot[s:s + 1, :]
    run = jnp.concatenate(offs, axis=0)
    rows = [None] * seglen
    for v in reversed(range(seglen)):
        nxt = run * f_ref[v * SB_SEGMENTS:(v + 1) * SB_SEGMENTS, :]
        rows[v] = run - nxt
        run = nxt
    return jnp.concatenate(rows, axis=0), tot


def _sb_prompt_body(bias_ref, q_ref, k_ref, v_ref, o_ref, kp_ref, vt_ref, f_ref, z_ref, w_ref, acc_ref,
                    *, tq, hb, d):
    i = pl.program_id(2)
    h0 = pl.program_id(1) * hb
    seglen = tq // SB_SEGMENTS
    nkb = kp_ref.shape[0]
    r = lax.broadcasted_iota(I32, (tq, tq), 0)
    key_pos = (r % SB_SEGMENTS) * seglen + r // SB_SEGMENTS
    strictly_before = key_pos < lax.broadcasted_iota(I32, (tq, tq), 1)

    @pl.when(i == 0)
    def _():
        def permute(kb, c):
            base = pl.multiple_of(kb * tq, tq)
            kp = [k_ref[pl.ds(base + v, SB_SEGMENTS, stride=seglen), :] for v in range(seglen)]
            vp = [v_ref[pl.ds(base + v, SB_SEGMENTS, stride=seglen), :] for v in range(seglen)]
            kp_ref[kb] = jnp.concatenate(kp, axis=0).astype(BF16)
            vt_ref[kb] = jnp.concatenate(vp, axis=0).T.astype(BF16)
            return c

        lax.fori_loop(0, nkb, permute, 0)

    qt_all = (q_ref[...] * (d ** -0.5)).T
    head_of_row = lax.broadcasted_iota(I32, qt_all.shape, 0) // d
    qts = [jnp.where(head_of_row == hh, qt_all, 0.0).astype(BF16) for hh in range(hb)]

    def logits(hh, kb):
        return _dot(kp_ref[kb], qts[hh]) + bias_ref[h0 + hh]

    def values(hh, kb):
        return vt_ref[kb, hh * d:(hh + 1) * d, :]

    carries = []
    for hh in range(hb):
        w, carry = _sb_weights(logits(hh, i), jnp.ones((1, tq), F32), strictly_before, f_ref.at[hh])
        w_ref[hh] = w.astype(BF16)
        z_ref[hh] = logits(hh, jnp.maximum(i - 1, 0))
        acc_ref[hh] = jnp.zeros((d, tq), F32)
        carries.append(carry)

    def step(it, carries):
        kb = i - 1 - it
        out = []
        for hh in range(hb):
            acc_ref[hh] += _dot(values(hh, kb + 1), w_ref[hh])
            z = z_ref[hh]
            z_ref[hh] = logits(hh, jnp.maximum(kb - 1, 0))
            w, carry = _sb_weights(z, carries[hh], None, f_ref.at[hh])
            w_ref[hh] = w.astype(BF16)
            out.append(carry)
        return tuple(out)

    lax.fori_loop(0, i, step, tuple(carries))
    out_t = [acc_ref[hh] + _dot(values(hh, 0), w_ref[hh]) for hh in range(hb)]
    o_ref[...] = jnp.concatenate(out_t, axis=0).T


def _sb_prompt(bias, proj3, q_col, k_col, v_col, n_heads, d, tq, name):
    b, length, _ = proj3.shape
    hb = LANES // d
    nkb = length // tq
    qc, kc, vc = q_col // LANES, k_col // LANES, v_col // LANES
    return pl.pallas_call(
        functools.partial(_sb_prompt_body, tq=tq, hb=hb, d=d),
        grid=(b, n_heads // hb, nkb),
        in_specs=[pl.BlockSpec(memory_space=pltpu.SMEM),
                  pl.BlockSpec((None, tq, LANES), lambda bi, hi, i: (bi, i, qc + hi)),
                  pl.BlockSpec((None, length, LANES), lambda bi, hi, i: (bi, 0, kc + hi)),
                  pl.BlockSpec((None, length, LANES), lambda bi, hi, i: (bi, 0, vc + hi))],
        out_specs=pl.BlockSpec((None, tq, LANES), lambda bi, hi, i: (bi, i, hi)),
        out_shape=jax.ShapeDtypeStruct((b, length, n_heads * d), F32),
        scratch_shapes=[pltpu.VMEM((nkb, tq, LANES), BF16), pltpu.VMEM((nkb, LANES, tq), BF16),
                        pltpu.VMEM((hb, tq, tq), F32), pltpu.VMEM((hb, tq, tq), F32),
                        pltpu.VMEM((hb, tq, tq), BF16), pltpu.VMEM((hb, d, tq), F32)],
        compiler_params=_params("parallel", "parallel", "arbitrary"),
        name=name,
    )(bias, proj3, proj3, proj3)


def _sb_sample_body(pt_ref, qbd_ref, bias_ref, knew_ref, vnew_ref, *refs, pages_per_step, n_new):
    k_refs = refs[:pages_per_step]
    v_refs = refs[pages_per_step:2 * pages_per_step]
    o_ref = refs[2 * pages_per_step]
    carry_ref = refs[2 * pages_per_step + 1]
    j = pl.program_id(1)
    qbd = qbd_ref[...]
    bias = bias_ref[...]
    nr, hd = qbd.shape
    tk = knew_ref.shape[-1]
    kk = lax.broadcasted_iota(I32, (tk, tk), 0)
    ss = lax.broadcasted_iota(I32, (tk, tk), 1)
    later = jnp.where(kk > ss, 1.0, 0.0).astype(BF16)

    def blocks(kts, vts, valid):
        n = len(kts)
        kcat = jnp.concatenate([kt.reshape(hd, tk).astype(BF16) for kt in kts], axis=1)
        vcat = jnp.concatenate([vt.reshape(hd, tk).astype(BF16) for vt in vts], axis=1)
        z = _dot(qbd, kcat) + bias
        sp = jnp.maximum(z, 0.0) + jnp.log(1.0 + jnp.exp(-jnp.abs(z)))
        log_fail = -sp
        log_beta = z - sp
        if valid is not None:
            log_fail = jnp.where(valid, log_fail, 0.0)
        lf_hi = log_fail.astype(BF16)
        lf_lo = (log_fail - lf_hi.astype(F32)).astype(BF16)
        stack = lambda a: jnp.concatenate([a[:, p * tk:(p + 1) * tk] for p in range(n)], axis=0)
        inner = _dot(stack(lf_hi), later) + _dot(stack(lf_lo), later)
        carry = carry_ref[...]
        tails = []
        for p in range(n):
            tails.append(inner[p * nr:(p + 1) * nr] + carry)
            carry = carry + jnp.sum(log_fail[:, p * tk:(p + 1) * tk], axis=-1, keepdims=True)
        carry_ref[...] = carry
        w = jnp.exp(log_beta + jnp.concatenate(tails, axis=1))
        if valid is not None:
            w = jnp.where(valid, w, 0.0)
        o_ref[...] += _dot_nt(w.astype(BF16), vcat)

    @pl.when(j == 0)
    def _():
        carry_ref[...] = jnp.zeros_like(carry_ref)
        o_ref[...] = jnp.zeros_like(o_ref)
        qi = lax.broadcasted_iota(I32, (nr, tk), 0) % n_new
        jn = lax.broadcasted_iota(I32, (nr, tk), 1)
        blocks([knew_ref[...]], [vnew_ref[...]], jn < qi)

    blocks([k_ref[...] for k_ref in k_refs], [v_ref[...] for v_ref in v_refs], None)


def _sb_sample(page_table, qbd, bias_rows, knew_t, vnew_t, cache_kt, cache_vt, pages_per_step, n_new, name):
    n_seq, n_pages = page_table.shape
    nr, hd = qbd.shape[1], qbd.shape[2]
    n_heads, d, page = cache_kt.shape[1], cache_kt.shape[2], cache_kt.shape[3]
    steps = n_pages // pages_per_step

    def page_spec(p):
        return pl.BlockSpec((None, n_heads, d, page),
                            lambda s, j, pt: (pt[s, n_pages - 1 - (j * pages_per_step + p)], 0, 0, 0))

    seq3 = lambda s, j, pt: (s, 0, 0)
    seq4 = lambda s, j, pt: (s, 0, 0, 0)
    grid_spec = pltpu.PrefetchScalarGridSpec(
        num_scalar_prefetch=1,
        grid=(n_seq, steps),
        in_specs=[pl.BlockSpec((None, nr, hd), seq3),
                  pl.BlockSpec((nr, 1), lambda s, j, pt: (0, 0)),
                  pl.BlockSpec((None, n_heads, d, page), seq4),
                  pl.BlockSpec((None, n_heads, d, page), seq4)]
                 + [page_spec(p) for p in range(pages_per_step)] * 2,
        out_specs=pl.BlockSpec((None, nr, hd), seq3),
        scratch_shapes=[pltpu.VMEM((nr, 1), F32)],
    )
    return pl.pallas_call(
        functools.partial(_sb_sample_body, pages_per_step=pages_per_step, n_new=n_new),
        grid_spec=grid_spec,
        out_shape=jax.ShapeDtypeStruct((n_seq, nr, hd), F32),
        compiler_params=_params("parallel", "arbitrary"),
        name=name,
    )(page_table, qbd, bias_rows, knew_t, vnew_t, *([cache_kt] * pages_per_step), *([cache_vt] * pages_per_step))


DIL_BLOCK = 128


def _dil_unit(start, prev_in_block, prev_ok, dil, q_ref, kc_ref, kp_ref, vc_ref, vp_ref, og_ref, lg_ref, d, scale):
    t = DIL_BLOCK
    rows = q_ref.shape[0]
    span = t * dil

    def idx(s0):
        return pl.ds(s0, t) if dil == 1 else pl.ds(s0, t, stride=dil)

    first_slot = lax.broadcasted_iota(I32, (t, LANES), 1) < d
    u = lax.broadcasted_iota(I32, (2 * t, 2 * t), 0) % t
    c = lax.broadcasted_iota(I32, (2 * t, 2 * t), 1)
    band_prev = jnp.logical_and(c < t, c >= u)
    band_cur = jnp.logical_and(c >= t, c - t <= u)
    if prev_ok is not None:
        band_prev = jnp.logical_and(band_prev, prev_ok)

    q2 = q_ref[idx(start), :] * scale
    if prev_in_block:
        k_prev, v_prev = kc_ref[idx(start - span), :], vc_ref[idx(start - span), :]
    else:
        k_prev, v_prev = kp_ref[idx(start + rows - span), :], vp_ref[idx(start + rows - span), :]
    qq = jnp.concatenate([jnp.where(first_slot, q2, 0.0), jnp.where(first_slot, 0.0, q2)], axis=0)
    kk = jnp.concatenate([k_prev, kc_ref[idx(start), :]], axis=0).astype(BF16)
    vv = jnp.concatenate([v_prev, vc_ref[idx(start), :]], axis=0).astype(BF16)
    s = jnp.where(jnp.logical_or(band_prev, band_cur), _dot_nt(qq.astype(BF16), kk), -jnp.inf)
    mx = jnp.max(s, axis=-1, keepdims=True)
    p = jnp.exp(s - mx)
    den = jnp.sum(p, axis=-1, keepdims=True)
    on = _dot(p.astype(BF16), vv) / den
    lse = jnp.broadcast_to(mx + jnp.log(den), on.shape)
    og_ref[idx(start), :] = jnp.where(first_slot, on[:t], on[t:])
    lg_ref[idx(start), :] = jnp.where(first_slot, lse[:t], lse[t:])


def _pairwise_loop(count, fn):
    def pair(m, carry):
        fn(2 * m)
        fn(2 * m + 1)
        return carry

    if count // 2:
        lax.fori_loop(0, count // 2, pair, 0)
    if count % 2:
        fn(count - 1)


def _dil_prompt_body(*refs, dilations, d, scale):
    n_g = len(dilations)
    ins = refs[:5 * n_g]
    o_ref = refs[5 * n_g]
    og_refs = refs[5 * n_g + 1:5 * n_g + 1 + n_g]
    lg_refs = refs[5 * n_g + 1 + n_g:]
    j = pl.program_id(2)
    rows = o_ref.shape[0]

    for g, dil in enumerate(dilations):
        unit = functools.partial(_dil_unit, dil=dil, q_ref=ins[5 * g], kc_ref=ins[5 * g + 1], kp_ref=ins[5 * g + 2],
                                 vc_ref=ins[5 * g + 3], vp_ref=ins[5 * g + 4], og_ref=og_refs[g],
                                 lg_ref=lg_refs[g], d=d, scale=scale)
        span = DIL_BLOCK * dil

        def first(r, unit=unit):
            unit(r, False, j > 0)

        def rest(n, unit=unit, dil=dil, span=span):
            start = n % dil + (n // dil + 1) * span
            if dil == 1:
                start = pl.multiple_of(start, DIL_BLOCK)
            unit(start, True, None)

        _pairwise_loop(dil, first)
        _pairwise_loop((rows // span - 1) * dil, rest)

    ls_ = [lg_ref[...] for lg_ref in lg_refs]
    mx = functools.reduce(jnp.maximum, ls_)
    es = [jnp.exp(l - mx) for l in ls_]
    num = functools.reduce(lambda a, b: a + b, [e * og_ref[...] for e, og_ref in zip(es, og_refs)])
    o_ref[...] = num / functools.reduce(lambda a, b: a + b, es)


def _dil_prompt(qkv3, dilations, n_slots, d, name):
    b, length, _ = qkv3.shape
    n_g = len(dilations)
    rows = DIL_BLOCK * max(dilations)
    sb = LANES // d
    gw = n_g * n_slots * d
    cur = lambda bi, si, j, c0: (bi, j, c0 + si)
    prev = lambda bi, si, j, c0: (bi, jnp.maximum(j - 1, 0), c0 + si)
    in_specs = []
    for g in range(n_g):
        qc, kc, vc = (g * n_slots * d) // LANES, (gw + g * n_slots * d) // LANES, (2 * gw + g * n_slots * d) // LANES
        for fn, c0 in ((cur, qc), (cur, kc), (prev, kc), (cur, vc), (prev, vc)):
            in_specs.append(pl.BlockSpec((None, rows, LANES), functools.partial(fn, c0=c0)))
    return pl.pallas_call(
        functools.partial(_dil_prompt_body, dilations=tuple(dilations), d=d, scale=d ** -0.5),
        grid=(b, n_slots // sb, length // rows),
        in_specs=in_specs,
        out_specs=pl.BlockSpec((None, rows, LANES), lambda bi, si, j: (bi, j, si)),
        out_shape=jax.ShapeDtypeStruct((b, length, n_slots * d), F32),
        scratch_shapes=[pltpu.VMEM((rows, LANES), F32)] * (2 * n_g),
        compiler_params=_params("parallel", "parallel", "arbitrary"),
        name=name,
    )(*([qkv3] * len(in_specs)))


def _dil_sample_body(qbd_ref, kbuf_ref, vbuf_ref, knew_ref, vnew_ref, o_ref, l_ref, kout_ref, vout_ref,
                     *, dil, n_new):
    n_slots, d, w = kbuf_ref.shape
    hd = n_slots * d
    nr = qbd_ref.shape[0]
    pad = knew_ref.shape[1]
    kb = kbuf_ref[...].reshape(hd, w)
    vb = vbuf_ref[...].reshape(hd, w)
    kn = knew_ref[...]
    vn = vnew_ref[...]
    tail_lane = lax.broadcasted_iota(I32, (hd, pad), 1) >= pad - n_new
    for src, new, out_ref in ((kb, kn, kout_ref), (vb, vn, vout_ref)):
        slid = pltpu.roll(src, w - n_new, 1)
        last = jnp.where(tail_lane, pltpu.roll(new, pad - n_new, 1), slid[:, w - pad:])
        if w > pad:
            out_ref[:, :, 0:w - pad] = slid[:, 0:w - pad].reshape(n_slots, d, w - pad)
        out_ref[:, :, w - pad:w] = last.reshape(n_slots, d, pad)

    qbd = qbd_ref[...]
    qi = lax.broadcasted_iota(I32, (nr, w), 0) % n_new
    diff = w + qi - lax.broadcasted_iota(I32, (nr, w), 1)
    ok_buf = jnp.logical_and(diff % dil == 0, diff <= dil * DIL_BLOCK)
    qi_n = lax.broadcasted_iota(I32, (nr, pad), 0) % n_new
    jn = lax.broadcasted_iota(I32, (nr, pad), 1)
    diff_n = qi_n - jn
    ok_new = jnp.logical_and(jnp.logical_and(diff_n >= 0, diff_n % dil == 0), jn < n_new)
    s_buf = jnp.where(ok_buf, _dot(qbd, kb.astype(BF16)), -jnp.inf)
    s_new = jnp.where(ok_new, _dot(qbd, kn.astype(BF16)), -jnp.inf)
    mx = jnp.maximum(jnp.max(s_buf, axis=-1, keepdims=True), jnp.max(s_new, axis=-1, keepdims=True))
    p_buf = jnp.exp(s_buf - mx)
    p_new = jnp.exp(s_new - mx)
    den = jnp.sum(p_buf, axis=-1, keepdims=True) + jnp.sum(p_new, axis=-1, keepdims=True)
    acc = _dot_nt(p_buf.astype(BF16), vb.astype(BF16)) + _dot_nt(p_new.astype(BF16), vn.astype(BF16))
    o_ref[...] = acc / den
    l_ref[...] = jnp.broadcast_to(mx + jnp.log(den), l_ref.shape)


def _dil_sample(qbd, kbuf_t, vbuf_t, knew_t, vnew_t, dil, n_new, name):
    n_seq, n_slots, d, w = kbuf_t.shape
    nr, hd = qbd.shape[1], qbd.shape[2]
    pad = knew_t.shape[2]
    s3 = lambda s: (s, 0, 0)
    s4 = lambda s: (s, 0, 0, 0)
    buf_spec = pl.BlockSpec((None, n_slots, d, w), s4)
    return pl.pallas_call(
        functools.partial(_dil_sample_body, dil=dil, n_new=n_new),
        grid=(n_seq,),
        in_specs=[pl.BlockSpec((None, nr, hd), s3), buf_spec, buf_spec,
                  pl.BlockSpec((None, hd, pad), s3), pl.BlockSpec((None, hd, pad), s3)],
        out_specs=[pl.BlockSpec((None, nr, hd), s3), pl.BlockSpec((None, nr, LANES), s3), buf_spec, buf_spec],
        out_shape=[jax.ShapeDtypeStruct((n_seq, nr, hd), F32), jax.ShapeDtypeStruct((n_seq, nr, LANES), F32),
                   jax.ShapeDtypeStruct(kbuf_t.shape, F32), jax.ShapeDtypeStruct(kbuf_t.shape, F32)],
        compiler_params=_params("parallel"),
        name=name,
    )(qbd, kbuf_t, vbuf_t, knew_t, vnew_t)


def _router_body(x_ref, w_ref, idx_ref, gate_ref, *, n_experts):
    xh, xm, _ = _split3(x_ref[...])
    wh, wm, _ = _split3(w_ref[...])
    logits = _dot(xh, wh) + _dot(xm, wh) + _dot(xh, wm)
    lane = lax.broadcasted_iota(I32, logits.shape, 1).astype(F32)
    logits = jnp.where(lane < n_experts, logits, -jnp.inf)
    m1 = jnp.max(logits, axis=-1, keepdims=True)
    i1 = jnp.min(jnp.where(logits == m1, lane, float(LANES)), axis=-1, keepdims=True)
    rest = jnp.where(lane == i1, -jnp.inf, logits)
    m2 = jnp.max(rest, axis=-1, keepdims=True)
    i2 = jnp.min(jnp.where(rest == m2, lane, float(LANES)), axis=-1, keepdims=True)
    e = jnp.exp(m2 - m1)
    g1 = 1.0 / (1.0 + e)
    idx_ref[...] = jnp.where(lane == 0.0, i1, jnp.where(lane == 1.0, i2, 0.0)).astype(I32)
    gate_ref[...] = jnp.where(lane == 0.0, g1, jnp.where(lane == 1.0, e * g1, 0.0))


def _router(x, w_pad, n_experts, tm, name):
    m, d = x.shape
    return pl.pallas_call(
        functools.partial(_router_body, n_experts=n_experts),
        grid=(m // tm,),
        in_specs=[pl.BlockSpec((tm, d), lambda i: (i, 0)), pl.BlockSpec((d, LANES), lambda i: (0, 0))],
        out_specs=[pl.BlockSpec((tm, LANES), lambda i: (i, 0))] * 2,
        out_shape=[jax.ShapeDtypeStruct((m, LANES), I32), jax.ShapeDtypeStruct((m, LANES), F32)],
        compiler_params=_params("parallel"),
        name=name,
    )(x, w_pad)


DMA_ISSUE_UNROLL = 8


def _row_copy(src_hbm, row, dst, r, sem):
    return pltpu.make_async_copy(src_hbm.at[pl.ds(row, 1)], dst.at[pl.ds(r, 1)], sem)


def _gather_body(src_ref, x_hbm, o_ref, buf, sem, *, rows):
    i = pl.program_id(0)

    def start(step):
        slot = step % 2

        def one(r, c):
            _row_copy(x_hbm, src_ref[step * rows + r], buf.at[slot], r, sem.at[slot]).start()
            return c

        lax.fori_loop(0, rows, one, 0, unroll=DMA_ISSUE_UNROLL)

    @pl.when(i == 0)
    def _():
        start(i)

    @pl.when(i + 1 < pl.num_programs(0))
    def _():
        start(i + 1)

    slot = i % 2
    pltpu.make_async_copy(x_hbm.at[pl.ds(0, rows)], buf.at[slot], sem.at[slot]).wait()
    o_ref[...] = buf[slot].astype(BF16)


def _gather_rows(src, x, rows, name):
    n = src.shape[0]
    d = x.shape[1]
    grid_spec = pltpu.PrefetchScalarGridSpec(
        num_scalar_prefetch=1,
        grid=(n // rows,),
        in_specs=[pl.BlockSpec(memory_space=pl.ANY)],
        out_specs=pl.BlockSpec((rows, d), lambda i, s: (i, 0)),
        scratch_shapes=[pltpu.VMEM((2, rows, d), x.dtype), pltpu.SemaphoreType.DMA((2,))],
    )
    return pl.pallas_call(
        functools.partial(_gather_body, rows=rows),
        grid_spec=grid_spec,
        out_shape=jax.ShapeDtypeStruct((n, d), BF16),
        compiler_params=_params("arbitrary"),
        name=name,
    )(src, x)


def _experts_body(te_ref, tb_ref, nu_ref, x_ref, w1_ref, w3_ref, w2_ref, o_ref, acc_ref):
    i = pl.program_id(0)
    f = pl.program_id(1)

    @pl.when(i < nu_ref[0])
    def _():
        @pl.when(f == 0)
        def _():
            acc_ref[...] = jnp.zeros_like(acc_ref)

        x = x_ref[...]
        hid = _silu(_dot(x, w1_ref[...].astype(BF16))) * _dot(x, w3_ref[...].astype(BF16))
        acc_ref[...] += _dot(hid.astype(BF16), w2_ref[...].astype(BF16))

        @pl.when(f == pl.num_programs(1) - 1)
        def _():
            o_ref[...] = acc_ref[...]

    @pl.when(jnp.logical_and(i >= nu_ref[0], f == pl.num_programs(1) - 1))
    def _():
        o_ref[...] = jnp.zeros_like(o_ref)


def _experts(tile_expert, tile_block, n_used, xs, w1, w3, w2, tm, tf, name):
    n_tiles = tile_expert.shape[0]
    d = xs.shape[1]
    ff = w1.shape[2]
    nf = ff // tf

    def fcol(i, f, nu):
        return jnp.where(i < nu[0], f, nf - 1)

    grid_spec = pltpu.PrefetchScalarGridSpec(
        num_scalar_prefetch=3,
        grid=(n_tiles, nf),
        in_specs=[pl.BlockSpec((tm, d), lambda i, f, te, tb, nu: (tb[i], 0)),
                  pl.BlockSpec((None, d, tf), lambda i, f, te, tb, nu: (te[i], 0, fcol(i, f, nu))),
                  pl.BlockSpec((None, d, tf), lambda i, f, te, tb, nu: (te[i], 0, fcol(i, f, nu))),
                  pl.BlockSpec((None, tf, d), lambda i, f, te, tb, nu: (te[i], fcol(i, f, nu), 0))],
        out_specs=pl.BlockSpec((tm, d), lambda i, f, te, tb, nu: (i, 0)),
        scratch_shapes=[pltpu.VMEM((tm, d), F32)],
    )
    return pl.pallas_call(
        _experts_body,
        grid_spec=grid_spec,
        out_shape=jax.ShapeDtypeStruct((n_tiles * tm, d), F32),
        compiler_params=_params("arbitrary", "arbitrary"),
        name=name,
    )(tile_expert, tile_block, n_used, xs, w1, w3, w2)


def _combine_ln_body(slot_ref, y_hbm, x_ref, gate_ref, g_ref, b_ref, o_ref, buf1, buf2, sem, *, rows, t0, alpha):
    base = (pl.program_id(0) * rows + t0) * TOP_K

    def start(r, c):
        _row_copy(y_hbm, slot_ref[base + TOP_K * r], buf1, r, sem).start()
        _row_copy(y_hbm, slot_ref[base + TOP_K * r + 1], buf2, r, sem).start()
        return c

    lax.fori_loop(0, rows, start, 0, unroll=DMA_ISSUE_UNROLL)
    pltpu.make_async_copy(y_hbm.at[pl.ds(0, rows)], buf1, sem).wait()
    pltpu.make_async_copy(y_hbm.at[pl.ds(0, rows)], buf2, sem).wait()
    gate = gate_ref[...]
    moe = gate[:, 0:1] * buf1[...] + gate[:, 1:2] * buf2[...]
    o_ref[...] = _layer_norm(alpha * x_ref[...] + moe, g_ref[...], b_ref[...])


def _combine_ln(slot, ys, x, gate, g, b, t0, alpha, rows, name):
    m, d = x.shape
    gb = t0 // rows
    grid_spec = pltpu.PrefetchScalarGridSpec(
        num_scalar_prefetch=1,
        grid=(m // rows,),
        in_specs=[pl.BlockSpec(memory_space=pl.ANY),
                  pl.BlockSpec((rows, d), lambda i, s: (i, 0)),
                  pl.BlockSpec((rows, LANES), lambda i, s: (i + gb, 0)),
                  pl.BlockSpec((1, d), lambda i, s: (0, 0)),
                  pl.BlockSpec((1, d), lambda i, s: (0, 0))],
        out_specs=pl.BlockSpec((rows, d), lambda i, s: (i, 0)),
        scratch_shapes=[pltpu.VMEM((rows, d), F32), pltpu.VMEM((rows, d), F32), pltpu.SemaphoreType.DMA(())],
    )
    return pl.pallas_call(
        functools.partial(_combine_ln_body, rows=rows, t0=t0, alpha=alpha),
        grid_spec=grid_spec,
        out_shape=jax.ShapeDtypeStruct((m, d), F32),
        compiler_params=_params("arbitrary"),
        name=name,
    )(slot, ys, x, gate, g, b)


def _moe_ln(xp, xs_, router_w, w1, w3, w2, g, b, alpha):
    n_experts = router_w.shape[1]
    d = xp.shape[1]
    tp, ts = xp.shape[0], xs_.shape[0]
    x_all = jnp.concatenate([xp, xs_], axis=0)
    t_all = tp + ts
    rw = jnp.pad(router_w, ((0, 0), (0, LANES - n_experts)))
    idx, gate = _router(x_all, rw, n_experts, ts, "moe_router")

    tm = 1024
    n_assign = t_all * TOP_K
    e_flat = idx[:, :TOP_K].reshape(n_assign)
    onehot = (e_flat[:, None] == jnp.arange(n_experts, dtype=I32)[None, :]).astype(I32)
    csum = jnp.cumsum(onehot, axis=0)
    pos = jnp.take_along_axis(csum, e_flat[:, None], axis=1)[:, 0] - 1
    counts = csum[-1]
    padded = ((counts + tm - 1) // tm) * tm
    ends = jnp.cumsum(padded)
    offs = ends - padded
    slot = (offs[e_flat] + pos).astype(I32)
    n_tiles = (n_assign + n_experts * (tm - 1)) // tm
    src = jnp.zeros((n_tiles * tm,), I32).at[slot].set(jnp.arange(n_assign, dtype=I32) // TOP_K)
    n_used = (ends[-1] // tm).astype(I32)
    tiles = jnp.arange(n_tiles, dtype=I32)
    t_exp = jnp.minimum(jnp.sum((tiles[:, None] * tm >= ends[None, :]).astype(I32), axis=1), n_experts - 1)
    last = jnp.maximum(n_used - 1, 0)
    tile_expert = jnp.where(tiles < n_used, t_exp, t_exp[last])
    tile_block = jnp.where(tiles < n_used, tiles, last)

    x_sorted = _gather_rows(src, x_all, 256, "moe_gather")
    y_sorted = _experts(tile_expert, tile_block, n_used.reshape(1), x_sorted, w1, w3, w2, tm, 512, "moe_experts")
    yp = _combine_ln(slot, y_sorted, xp, gate, g, b, 0, alpha, 256, "moe_combine_prompt")
    ys = _combine_ln(slot, y_sorted, xs_, gate, g, b, tp, alpha, 256, "moe_combine_sample")
    return yp, ys


def _rope_tables(pos):
    half = ROT_DIM // 2
    inv = ROPE_THETA ** (-jnp.arange(0, ROT_DIM, 2, dtype=F32) / ROT_DIM)
    ang = pos.astype(F32)[:, None] * inv[None, :]
    cos, sin = jnp.cos(ang), jnp.sin(ang)
    n = pos.shape[0]
    ones = jnp.ones((n, HEAD_DIM - ROT_DIM), F32)
    zeros = jnp.zeros((n, HEAD_DIM - ROT_DIM), F32)
    zh = jnp.zeros((n, half), F32)
    reps = LANES // HEAD_DIM
    c = jnp.tile(jnp.concatenate([cos, cos, ones], axis=1), (1, reps))
    sa = jnp.tile(jnp.concatenate([-sin, zh, zeros], axis=1), (1, reps))
    sb = jnp.tile(jnp.concatenate([zh, sin, zeros], axis=1), (1, reps))
    return c, sa, sb


def _block_diag_q(q, scale):
    n_seq, n_new, nh, d = q.shape
    eye = jnp.eye(nh, dtype=F32)
    out = jnp.einsum("sihd,hg->shigd", q * scale, eye)
    return out.reshape(n_seq, nh * n_new, nh * d).astype(BF16)


def _diag_blocks(o, n_new, nh, d):
    n_seq = o.shape[0]
    o5 = o.reshape(n_seq, nh, n_new, nh, d)
    sel = jnp.stack([o5[:, h, :, h, :] for h in range(nh)], axis=2)
    return sel.reshape(n_seq * n_new, nh * d)


def kernel(x_prompt, x_sample, cache_sb_k, cache_sb_v, page_table, state_ssm, state_conv, cache_c_w128_k, cache_c_w128_v, cache_c_w512_k, cache_c_w512_v, cache_c_w2048_k, cache_c_w2048_v, w_in_a, conv_w, conv_b, dt_bias, a_log, d_skip, ssm_norm_w, sb_bias, w_out_a, ln_a_mix_g, ln_a_mix_b, ffn_w1, ffn_w3, ffn_w2, ln_a_ffn_g, ln_a_ffn_b, w_in_c, w_out_c, ln_c_mix_g, ln_c_mix_b, router_w, moe_w1, moe_w3, moe_w2, ln_c_ffn_g, ln_c_ffn_b):
    n_pr, seq, d_model = x_prompt.shape
    n_dec, dec_seq, _ = x_sample.shape
    n_pages = page_table.shape[1]
    page = cache_sb_k.shape[2]
    past_len = n_pages * page
    ssm_heads, p, n_state = state_ssm.shape[2], state_ssm.shape[3], state_ssm.shape[4]
    d_inner = ssm_heads * p
    conv_dim = state_conv.shape[3]
    sb_heads = cache_sb_k.shape[3]
    sb_w = sb_heads * HEAD_DIM
    c_slots = cache_c_w128_k.shape[3]
    c_wd = c_slots * HEAD_DIM
    n_groups = len(C_GROUPS)
    depth = w_in_a.shape[0] + w_in_c.shape[0]
    alpha = (2 * depth) ** 0.25
    tp = n_pr * seq
    ts = n_dec * dec_seq
    win_k = (cache_c_w128_k, cache_c_w512_k, cache_c_w2048_k)
    win_v = (cache_c_w128_v, cache_c_w512_v, cache_c_w2048_v)
    row = lambda v: v.reshape(1, -1)

    yp = x_prompt.reshape(tp, d_model)
    ys = x_sample.reshape(ts, d_model)
    outs = {}

    i = 0
    w = w_in_a[i]
    c0, c1, c2 = d_inner, d_inner + conv_dim, d_inner + conv_dim + ssm_heads
    w_re = jnp.concatenate([w[:, c0:c1], w[:, :c0], w[:, c2:], w[:, c1:c2],
                            jnp.zeros((d_model, LANES - ssm_heads), F32)], axis=1).astype(BF16)
    o_q = conv_dim + d_inner
    proj_p = _matmul(yp, w_re, 512, "in_proj_a_prompt")
    proj_s = _matmul(ys, w_re, ts, "in_proj_a_sample")

    pad_l = LANES - ssm_heads
    ssd_par = (conv_w[i], row(conv_b[i]), row(jnp.pad(dt_bias[i], (0, pad_l))),
               row(jnp.pad(-jnp.exp(a_log[i]), (0, pad_l))), row(jnp.repeat(d_skip[i], p)), row(ssm_norm_w[i]))
    ya_p, ssm_p = _ssd(proj_p.reshape(n_pr, seq, -1), jnp.zeros((n_pr, CONV_W - 1, conv_dim), F32),
                       jnp.zeros((n_pr, ssm_heads, p, n_state), F32), *ssd_par, SSD_CHUNK, "ssd_prompt")
    proj_s3 = proj_s.reshape(n_dec, dec_seq, -1)
    proj_s_pad = jnp.pad(proj_s3, ((0, 0), (0, SSD_CHUNK - dec_seq), (0, 0)))
    ya_s, ssm_s = _ssd(proj_s_pad, state_conv[i], state_ssm[i], *ssd_par, dec_seq, "ssd_sample")
    ya_p = ya_p.reshape(tp, d_inner)
    ya_s = ya_s[:, :dec_seq].reshape(ts, d_inner)
    conv_p = proj_p.reshape(n_pr, seq, -1)[:, seq - (CONV_W - 1):, :conv_dim]
    conv_s = jnp.concatenate([state_conv[i], proj_s3[:, :, :conv_dim]], axis=1)[:, dec_seq:]

    k_p = proj_p[:, o_q + sb_w:o_q + 2 * sb_w].reshape(n_pr, seq, sb_heads, HEAD_DIM)
    v_p = proj_p[:, o_q + 2 * sb_w:o_q + 3 * sb_w].reshape(n_pr, seq, sb_heads, HEAD_DIM)
    o_p = _sb_prompt(sb_bias[i], proj_p.reshape(n_pr, seq, -1), o_q, o_q + sb_w, o_q + 2 * sb_w,
                     sb_heads, HEAD_DIM, 256, "sb_prompt").reshape(tp, sb_w)

    q_s = proj_s[:, o_q:o_q + sb_w].reshape(n_dec, dec_seq, sb_heads, HEAD_DIM)
    k_s = proj_s[:, o_q + sb_w:o_q + 2 * sb_w].reshape(n_dec, dec_seq, sb_w)
    v_s = proj_s[:, o_q + 2 * sb_w:o_q + 3 * sb_w].reshape(n_dec, dec_seq, sb_w)
    qbd = _block_diag_q(q_s, HEAD_DIM ** -0.5)
    bias_rows = jnp.repeat(sb_bias[i], dec_seq).reshape(sb_heads * dec_seq, 1)

    def keys_minor(a):
        return a.transpose(0, 2, 3, 1)

    pad_new = ((0, 0), (0, 0), (0, 0), (0, page - dec_seq))
    o_s = _sb_sample(page_table, qbd, bias_rows,
                     jnp.pad(keys_minor(k_s.reshape(n_dec, dec_seq, sb_heads, HEAD_DIM)), pad_new),
                     jnp.pad(keys_minor(v_s.reshape(n_dec, dec_seq, sb_heads, HEAD_DIM)), pad_new),
                     keys_minor(cache_sb_k[i]), keys_minor(cache_sb_v[i]), 16, dec_seq, "sb_sample")
    o_s = _diag_blocks(o_s, dec_seq, sb_heads, HEAD_DIM)

    wo = w_out_a[i].astype(BF16)
    ln_g, ln_b = row(ln_a_mix_g[i]), row(ln_a_mix_b[i])
    wo2 = [wo[:d_inner], wo[d_inner:]]
    yp = _proj_ln([ya_p, o_p], wo2, yp, ln_g, ln_b, alpha, 512, "out_proj_a_prompt")
    ys = _proj_ln([ya_s, o_s], wo2, ys, ln_g, ln_b, alpha, ts, "out_proj_a_sample")
    f1, f3, f2 = ffn_w1[i].astype(BF16), ffn_w3[i].astype(BF16), ffn_w2[i].astype(BF16)
    ln_g, ln_b = row(ln_a_ffn_g[i]), row(ln_a_ffn_b[i])
    d_ff = f1.shape[1]
    tf = 2 * LANES if d_ff % (2 * LANES) == 0 else LANES
    yp = _ffn_ln(yp, f1, f3, f2, ln_g, ln_b, alpha, 1024, tf, "ffn_prompt")
    ys = _ffn_ln(ys, f1, f3, f2, ln_g, ln_b, alpha, ts, tf, "ffn_sample")

    outs["sb_kp"] = k_p[None]
    outs["sb_vp"] = v_p[None]
    outs["sb_ks"] = k_s.reshape(1, n_dec, dec_seq, sb_heads, HEAD_DIM)
    outs["sb_vs"] = v_s.reshape(1, n_dec, dec_seq, sb_heads, HEAD_DIM)
    outs["ssm_p"], outs["ssm_s"], outs["conv_p"], outs["conv_s"] = ssm_p[None], ssm_s[None], conv_p[None], conv_s[None]

    wc = w_in_c[i].astype(BF16)
    gw = n_groups * c_wd
    cos_p, sa_p, sb_p = _rope_tables(jnp.arange(seq))
    cos_s, sa_s, sb_s = _rope_tables(past_len + jnp.arange(ts) % dec_seq)
    qkv_p = _matmul_rope(yp, wc, cos_p, sa_p, sb_p, 512, "in_proj_c_prompt")
    qkv_s = _matmul_rope(ys, wc, cos_s, sa_s, sb_s, ts, "in_proj_c_sample")

    scale = HEAD_DIM ** -0.5
    o_c_p = _dil_prompt(qkv_p.reshape(n_pr, seq, -1), [dil for _, dil in C_GROUPS], c_slots, HEAD_DIM,
                        "dilated_prompt").reshape(tp, c_wd)
    o_groups_s, l_groups_s = [], []
    for g, (win, dil) in enumerate(C_GROUPS):
        cq, ck, cv = g * c_wd, gw + g * c_wd, 2 * gw + g * c_wd
        kg_p = qkv_p[:, ck:ck + c_wd]
        vg_p = qkv_p[:, cv:cv + c_wd]
        keep = min(win, seq)
        outs[f"c_kp{g}"] = kg_p.reshape(n_pr, seq, c_slots, HEAD_DIM)[:, seq - keep:][None]
        outs[f"c_vp{g}"] = vg_p.reshape(n_pr, seq, c_slots, HEAD_DIM)[:, seq - keep:][None]

        qg_s = qkv_s[:, cq:cq + c_wd].reshape(n_dec, dec_seq, c_slots, HEAD_DIM)
        pad_keys = ((0, 0), (0, 0), (0, DIL_BLOCK - dec_seq))
        kn = jnp.pad(qkv_s[:, ck:ck + c_wd].reshape(n_dec, dec_seq, c_wd).transpose(0, 2, 1), pad_keys)
        vn = jnp.pad(qkv_s[:, cv:cv + c_wd].reshape(n_dec, dec_seq, c_wd).transpose(0, 2, 1), pad_keys)
        o_g, l_g, k_new, v_new = _dil_sample(_block_diag_q(qg_s, scale), keys_minor(win_k[g][i]),
                                             keys_minor(win_v[g][i]), kn, vn, dil, dec_seq,
                                             f"dilated_sample_{g}")
        o_groups_s.append(_diag_blocks(o_g, dec_seq, c_slots, HEAD_DIM))
        l_s = l_g[:, :, 0].reshape(n_dec, c_slots, dec_seq).transpose(0, 2, 1)
        l_groups_s.append(jnp.repeat(l_s.reshape(ts, c_slots), HEAD_DIM, axis=1))
        outs[f"c_ks{g}"] = k_new.transpose(0, 3, 1, 2)[None]
        outs[f"c_vs{g}"] = v_new.transpose(0, 3, 1, 2)[None]

    woc = w_out_c[i].astype(BF16)
    ln_g, ln_b = row(ln_c_mix_g[i]), row(ln_c_mix_b[i])
    yp = _proj_ln([o_c_p], [woc], yp, ln_g, ln_b, alpha, 512, "out_proj_c_prompt")
    ys = _merge_proj_ln(o_groups_s, l_groups_s, woc, ys, ln_g, ln_b, alpha, ts, "out_proj_c_sample")
    yp, ys = _moe_ln(yp, ys, router_w[i], moe_w1[i], moe_w3[i], moe_w2[i],
                     row(ln_c_ffn_g[i]), row(ln_c_ffn_b[i]), alpha)

    res = [yp.reshape(n_pr, seq, d_model), ys.reshape(n_dec, dec_seq, d_model),
           outs["sb_kp"], outs["sb_vp"], outs["sb_ks"], outs["sb_vs"],
           outs["ssm_p"], outs["ssm_s"], outs["conv_p"], outs["conv_s"]]
    for g in range(n_groups):
        res += [outs[f"c_kp{g}"], outs[f"c_vp{g}"], outs[f"c_ks{g}"], outs[f"c_vs{g}"]]
    return tuple(res)
```

```python
import functools
import math

import numpy as np
import jax
import jax.numpy as jnp
from jax import lax
from jax.experimental import pallas as pl
from jax.experimental.pallas import tpu as pltpu

F32 = jnp.float32
BF16 = jnp.bfloat16
I32 = jnp.int32

HEAD_DIM = 64
SSM_HEAD_DIM = 64
SSM_GROUPS = 2
SSM_STATE = 128
CONV_W = 4
SSD_CHUNK = 128
C_GROUPS = ((128, 1), (512, 4), (2048, 16))
ROT_DIM = HEAD_DIM // 4
ROPE_THETA = 500000.0
TOP_K = 2
LN_EPS = 1e-5
RMS_EPS = 1e-6

LANES = 128
SUBLANES = 8
VMEM_LIMIT_BYTES = 56 * 1024 * 1024


def _params(*semantics):
    return pltpu.CompilerParams(dimension_semantics=semantics, vmem_limit_bytes=VMEM_LIMIT_BYTES)


def _dot(a, b):
    return jnp.dot(a, b, preferred_element_type=F32)


def _dot_nt(a, b):
    return lax.dot_general(a, b, (((1,), (1,)), ((), ())), preferred_element_type=F32)


def _layer_norm(y, g, b):
    mu = jnp.mean(y, axis=-1, keepdims=True)
    d = y - mu
    var = jnp.mean(d * d, axis=-1, keepdims=True)
    return d * lax.rsqrt(var + LN_EPS) * g + b


def _silu(a):
    return a * jax.nn.sigmoid(a)


def _split3(a):
    hi = a.astype(BF16)
    r = a - hi.astype(F32)
    mid = r.astype(BF16)
    lo = (r - mid.astype(F32)).astype(BF16)
    return hi, mid, lo


def _mm_body(x_ref, w_ref, o_ref):
    o_ref[...] = _dot(x_ref[...].astype(BF16), w_ref[...])


def _matmul(x, w, tm, name):
    m, k = x.shape
    n = w.shape[1]
    return pl.pallas_call(
        _mm_body,
        grid=(m // tm,),
        in_specs=[pl.BlockSpec((tm, k), lambda i: (i, 0)), pl.BlockSpec((k, n), lambda i: (0, 0))],
        out_specs=pl.BlockSpec((tm, n), lambda i: (i, 0)),
        out_shape=jax.ShapeDtypeStruct((m, n), F32),
        compiler_params=_params("parallel"),
        name=name,
    )(x, w)


MXU_WIDTH = 2 * LANES


def _mm_rope_body(x_ref, w_ref, cos_ref, sa_ref, sb_ref, o_ref, xb_ref, *, n_rot_tiles):
    j = pl.program_id(1)

    @pl.when(j == 0)
    def _():
        xb_ref[...] = x_ref[...].astype(BF16)

    xb = xb_ref[...]

    @pl.when(j < n_rot_tiles)
    def _():
        cos = cos_ref[...]
        sa = sa_ref[...]
        sb = sb_ref[...]
        for c in range(o_ref.shape[1] // MXU_WIDTH):
            y = _dot(xb, w_ref[:, c * MXU_WIDTH:(c + 1) * MXU_WIDTH])
            for h in range(MXU_WIDTH // LANES):
                yc = y[:, h * LANES:(h + 1) * LANES]
                lo = c * MXU_WIDTH + h * LANES
                o_ref[:, lo:lo + LANES] = (
                    yc * cos + pltpu.roll(yc, LANES - ROT_DIM // 2, 1) * sa + pltpu.roll(yc, ROT_DIM // 2, 1) * sb)

    @pl.when(j >= n_rot_tiles)
    def _():
        o_ref[...] = _dot(xb, w_ref[...])


def _matmul_rope(x, w, cos, sa, sb, tm, name):
    m, k = x.shape
    n = w.shape[1]
    tn = n // 3
    nb = cos.shape[0] // tm
    return pl.pallas_call(
        functools.partial(_mm_rope_body, n_rot_tiles=2),
        grid=(m // tm, 3),
        in_specs=[pl.BlockSpec((tm, k), lambda i, j: (i, 0)),
                  pl.BlockSpec((k, tn), lambda i, j: (0, j)),
                  pl.BlockSpec((tm, LANES), lambda i, j: (i % nb, 0)),
                  pl.BlockSpec((tm, LANES), lambda i, j: (i % nb, 0)),
                  pl.BlockSpec((tm, LANES), lambda i, j: (i % nb, 0))],
        out_specs=pl.BlockSpec((tm, tn), lambda i, j: (i, j)),
        out_shape=jax.ShapeDtypeStruct((m, n), F32),
        scratch_shapes=[pltpu.VMEM((tm, k), BF16)],
        compiler_params=_params("parallel", "arbitrary"),
        name=name,
    )(x, w, cos, sa, sb)


def _proj_ln_body(*refs, alpha, n_in):
    h_refs, w_refs = refs[:n_in], refs[n_in:2 * n_in]
    r_ref, g_ref, b_ref, o_ref = refs[2 * n_in:]
    m = _dot(h_refs[0][...].astype(BF16), w_refs[0][...])
    for h_ref, w_ref in zip(h_refs[1:], w_refs[1:]):
        m = m + _dot(h_ref[...].astype(BF16), w_ref[...])
    o_ref[...] = _layer_norm(alpha * r_ref[...] + m, g_ref[...], b_ref[...])


def _proj_ln(hs, ws, resid, g, b, alpha, tm, name):
    m, d = resid.shape
    row = lambda i: (i, 0)
    fix = lambda i: (0, 0)
    return pl.pallas_call(
        functools.partial(_proj_ln_body, alpha=alpha, n_in=len(hs)),
        grid=(m // tm,),
        in_specs=[pl.BlockSpec((tm, h.shape[1]), row) for h in hs]
                 + [pl.BlockSpec((w.shape[0], d), fix) for w in ws]
                 + [pl.BlockSpec((tm, d), row), pl.BlockSpec((1, d), fix), pl.BlockSpec((1, d), fix)],
        out_specs=pl.BlockSpec((tm, d), row),
        out_shape=jax.ShapeDtypeStruct((m, d), F32),
        compiler_params=_params("parallel"),
        name=name,
    )(*hs, *ws, resid, g, b)


def _merge_proj_ln_body(o0_ref, o1_ref, o2_ref, l0_ref, l1_ref, l2_ref, w_ref, r_ref, g_ref, b_ref, o_ref, *, alpha):
    l0, l1, l2 = l0_ref[...], l1_ref[...], l2_ref[...]
    mx = jnp.maximum(jnp.maximum(l0, l1), l2)
    e0, e1, e2 = jnp.exp(l0 - mx), jnp.exp(l1 - mx), jnp.exp(l2 - mx)
    o = (e0 * o0_ref[...] + e1 * o1_ref[...] + e2 * o2_ref[...]) / (e0 + e1 + e2)
    m = _dot(o.astype(BF16), w_ref[...])
    o_ref[...] = _layer_norm(alpha * r_ref[...] + m, g_ref[...], b_ref[...])


def _merge_proj_ln(os_, ls_, w, resid, g, b, alpha, tm, name):
    m, d = resid.shape
    k = w.shape[0]
    row = lambda i: (i, 0)
    fix = lambda i: (0, 0)
    return pl.pallas_call(
        functools.partial(_merge_proj_ln_body, alpha=alpha),
        grid=(m // tm,),
        in_specs=[pl.BlockSpec((tm, k), row)] * 6 + [pl.BlockSpec((k, d), fix), pl.BlockSpec((tm, d), row),
                                                     pl.BlockSpec((1, d), fix), pl.BlockSpec((1, d), fix)],
        out_specs=pl.BlockSpec((tm, d), row),
        out_shape=jax.ShapeDtypeStruct((m, d), F32),
        compiler_params=_params("parallel"),
        name=name,
    )(*os_, *ls_, w, resid, g, b)


def _ffn_ln_body(x_ref, w1_ref, w3_ref, w2_ref, g_ref, b_ref, o_ref, xb_ref, acc_ref, *, alpha):
    f = pl.program_id(1)

    @pl.when(f == 0)
    def _():
        xb_ref[...] = x_ref[...].astype(BF16)
        acc_ref[...] = jnp.zeros_like(acc_ref)

    xb = xb_ref[...]
    hid = _silu(_dot(xb, w1_ref[...])) * _dot(xb, w3_ref[...])
    acc_ref[...] += _dot(hid.astype(BF16), w2_ref[...])

    @pl.when(f == pl.num_programs(1) - 1)
    def _():
        o_ref[...] = _layer_norm(alpha * x_ref[...] + acc_ref[...], g_ref[...], b_ref[...])


def _ffn_ln(x, w1, w3, w2, g, b, alpha, tm, tf, name):
    m, d = x.shape
    ff = w1.shape[1]
    return pl.pallas_call(
        functools.partial(_ffn_ln_body, alpha=alpha),
        grid=(m // tm, ff // tf),
        in_specs=[pl.BlockSpec((tm, d), lambda i, f: (i, 0)),
                  pl.BlockSpec((d, tf), lambda i, f: (0, f)),
                  pl.BlockSpec((d, tf), lambda i, f: (0, f)),
                  pl.BlockSpec((tf, d), lambda i, f: (f, 0)),
                  pl.BlockSpec((1, d), lambda i, f: (0, 0)),
                  pl.BlockSpec((1, d), lambda i, f: (0, 0))],
        out_specs=pl.BlockSpec((tm, d), lambda i, f: (i, 0)),
        out_shape=jax.ShapeDtypeStruct((m, d), F32),
        scratch_shapes=[pltpu.VMEM((tm, d), BF16), pltpu.VMEM((tm, d), F32)],
        compiler_params=_params("parallel", "arbitrary"),
        name=name,
    )(x, w1, w3, w2, g, b)


def _ssd_body(xbc_ref, z_ref, dt_ref, conv0_ref, h0_ref, cw_ref, cb_ref, dtb_ref, aneg_ref, dsk_ref, nw_ref,
              y_ref, hn_ref, xpad_ref, *, q, valid_len, n_heads, d_inner):
    c = pl.program_id(1)
    conv_dim = xpad_ref.shape[1]
    gn = SSM_GROUPS * SSM_STATE
    heads_per_group = n_heads // SSM_GROUPS
    p = SSM_HEAD_DIM

    @pl.when(c == 0)
    def _():
        xpad_ref[0:SUBLANES, :] = jnp.zeros((SUBLANES, conv_dim), F32)
        xpad_ref[SUBLANES - (CONV_W - 1):SUBLANES, :] = conv0_ref[...]
        hn_ref[...] = h0_ref[...]

    x_raw = xbc_ref[...]
    xpad_ref[SUBLANES:SUBLANES + q, :] = x_raw
    cw = cw_ref[...]
    acc = cb_ref[...] + x_raw * cw[CONV_W - 1:CONV_W]
    for j in range(CONV_W - 1):
        off = SUBLANES - (CONV_W - 1) + j
        acc = acc + xpad_ref[off:off + q, :] * cw[j:j + 1]
    xpad_ref[SUBLANES - (CONV_W - 1):SUBLANES, :] = x_raw[q - (CONV_W - 1):q, :]
    xc = _silu(acc)
    xs = xc[:, :d_inner]
    bm = xc[:, d_inner:d_inner + gn].astype(BF16)
    cm = xc[:, d_inner + gn:d_inner + 2 * gn].astype(BF16)

    t = dt_ref[...] + dtb_ref[...]
    dtv = jnp.maximum(t, 0.0) + jnp.log(1.0 + jnp.exp(-jnp.abs(t)))
    rows = lax.broadcasted_iota(I32, (q, q), 0)
    cols = lax.broadcasted_iota(I32, (q, q), 1)
    if valid_len < q:
        dtv = jnp.where(lax.broadcasted_iota(I32, dtv.shape, 0) < valid_len, dtv, 0.0)
    a = dtv * aneg_ref[...]
    causal = cols <= rows
    tri = jnp.where(causal, 1.0, 0.0).astype(BF16)
    a_hi, a_mid, a_lo = _split3(a)
    a_cum = _dot(tri, a_hi) + _dot(tri, a_mid) + _dot(tri, a_lo)
    a_cum_t = a_cum.T
    a_tot = a_cum[q - 1:q, :]

    ys = []
    xws = []
    for g in range(SSM_GROUPS):
        cg = cm[:, g * SSM_STATE:(g + 1) * SSM_STATE]
        bg = bm[:, g * SSM_STATE:(g + 1) * SSM_STATE]
        scores = _dot_nt(cg, bg)
        for r in range(heads_per_group):
            h = g * heads_per_group + r
            col = a_cum[:, h:h + 1]
            row = a_cum_t[h:h + 1, :]
            decay_ls = jnp.where(causal, jnp.exp(col - row), 0.0)
            w_diag = (scores * decay_ls).astype(BF16)
            xs_h = xs[:, h * p:(h + 1) * p]
            xdt = xs_h * dtv[:, h:h + 1]
            y_diag = _dot(w_diag, xdt.astype(BF16))
            y_off = _dot_nt(cg, hn_ref[h].astype(BF16)) * jnp.exp(col)
            ys.append(y_diag + y_off)
            xws.append(xdt * jnp.exp(a_tot[:, h:h + 1] - col))
    y = jnp.concatenate(ys, axis=1) + dsk_ref[...] * xs
    xw = jnp.concatenate(xws, axis=1)

    heads_per_tile = LANES // p
    for j in range(d_inner // LANES):
        xw_t = xw[:, j * LANES:(j + 1) * LANES].T.astype(BF16)
        for hh in range(heads_per_tile):
            h = j * heads_per_tile + hh
            g = h // heads_per_group
            bg = bm[:, g * SSM_STATE:(g + 1) * SSM_STATE]
            st = _dot(xw_t[hh * p:(hh + 1) * p, :], bg)
            hn_ref[h] = jnp.exp(a_tot[:, h:h + 1]) * hn_ref[h] + st

    zz = z_ref[...]
    yg = y * _silu(zz)
    gsz = d_inner // SSM_GROUPS
    outs = []
    for g in range(SSM_GROUPS):
        blk = yg[:, g * gsz:(g + 1) * gsz]
        ms = jnp.mean(blk * blk, axis=-1, keepdims=True)
        outs.append(blk * lax.rsqrt(ms + RMS_EPS))
    y_ref[...] = jnp.concatenate(outs, axis=1) * nw_ref[...]


def _ssd(proj3, conv0, h0, cw, cb, dtb, aneg, dsk, nw, valid_len, name):
    b, length, _ = proj3.shape
    n_heads, p, n = h0.shape[1], h0.shape[2], h0.shape[3]
    d_inner = n_heads * p
    conv_dim = cw.shape[1]
    q = SSD_CHUNK
    z_blk = conv_dim // d_inner
    dt_blk = (proj3.shape[2] - LANES) // LANES
    fix2 = lambda i, c: (0, 0)
    return pl.pallas_call(
        functools.partial(_ssd_body, q=q, valid_len=valid_len, n_heads=n_heads, d_inner=d_inner),
        grid=(b, length // q),
        in_specs=[pl.BlockSpec((None, q, conv_dim), lambda i, c: (i, c, 0)),
                  pl.BlockSpec((None, q, d_inner), lambda i, c: (i, c, z_blk)),
                  pl.BlockSpec((None, q, LANES), lambda i, c: (i, c, dt_blk)),
                  pl.BlockSpec((None, CONV_W - 1, conv_dim), lambda i, c: (i, 0, 0)),
                  pl.BlockSpec((None, n_heads, p, n), lambda i, c: (i, 0, 0, 0)),
                  pl.BlockSpec((CONV_W, conv_dim), fix2),
                  pl.BlockSpec((1, conv_dim), fix2),
                  pl.BlockSpec((1, LANES), fix2),
                  pl.BlockSpec((1, LANES), fix2),
                  pl.BlockSpec((1, d_inner), fix2),
                  pl.BlockSpec((1, d_inner), fix2)],
        out_specs=[pl.BlockSpec((None, q, d_inner), lambda i, c: (i, c, 0)),
                   pl.BlockSpec((None, n_heads, p, n), lambda i, c: (i, 0, 0, 0))],
        out_shape=[jax.ShapeDtypeStruct((b, length, d_inner), F32),
                   jax.ShapeDtypeStruct((b, n_heads, p, n), F32)],
        scratch_shapes=[pltpu.VMEM((SUBLANES + q, conv_dim), F32)],
        compiler_params=_params("parallel", "arbitrary"),
        name=name,
    )(proj3, proj3, proj3, conv0, h0, cw, cb, dtb, aneg, dsk, nw)


SB_SEGMENTS = SUBLANES


def _sb_weights(z, carry, valid, f_ref):
    tk, tq = z.shape
    seglen = tk // SB_SEGMENTS
    fail = 1.0 / (1.0 + jnp.exp(z))
    if valid is not None:
        fail = jnp.where(valid, fail, 1.0)
    f_ref[...] = fail
    seg_tot = jnp.ones((SB_SEGMENTS, tq), F32)
    for v in range(seglen):
        seg_tot = seg_tot * f_ref[v * SB_SEGMENTS:(v + 1) * SB_SEGMENTS, :]
    offs = [None] * SB_SEGMENTS
    tot = carry
    for s in reversed(range(SB_SEGMENTS)):
        offs[s] = tot
        tot = tot * seg_tot[s:s + 1, :]
    run = jnp.concatenate(offs, axis=0)
    rows = [None] * seglen
    for v in reversed(range(seglen)):
        nxt = run * f_ref[v * SB_SEGMENTS:(v + 1) * SB_SEGMENTS, :]
        rows[v] = run - nxt
        run = nxt
    return jnp.concatenate(rows, axis=0), tot


def _sb_prompt_body(bias_ref, q_ref, k_ref, v_ref, o_ref, kp_ref, vt_ref, f_ref, z_ref, w_ref, acc_ref,
                    *, tq, hb, d):
    i = pl.program_id(2)
    h0 = pl.program_id(1) * hb
    seglen = tq // SB_SEGMENTS
    nkb = kp_ref.shape[0]
    r = lax.broadcasted_iota(I32, (tq, tq), 0)
    key_pos = (r % SB_SEGMENTS) * seglen + r // SB_SEGMENTS
    strictly_before = key_pos < lax.broadcasted_iota(I32, (tq, tq), 1)

    @pl.when(i == 0)
    def _():
        def permute(kb, c):
            base = pl.multiple_of(kb * tq, tq)
            kp = [k_ref[pl.ds(base + v, SB_SEGMENTS, stride=seglen), :] for v in range(seglen)]
            vp = [v_ref[pl.ds(base + v, SB_SEGMENTS, stride=seglen), :] for v in range(seglen)]
            kp_ref[kb] = jnp.concatenate(kp, axis=0).astype(BF16)
            vt_ref[kb] = jnp.concatenate(vp, axis=0).T.astype(BF16)
            return c

        lax.fori_loop(0, nkb, permute, 0)

    qt_all = (q_ref[...] * (d ** -0.5)).T
    head_of_row = lax.broadcasted_iota(I32, qt_all.shape, 0) // d
    qts = [jnp.where(head_of_row == hh, qt_all, 0.0).astype(BF16) for hh in range(hb)]

    def logits(hh, kb):
        return _dot(kp_ref[kb], qts[hh]) + bias_ref[h0 + hh]

    def values(hh, kb):
        return vt_ref[kb, hh * d:(hh + 1) * d, :]

    carries = []
    for hh in range(hb):
        w, carry = _sb_weights(logits(hh, i), jnp.ones((1, tq), F32), strictly_before, f_ref.at[hh])
        w_ref[hh] = w.astype(BF16)
        z_ref[hh] = logits(hh, jnp.maximum(i - 1, 0))
        acc_ref[hh] = jnp.zeros((d, tq), F32)
        carries.append(carry)

    def step(it, carries):
        kb = i - 1 - it
        out = []
        for hh in range(hb):
            acc_ref[hh] += _dot(values(hh, kb + 1), w_ref[hh])
            z = z_ref[hh]
            z_ref[hh] = logits(hh, jnp.maximum(kb - 1, 0))
            w, carry = _sb_weights(z, carries[hh], None, f_ref.at[hh])
            w_ref[hh] = w.astype(BF16)
            out.append(carry)
        return tuple(out)

    lax.fori_loop(0, i, step, tuple(carries))
    out_t = [acc_ref[hh] + _dot(values(hh, 0), w_ref[hh]) for hh in range(hb)]
    o_ref[...] = jnp.concatenate(out_t, axis=0).T


def _sb_prompt(bias, proj3, q_col, k_col, v_col, n_heads, d, tq, name):
    b, length, _ = proj3.shape
    hb = LANES // d
    nkb = length // tq
    qc, kc, vc = q_col // LANES, k_col // LANES, v_col // LANES
    return pl.pallas_call(
        functools.partial(_sb_prompt_body, tq=tq, hb=hb, d=d),
        grid=(b, n_heads // hb, nkb),
        in_specs=[pl.BlockSpec(memory_space=pltpu.SMEM),
                  pl.BlockSpec((None, tq, LANES), lambda bi, hi, i: (bi, i, qc + hi)),
                  pl.BlockSpec((None, length, LANES), lambda bi, hi, i: (bi, 0, kc + hi)),
                  pl.BlockSpec((None, length, LANES), lambda bi, hi, i: (bi, 0, vc + hi))],
        out_specs=pl.BlockSpec((None, tq, LANES), lambda bi, hi, i: (bi, i, hi)),
        out_shape=jax.ShapeDtypeStruct((b, length, n_heads * d), F32),
        scratch_shapes=[pltpu.VMEM((nkb, tq, LANES), BF16), pltpu.VMEM((nkb, LANES, tq), BF16),
                        pltpu.VMEM((hb, tq, tq), F32), pltpu.VMEM((hb, tq, tq), F32),
                        pltpu.VMEM((hb, tq, tq), BF16), pltpu.VMEM((hb, d, tq), F32)],
        compiler_params=_params("parallel", "parallel", "arbitrary"),
        name=name,
    )(bias, proj3, proj3, proj3)


def _sb_sample_body(pt_ref, qbd_ref, bias_ref, knew_ref, vnew_ref, *refs, pages_per_step, n_new):
    k_refs = refs[:pages_per_step]
    v_refs = refs[pages_per_step:2 * pages_per_step]
    o_ref = refs[2 * pages_per_step]
    carry_ref = refs[2 * pages_per_step + 1]
    j = pl.program_id(1)
    qbd = qbd_ref[...]
    bias = bias_ref[...]
    nr, hd = qbd.shape
    tk = knew_ref.shape[-1]
    kk = lax.broadcasted_iota(I32, (tk, tk), 0)
    ss = lax.broadcasted_iota(I32, (tk, tk), 1)
    later = jnp.where(kk > ss, 1.0, 0.0).astype(BF16)

    def blocks(kts, vts, valid):
        n = len(kts)
        kcat = jnp.concatenate([kt.reshape(hd, tk).astype(BF16) for kt in kts], axis=1)
        vcat = jnp.concatenate([vt.reshape(hd, tk).astype(BF16) for vt in vts], axis=1)
        z = _dot(qbd, kcat) + bias
        sp = jnp.maximum(z, 0.0) + jnp.log(1.0 + jnp.exp(-jnp.abs(z)))
        log_fail = -sp
        log_beta = z - sp
        if valid is not None:
            log_fail = jnp.where(valid, log_fail, 0.0)
        lf_hi = log_fail.astype(BF16)
        lf_lo = (log_fail - lf_hi.astype(F32)).astype(BF16)
        stack = lambda a: jnp.concatenate([a[:, p * tk:(p + 1) * tk] for p in range(n)], axis=0)
        inner = _dot(stack(lf_hi), later) + _dot(stack(lf_lo), later)
        carry = carry_ref[...]
        tails = []
        for p in range(n):
            tails.append(inner[p * nr:(p + 1) * nr] + carry)
            carry = carry + jnp.sum(log_fail[:, p * tk:(p + 1) * tk], axis=-1, keepdims=True)
        carry_ref[...] = carry
        w = jnp.exp(log_beta + jnp.concatenate(tails, axis=1))
        if valid is not None:
            w = jnp.where(valid, w, 0.0)
        o_ref[...] += _dot_nt(w.astype(BF16), vcat)

    @pl.when(j == 0)
    def _():
        carry_ref[...] = jnp.zeros_like(carry_ref)
        o_ref[...] = jnp.zeros_like(o_ref)
        qi = lax.broadcasted_iota(I32, (nr, tk), 0) % n_new
        jn = lax.broadcasted_iota(I32, (nr, tk), 1)
        blocks([knew_ref[...]], [vnew_ref[...]], jn < qi)

    blocks([k_ref[...] for k_ref in k_refs], [v_ref[...] for v_ref in v_refs], None)


def _sb_sample(page_table, qbd, bias_rows, knew_t, vnew_t, cache_kt, cache_vt, pages_per_step, n_new, name):
    n_seq, n_pages = page_table.shape
    nr, hd = qbd.shape[1], qbd.shape[2]
    n_heads, d, page = cache_kt.shape[1], cache_kt.shape[2], cache_kt.shape[3]
    steps = n_pages // pages_per_step

    def page_spec(p):
        return pl.BlockSpec((None, n_heads, d, page),
                            lambda s, j, pt: (pt[s, n_pages - 1 - (j * pages_per_step + p)], 0, 0, 0))

    seq3 = lambda s, j, pt: (s, 0, 0)
    seq4 = lambda s, j, pt: (s, 0, 0, 0)
    grid_spec = pltpu.PrefetchScalarGridSpec(
        num_scalar_prefetch=1,
        grid=(n_seq, steps),
        in_specs=[pl.BlockSpec((None, nr, hd), seq3),
                  pl.BlockSpec((nr, 1), lambda s, j, pt: (0, 0)),
                  pl.BlockSpec((None, n_heads, d, page), seq4),
                  pl.BlockSpec((None, n_heads, d, page), seq4)]
                 + [page_spec(p) for p in range(pages_per_step)] * 2,
        out_specs=pl.BlockSpec((None, nr, hd), seq3),
        scratch_shapes=[pltpu.VMEM((nr, 1), F32)],
    )
    return pl.pallas_call(
        functools.partial(_sb_sample_body, pages_per_step=pages_per_step, n_new=n_new),
        grid_spec=grid_spec,
        out_shape=jax.ShapeDtypeStruct((n_seq, nr, hd), F32),
        compiler_params=_params("parallel", "arbitrary"),
        name=name,
    )(page_table, qbd, bias_rows, knew_t, vnew_t, *([cache_kt] * pages_per_step), *([cache_vt] * pages_per_step))


DIL_BLOCK = 128


def _dil_unit(start, prev_in_block, prev_ok, dil, q_ref, kc_ref, kp_ref, vc_ref, vp_ref, og_ref, lg_ref, d, scale):
    t = DIL_BLOCK
    rows = q_ref.shape[0]
    span = t * dil

    def idx(s0):
        return pl.ds(s0, t) if dil == 1 else pl.ds(s0, t, stride=dil)

    first_slot = lax.broadcasted_iota(I32, (t, LANES), 1) < d
    u = lax.broadcasted_iota(I32, (2 * t, 2 * t), 0) % t
    c = lax.broadcasted_iota(I32, (2 * t, 2 * t), 1)
    band_prev = jnp.logical_and(c < t, c >= u)
    band_cur = jnp.logical_and(c >= t, c - t <= u)
    if prev_ok is not None:
        band_prev = jnp.logical_and(band_prev, prev_ok)

    q2 = q_ref[idx(start), :] * scale
    if prev_in_block:
        k_prev, v_prev = kc_ref[idx(start - span), :], vc_ref[idx(start - span), :]
    else:
        k_prev, v_prev = kp_ref[idx(start + rows - span), :], vp_ref[idx(start + rows - span), :]
    qq = jnp.concatenate([jnp.where(first_slot, q2, 0.0), jnp.where(first_slot, 0.0, q2)], axis=0)
    kk = jnp.concatenate([k_prev, kc_ref[idx(start), :]], axis=0).astype(BF16)
    vv = jnp.concatenate([v_prev, vc_ref[idx(start), :]], axis=0).astype(BF16)
    s = jnp.where(jnp.logical_or(band_prev, band_cur), _dot_nt(qq.astype(BF16), kk), -jnp.inf)
    mx = jnp.max(s, axis=-1, keepdims=True)
    p = jnp.exp(s - mx)
    den = jnp.sum(p, axis=-1, keepdims=True)
    on = _dot(p.astype(BF16), vv) / den
    lse = jnp.broadcast_to(mx + jnp.log(den), on.shape)
    og_ref[idx(start), :] = jnp.where(first_slot, on[:t], on[t:])
    lg_ref[idx(start), :] = jnp.where(first_slot, lse[:t], lse[t:])


def _pairwise_loop(count, fn):
    def pair(m, carry):
        fn(2 * m)
        fn(2 * m + 1)
        return carry

    if count // 2:
        lax.fori_loop(0, count // 2, pair, 0)
    if count % 2:
        fn(count - 1)


def _dil_prompt_body(*refs, dilations, d, scale):
    n_g = len(dilations)
    ins = refs[:5 * n_g]
    o_ref = refs[5 * n_g]
    og_refs = refs[5 * n_g + 1:5 * n_g + 1 + n_g]
    lg_refs = refs[5 * n_g + 1 + n_g:]
    j = pl.program_id(2)
    rows = o_ref.shape[0]

    for g, dil in enumerate(dilations):
        unit = functools.partial(_dil_unit, dil=dil, q_ref=ins[5 * g], kc_ref=ins[5 * g + 1], kp_ref=ins[5 * g + 2],
                                 vc_ref=ins[5 * g + 3], vp_ref=ins[5 * g + 4], og_ref=og_refs[g],
                                 lg_ref=lg_refs[g], d=d, scale=scale)
        span = DIL_BLOCK * dil

        def first(r, unit=unit):
            unit(r, False, j > 0)

        def rest(n, unit=unit, dil=dil, span=span):
            start = n % dil + (n // dil + 1) * span
            if dil == 1:
                start = pl.multiple_of(start, DIL_BLOCK)
            unit(start, True, None)

        _pairwise_loop(dil, first)
        _pairwise_loop((rows // span - 1) * dil, rest)

    ls_ = [lg_ref[...] for lg_ref in lg_refs]
    mx = functools.reduce(jnp.maximum, ls_)
    es = [jnp.exp(l - mx) for l in ls_]
    num = functools.reduce(lambda a, b: a + b, [e * og_ref[...] for e, og_ref in zip(es, og_refs)])
    o_ref[...] = num / functools.reduce(lambda a, b: a + b, es)


def _dil_prompt(qkv3, dilations, n_slots, d, name):
    b, length, _ = qkv3.shape
    n_g = len(dilations)
    rows = DIL_BLOCK * max(dilations)
    sb = LANES // d
    gw = n_g * n_slots * d
    cur = lambda bi, si, j, c0: (bi, j, c0 + si)
    prev = lambda bi, si, j, c0: (bi, jnp.maximum(j - 1, 0), c0 + si)
    in_specs = []
    for g in range(n_g):
        qc, kc, vc = (g * n_slots * d) // LANES, (gw + g * n_slots * d) // LANES, (2 * gw + g * n_slots * d) // LANES
        for fn, c0 in ((cur, qc), (cur, kc), (prev, kc), (cur, vc), (prev, vc)):
            in_specs.append(pl.BlockSpec((None, rows, LANES), functools.partial(fn, c0=c0)))
    return pl.pallas_call(
        functools.partial(_dil_prompt_body, dilations=tuple(dilations), d=d, scale=d ** -0.5),
        grid=(b, n_slots // sb, length // rows),
        in_specs=in_specs,
        out_specs=pl.BlockSpec((None, rows, LANES), lambda bi, si, j: (bi, j, si)),
        out_shape=jax.ShapeDtypeStruct((b, length, n_slots * d), F32),
        scratch_shapes=[pltpu.VMEM((rows, LANES), F32)] * (2 * n_g),
        compiler_params=_params("parallel", "parallel", "arbitrary"),
        name=name,
    )(*([qkv3] * len(in_specs)))


def _dil_sample_body(qbd_ref, kbuf_ref, vbuf_ref, knew_ref, vnew_ref, o_ref, l_ref, kout_ref, vout_ref,
                     *, dil, n_new):
    n_slots, d, w = kbuf_ref.shape
    hd = n_slots * d
    nr = qbd_ref.shape[0]
    pad = knew_ref.shape[1]
    kb = kbuf_ref[...].reshape(hd, w)
    vb = vbuf_ref[...].reshape(hd, w)
    kn = knew_ref[...]
    vn = vnew_ref[...]
    tail_lane = lax.broadcasted_iota(I32, (hd, pad), 1) >= pad - n_new
    for src, new, out_ref in ((kb, kn, kout_ref), (vb, vn, vout_ref)):
        slid = pltpu.roll(src, w - n_new, 1)
        last = jnp.where(tail_lane, pltpu.roll(new, pad - n_new, 1), slid[:, w - pad:])
        if w > pad:
            out_ref[:, :, 0:w - pad] = slid[:, 0:w - pad].reshape(n_slots, d, w - pad)
        out_ref[:, :, w - pad:w] = last.reshape(n_slots, d, pad)

    qbd = qbd_ref[...]
    qi = lax.broadcasted_iota(I32, (nr, w), 0) % n_new
    diff = w + qi - lax.broadcasted_iota(I32, (nr, w), 1)
    ok_buf = jnp.logical_and(diff % dil == 0, diff <= dil * DIL_BLOCK)
    qi_n = lax.broadcasted_iota(I32, (nr, pad), 0) % n_new
    jn = lax.broadcasted_iota(I32, (nr, pad), 1)
    diff_n = qi_n - jn
    ok_new = jnp.logical_and(jnp.logical_and(diff_n >= 0, diff_n % dil == 0), jn < n_new)
    s_buf = jnp.where(ok_buf, _dot(qbd, kb.astype(BF16)), -jnp.inf)
    s_new = jnp.where(ok_new, _dot(qbd, kn.astype(BF16)), -jnp.inf)
    mx = jnp.maximum(jnp.max(s_buf, axis=-1, keepdims=True), jnp.max(s_new, axis=-1, keepdims=True))
    p_buf = jnp.exp(s_buf - mx)
    p_new = jnp.exp(s_new - mx)
    den = jnp.sum(p_buf, axis=-1, keepdims=True) + jnp.sum(p_new, axis=-1, keepdims=True)
    acc = _dot_nt(p_buf.astype(BF16), vb.astype(BF16)) + _dot_nt(p_new.astype(BF16), vn.astype(BF16))
    o_ref[...] = acc / den
    l_ref[...] = jnp.broadcast_to(mx + jnp.log(den), l_ref.shape)


def _dil_sample(qbd, kbuf_t, vbuf_t, knew_t, vnew_t, dil, n_new, name):
    n_seq, n_slots, d, w = kbuf_t.shape
    nr, hd = qbd.shape[1], qbd.shape[2]
    pad = knew_t.shape[2]
    s3 = lambda s: (s, 0, 0)
    s4 = lambda s: (s, 0, 0, 0)
    buf_spec = pl.BlockSpec((None, n_slots, d, w), s4)
    return pl.pallas_call(
        functools.partial(_dil_sample_body, dil=dil, n_new=n_new),
        grid=(n_seq,),
        in_specs=[pl.BlockSpec((None, nr, hd), s3), buf_spec, buf_spec,
                  pl.BlockSpec((None, hd, pad), s3), pl.BlockSpec((None, hd, pad), s3)],
        out_specs=[pl.BlockSpec((None, nr, hd), s3), pl.BlockSpec((None, nr, LANES), s3), buf_spec, buf_spec],
        out_shape=[jax.ShapeDtypeStruct((n_seq, nr, hd), F32), jax.ShapeDtypeStruct((n_seq, nr, LANES), F32),
                   jax.ShapeDtypeStruct(kbuf_t.shape, F32), jax.ShapeDtypeStruct(kbuf_t.shape, F32)],
        compiler_params=_params("parallel"),
        name=name,
    )(qbd, kbuf_t, vbuf_t, knew_t, vnew_t)


def _router_body(x_ref, w_ref, idx_ref, gate_ref, *, n_experts):
    xh, xm, _ = _split3(x_ref[...])
    wh, wm, _ = _split3(w_ref[...])
    logits = _dot(xh, wh) + _dot(xm, wh) + _dot(xh, wm)
    lane = lax.broadcasted_iota(I32, logits.shape, 1).astype(F32)
    logits = jnp.where(lane < n_experts, logits, -jnp.inf)
    m1 = jnp.max(logits, axis=-1, keepdims=True)
    i1 = jnp.min(jnp.where(logits == m1, lane, float(LANES)), axis=-1, keepdims=True)
    rest = jnp.where(lane == i1, -jnp.inf, logits)
    m2 = jnp.max(rest, axis=-1, keepdims=True)
    i2 = jnp.min(jnp.where(rest == m2, lane, float(LANES)), axis=-1, keepdims=True)
    e = jnp.exp(m2 - m1)
    g1 = 1.0 / (1.0 + e)
    idx_ref[...] = jnp.where(lane == 0.0, i1, jnp.where(lane == 1.0, i2, 0.0)).astype(I32)
    gate_ref[...] = jnp.where(lane == 0.0, g1, jnp.where(lane == 1.0, e * g1, 0.0))


def _router(x, w_pad, n_experts, tm, name):
    m, d = x.shape
    return pl.pallas_call(
        functools.partial(_router_body, n_experts=n_experts),
        grid=(m // tm,),
        in_specs=[pl.BlockSpec((tm, d), lambda i: (i, 0)), pl.BlockSpec((d, LANES), lambda i: (0, 0))],
        out_specs=[pl.BlockSpec((tm, LANES), lambda i: (i, 0))] * 2,
        out_shape=[jax.ShapeDtypeStruct((m, LANES), I32), jax.ShapeDtypeStruct((m, LANES), F32)],
        compiler_params=_params("parallel"),
        name=name,
    )(x, w_pad)


DMA_ISSUE_UNROLL = 8


def _row_copy(src_hbm, row, dst, r, sem):
    return pltpu.make_async_copy(src_hbm.at[pl.ds(row, 1)], dst.at[pl.ds(r, 1)], sem)


def _gather_body(src_ref, x_hbm, o_ref, buf, sem, *, rows):
    i = pl.program_id(0)

    def start(step):
        slot = step % 2

        def one(r, c):
            _row_copy(x_hbm, src_ref[step * rows + r], buf.at[slot], r, sem.at[slot]).start()
            return c

        lax.fori_loop(0, rows, one, 0, unroll=DMA_ISSUE_UNROLL)

    @pl.when(i == 0)
    def _():
        start(i)

    @pl.when(i + 1 < pl.num_programs(0))
    def _():
        start(i + 1)

    slot = i % 2
    pltpu.make_async_copy(x_hbm.at[pl.ds(0, rows)], buf.at[slot], sem.at[slot]).wait()
    o_ref[...] = buf[slot].astype(BF16)


def _gather_rows(src, x, rows, name):
    n = src.shape[0]
    d = x.shape[1]
    grid_spec = pltpu.PrefetchScalarGridSpec(
        num_scalar_prefetch=1,
        grid=(n // rows,),
        in_specs=[pl.BlockSpec(memory_space=pl.ANY)],
        out_specs=pl.BlockSpec((rows, d), lambda i, s: (i, 0)),
        scratch_shapes=[pltpu.VMEM((2, rows, d), x.dtype), pltpu.SemaphoreType.DMA((2,))],
    )
    return pl.pallas_call(
        functools.partial(_gather_body, rows=rows),
        grid_spec=grid_spec,
        out_shape=jax.ShapeDtypeStruct((n, d), BF16),
        compiler_params=_params("arbitrary"),
        name=name,
    )(src, x)


def _experts_body(te_ref, tb_ref, nu_ref, x_ref, w1_ref, w3_ref, w2_ref, o_ref, acc_ref):
    i = pl.program_id(0)
    f = pl.program_id(1)

    @pl.when(i < nu_ref[0])
    def _():
        @pl.when(f == 0)
        def _():
            acc_ref[...] = jnp.zeros_like(acc_ref)

        x = x_ref[...]
        hid = _silu(_dot(x, w1_ref[...].astype(BF16))) * _dot(x, w3_ref[...].astype(BF16))
        acc_ref[...] += _dot(hid.astype(BF16), w2_ref[...].astype(BF16))

        @pl.when(f == pl.num_programs(1) - 1)
        def _():
            o_ref[...] = acc_ref[...]

    @pl.when(jnp.logical_and(i >= nu_ref[0], f == pl.num_programs(1) - 1))
    def _():
        o_ref[...] = jnp.zeros_like(o_ref)


def _experts(tile_expert, tile_block, n_used, xs, w1, w3, w2, tm, tf, name):
    n_tiles = tile_expert.shape[0]
    d = xs.shape[1]
    ff = w1.shape[2]
    nf = ff // tf

    def fcol(i, f, nu):
        return jnp.where(i < nu[0], f, nf - 1)

    grid_spec = pltpu.PrefetchScalarGridSpec(
        num_scalar_prefetch=3,
        grid=(n_tiles, nf),
        in_specs=[pl.BlockSpec((tm, d), lambda i, f, te, tb, nu: (tb[i], 0)),
                  pl.BlockSpec((None, d, tf), lambda i, f, te, tb, nu: (te[i], 0, fcol(i, f, nu))),
                  pl.BlockSpec((None, d, tf), lambda i, f, te, tb, nu: (te[i], 0, fcol(i, f, nu))),
                  pl.BlockSpec((None, tf, d), lambda i, f, te, tb, nu: (te[i], fcol(i, f, nu), 0))],
        out_specs=pl.BlockSpec((tm, d), lambda i, f, te, tb, nu: (i, 0)),
        scratch_shapes=[pltpu.VMEM((tm, d), F32)],
    )
    return pl.pallas_call(
        _experts_body,
        grid_spec=grid_spec,
        out_shape=jax.ShapeDtypeStruct((n_tiles * tm, d), F32),
        compiler_params=_params("arbitrary", "arbitrary"),
        name=name,
    )(tile_expert, tile_block, n_used, xs, w1, w3, w2)


def _combine_ln_body(slot_ref, y_hbm, x_ref, gate_ref, g_ref, b_ref, o_ref, buf1, buf2, sem, *, rows, t0, alpha):
    base = (pl.program_id(0) * rows + t0) * TOP_K

    def start(r, c):
        _row_copy(y_hbm, slot_ref[base + TOP_K * r], buf1, r, sem).start()
        _row_copy(y_hbm, slot_ref[base + TOP_K * r + 1], buf2, r, sem).start()
        return c

    lax.fori_loop(0, rows, start, 0, unroll=DMA_ISSUE_UNROLL)
    pltpu.make_async_copy(y_hbm.at[pl.ds(0, rows)], buf1, sem).wait()
    pltpu.make_async_copy(y_hbm.at[pl.ds(0, rows)], buf2, sem).wait()
    gate = gate_ref[...]
    moe = gate[:, 0:1] * buf1[...] + gate[:, 1:2] * buf2[...]
    o_ref[...] = _layer_norm(alpha * x_ref[...] + moe, g_ref[...], b_ref[...])


def _combine_ln(slot, ys, x, gate, g, b, t0, alpha, rows, name):
    m, d = x.shape
    gb = t0 // rows
    grid_spec = pltpu.PrefetchScalarGridSpec(
        num_scalar_prefetch=1,
        grid=(m // rows,),
        in_specs=[pl.BlockSpec(memory_space=pl.ANY),
                  pl.BlockSpec((rows, d), lambda i, s: (i, 0)),
                  pl.BlockSpec((rows, LANES), lambda i, s: (i + gb, 0)),
                  pl.BlockSpec((1, d), lambda i, s: (0, 0)),
                  pl.BlockSpec((1, d), lambda i, s: (0, 0))],
        out_specs=pl.BlockSpec((rows, d), lambda i, s: (i, 0)),
        scratch_shapes=[pltpu.VMEM((rows, d), F32), pltpu.VMEM((rows, d), F32), pltpu.SemaphoreType.DMA(())],
    )
    return pl.pallas_call(
        functools.partial(_combine_ln_body, rows=rows, t0=t0, alpha=alpha),
        grid_spec=grid_spec,
        out_shape=jax.ShapeDtypeStruct((m, d), F32),
        compiler_params=_params("arbitrary"),
        name=name,
    )(slot, ys, x, gate, g, b)


def _moe_ln(xp, xs_, router_w, w1, w3, w2, g, b, alpha):
    n_experts = router_w.shape[1]
    d = xp.shape[1]
    tp, ts = xp.shape[0], xs_.shape[0]
    x_all = jnp.concatenate([xp, xs_], axis=0)
    t_all = tp + ts
    rw = jnp.pad(router_w, ((0, 0), (0, LANES - n_experts)))
    idx, gate = _router(x_all, rw, n_experts, ts, "moe_router")

    tm = 1024
    n_assign = t_all * TOP_K
    e_flat = idx[:, :TOP_K].reshape(n_assign)
    onehot = (e_flat[:, None] == jnp.arange(n_experts, dtype=I32)[None, :]).astype(I32)
    csum = jnp.cumsum(onehot, axis=0)
    pos = jnp.take_along_axis(csum, e_flat[:, None], axis=1)[:, 0] - 1
    counts = csum[-1]
    padded = ((counts + tm - 1) // tm) * tm
    ends = jnp.cumsum(padded)
    offs = ends - padded
    slot = (offs[e_flat] + pos).astype(I32)
    n_tiles = (n_assign + n_experts * (tm - 1)) // tm
    src = jnp.zeros((n_tiles * tm,), I32).at[slot].set(jnp.arange(n_assign, dtype=I32) // TOP_K)
    n_used = (ends[-1] // tm).astype(I32)
    tiles = jnp.arange(n_tiles, dtype=I32)
    t_exp = jnp.minimum(jnp.sum((tiles[:, None] * tm >= ends[None, :]).astype(I32), axis=1), n_experts - 1)
    last = jnp.maximum(n_used - 1, 0)
    tile_expert = jnp.where(tiles < n_used, t_exp, t_exp[last])
    tile_block = jnp.where(tiles < n_used, tiles, last)

    x_sorted = _gather_rows(src, x_all, 256, "moe_gather")
    y_sorted = _experts(tile_expert, tile_block, n_used.reshape(1), x_sorted, w1, w3, w2, tm, 512, "moe_experts")
    yp = _combine_ln(slot, y_sorted, xp, gate, g, b, 0, alpha, 256, "moe_combine_prompt")
    ys = _combine_ln(slot, y_sorted, xs_, gate, g, b, tp, alpha, 256, "moe_combine_sample")
    return yp, ys


def _rope_tables(pos):
    half = ROT_DIM // 2
    inv = ROPE_THETA ** (-jnp.arange(0, ROT_DIM, 2, dtype=F32) / ROT_DIM)
    ang = pos.astype(F32)[:, None] * inv[None, :]
    cos, sin = jnp.cos(ang), jnp.sin(ang)
    n = pos.shape[0]
    ones = jnp.ones((n, HEAD_DIM - ROT_DIM), F32)
    zeros = jnp.zeros((n, HEAD_DIM - ROT_DIM), F32)
    zh = jnp.zeros((n, half), F32)
    reps = LANES // HEAD_DIM
    c = jnp.tile(jnp.concatenate([cos, cos, ones], axis=1), (1, reps))
    sa = jnp.tile(jnp.concatenate([-sin, zh, zeros], axis=1), (1, reps))
    sb = jnp.tile(jnp.concatenate([zh, sin, zeros], axis=1), (1, reps))
    return c, sa, sb


def _block_diag_q(q, scale):
    n_seq, n_new, nh, d = q.shape
    eye = jnp.eye(nh, dtype=F32)
    out = jnp.einsum("sihd,hg->shigd", q * scale, eye)
    return out.reshape(n_seq, nh * n_new, nh * d).astype(BF16)


def _diag_blocks(o, n_new, nh, d):
    n_seq = o.shape[0]
    o5 = o.reshape(n_seq, nh, n_new, nh, d)
    sel = jnp.stack([o5[:, h, :, h, :] for h in range(nh)], axis=2)
    return sel.reshape(n_seq * n_new, nh * d)


def kernel(x_prompt, x_sample, cache_sb_k, cache_sb_v, page_table, state_ssm, state_conv, cache_c_w128_k, cache_c_w128_v, cache_c_w512_k, cache_c_w512_v, cache_c_w2048_k, cache_c_w2048_v, w_in_a, conv_w, conv_b, dt_bias, a_log, d_skip, ssm_norm_w, sb_bias, w_out_a, ln_a_mix_g, ln_a_mix_b, ffn_w1, ffn_w3, ffn_w2, ln_a_ffn_g, ln_a_ffn_b, w_in_c, w_out_c, ln_c_mix_g, ln_c_mix_b, router_w, moe_w1, moe_w3, moe_w2, ln_c_ffn_g, ln_c_ffn_b):
    n_pr, seq, d_model = x_prompt.shape
    n_dec, dec_seq, _ = x_sample.shape
    n_pages = page_table.shape[1]
    page = cache_sb_k.shape[2]
    past_len = n_pages * page
    ssm_heads, p, n_state = state_ssm.shape[2], state_ssm.shape[3], state_ssm.shape[4]
    d_inner = ssm_heads * p
    conv_dim = state_conv.shape[3]
    sb_heads = cache_sb_k.shape[3]
    sb_w = sb_heads * HEAD_DIM
    c_slots = cache_c_w128_k.shape[3]
    c_wd = c_slots * HEAD_DIM
    n_groups = len(C_GROUPS)
    depth = w_in_a.shape[0] + w_in_c.shape[0]
    alpha = (2 * depth) ** 0.25
    tp = n_pr * seq
    ts = n_dec * dec_seq
    win_k = (cache_c_w128_k, cache_c_w512_k, cache_c_w2048_k)
    win_v = (cache_c_w128_v, cache_c_w512_v, cache_c_w2048_v)
    row = lambda v: v.reshape(1, -1)

    yp = x_prompt.reshape(tp, d_model)
    ys = x_sample.reshape(ts, d_model)
    outs = {}

    i = 0
    w = w_in_a[i]
    c0, c1, c2 = d_inner, d_inner + conv_dim, d_inner + conv_dim + ssm_heads
    w_re = jnp.concatenate([w[:, c0:c1], w[:, :c0], w[:, c2:], w[:, c1:c2],
                            jnp.zeros((d_model, LANES - ssm_heads), F32)], axis=1).astype(BF16)
    o_q = conv_dim + d_inner
    proj_p = _matmul(yp, w_re, 512, "in_proj_a_prompt")
    proj_s = _matmul(ys, w_re, ts, "in_proj_a_sample")

    pad_l = LANES - ssm_heads
    ssd_par = (conv_w[i], row(conv_b[i]), row(jnp.pad(dt_bias[i], (0, pad_l))),
               row(jnp.pad(-jnp.exp(a_log[i]), (0, pad_l))), row(jnp.repeat(d_skip[i], p)), row(ssm_norm_w[i]))
    ya_p, ssm_p = _ssd(proj_p.reshape(n_pr, seq, -1), jnp.zeros((n_pr, CONV_W - 1, conv_dim), F32),
                       jnp.zeros((n_pr, ssm_heads, p, n_state), F32), *ssd_par, SSD_CHUNK, "ssd_prompt")
    proj_s3 = proj_s.reshape(n_dec, dec_seq, -1)
    proj_s_pad = jnp.pad(proj_s3, ((0, 0), (0, SSD_CHUNK - dec_seq), (0, 0)))
    ya_s, ssm_s = _ssd(proj_s_pad, state_conv[i], state_ssm[i], *ssd_par, dec_seq, "ssd_sample")
    ya_p = ya_p.reshape(tp, d_inner)
    ya_s = ya_s[:, :dec_seq].reshape(ts, d_inner)
    conv_p = proj_p.reshape(n_pr, seq, -1)[:, seq - (CONV_W - 1):, :conv_dim]
    conv_s = jnp.concatenate([state_conv[i], proj_s3[:, :, :conv_dim]], axis=1)[:, dec_seq:]

    k_p = proj_p[:, o_q + sb_w:o_q + 2 * sb_w].reshape(n_pr, seq, sb_heads, HEAD_DIM)
    v_p = proj_p[:, o_q + 2 * sb_w:o_q + 3 * sb_w].reshape(n_pr, seq, sb_heads, HEAD_DIM)
    o_p = _sb_prompt(sb_bias[i], proj_p.reshape(n_pr, seq, -1), o_q, o_q + sb_w, o_q + 2 * sb_w,
                     sb_heads, HEAD_DIM, 256, "sb_prompt").reshape(tp, sb_w)

    q_s = proj_s[:, o_q:o_q + sb_w].reshape(n_dec, dec_seq, sb_heads, HEAD_DIM)
    k_s = proj_s[:, o_q + sb_w:o_q + 2 * sb_w].reshape(n_dec, dec_seq, sb_w)
    v_s = proj_s[:, o_q + 2 * sb_w:o_q + 3 * sb_w].reshape(n_dec, dec_seq, sb_w)
    qbd = _block_diag_q(q_s, HEAD_DIM ** -0.5)
    bias_rows = jnp.repeat(sb_bias[i], dec_seq).reshape(sb_heads * dec_seq, 1)

    def keys_minor(a):
        return a.transpose(0, 2, 3, 1)

    pad_new = ((0, 0), (0, 0), (0, 0), (0, page - dec_seq))
    o_s = _sb_sample(page_table, qbd, bias_rows,
                     jnp.pad(keys_minor(k_s.reshape(n_dec, dec_seq, sb_heads, HEAD_DIM)), pad_new),
                     jnp.pad(keys_minor(v_s.reshape(n_dec, dec_seq, sb_heads, HEAD_DIM)), pad_new),
                     keys_minor(cache_sb_k[i]), keys_minor(cache_sb_v[i]), 16, dec_seq, "sb_sample")
    o_s = _diag_blocks(o_s, dec_seq, sb_heads, HEAD_DIM)

    wo = w_out_a[i].astype(BF16)
    ln_g, ln_b = row(ln_a_mix_g[i]), row(ln_a_mix_b[i])
    wo2 = [wo[:d_inner], wo[d_inner:]]
    yp = _proj_ln([ya_p, o_p], wo2, yp, ln_g, ln_b, alpha, 512, "out_proj_a_prompt")
    ys = _proj_ln([ya_s, o_s], wo2, ys, ln_g, ln_b, alpha, ts, "out_proj_a_sample")
    f1, f3, f2 = ffn_w1[i].astype(BF16), ffn_w3[i].astype(BF16), ffn_w2[i].astype(BF16)
    ln_g, ln_b = row(ln_a_ffn_g[i]), row(ln_a_ffn_b[i])
    d_ff = f1.shape[1]
    tf = d_ff // 2 if d_ff % (2 * LANES) == 0 else LANES
    yp = _ffn_ln(yp, f1, f3, f2, ln_g, ln_b, alpha, 512, tf, "ffn_prompt")
    ys = _ffn_ln(ys, f1, f3, f2, ln_g, ln_b, alpha, ts, tf, "ffn_sample")

    outs["sb_kp"] = k_p[None]
    outs["sb_vp"] = v_p[None]
    outs["sb_ks"] = k_s.reshape(1, n_dec, dec_seq, sb_heads, HEAD_DIM)
    outs["sb_vs"] = v_s.reshape(1, n_dec, dec_seq, sb_heads, HEAD_DIM)
    outs["ssm_p"], outs["ssm_s"], outs["conv_p"], outs["conv_s"] = ssm_p[None], ssm_s[None], conv_p[None], conv_s[None]

    wc = w_in_c[i].astype(BF16)
    gw = n_groups * c_wd
    cos_p, sa_p, sb_p = _rope_tables(jnp.arange(seq))
    cos_s, sa_s, sb_s = _rope_tables(past_len + jnp.arange(ts) % dec_seq)
    qkv_p = _matmul_rope(yp, wc, cos_p, sa_p, sb_p, 512, "in_proj_c_prompt")
    qkv_s = _matmul_rope(ys, wc, cos_s, sa_s, sb_s, ts, "in_proj_c_sample")

    scale = HEAD_DIM ** -0.5
    o_c_p = _dil_prompt(qkv_p.reshape(n_pr, seq, -1), [dil for _, dil in C_GROUPS], c_slots, HEAD_DIM,
                        "dilated_prompt").reshape(tp, c_wd)
    o_groups_s, l_groups_s = [], []
    for g, (win, dil) in enumerate(C_GROUPS):
        cq, ck, cv = g * c_wd, gw + g * c_wd, 2 * gw + g * c_wd
        keep = min(win, seq)
        last_rows = qkv_p.reshape(n_pr, seq, -1)[:, seq - keep:]
        outs[f"c_kp{g}"] = last_rows[:, :, ck:ck + c_wd].reshape(1, n_pr, keep, c_slots, HEAD_DIM)
        outs[f"c_vp{g}"] = last_rows[:, :, cv:cv + c_wd].reshape(1, n_pr, keep, c_slots, HEAD_DIM)

        qg_s = qkv_s[:, cq:cq + c_wd].reshape(n_dec, dec_seq, c_slots, HEAD_DIM)
        pad_keys = ((0, 0), (0, 0), (0, DIL_BLOCK - dec_seq))
        kn = jnp.pad(qkv_s[:, ck:ck + c_wd].reshape(n_dec, dec_seq, c_wd).transpose(0, 2, 1), pad_keys)
        vn = jnp.pad(qkv_s[:, cv:cv + c_wd].reshape(n_dec, dec_seq, c_wd).transpose(0, 2, 1), pad_keys)
        o_g, l_g, k_new, v_new = _dil_sample(_block_diag_q(qg_s, scale), keys_minor(win_k[g][i]),
                                             keys_minor(win_v[g][i]), kn, vn, dil, dec_seq,
                                             f"dilated_sample_{g}")
        o_groups_s.append(_diag_blocks(o_g, dec_seq, c_slots, HEAD_DIM))
        l_s = l_g[:, :, 0].reshape(n_dec, c_slots, dec_seq).transpose(0, 2, 1)
        l_groups_s.append(jnp.repeat(l_s.reshape(ts, c_slots), HEAD_DIM, axis=1))
        outs[f"c_ks{g}"] = k_new.transpose(0, 3, 1, 2)[None]
        outs[f"c_vs{g}"] = v_new.transpose(0, 3, 1, 2)[None]

    woc = w_out_c[i].astype(BF16)
    ln_g, ln_b = row(ln_c_mix_g[i]), row(ln_c_mix_b[i])
    yp = _proj_ln([o_c_p], [woc], yp, ln_g, ln_b, alpha, 512, "out_proj_c_prompt")
    ys = _merge_proj_ln(o_groups_s, l_groups_s, woc, ys, ln_g, ln_b, alpha, ts, "out_proj_c_sample")
    yp, ys = _moe_ln(yp, ys, router_w[i], moe_w1[i], moe_w3[i], moe_w2[i],
                     row(ln_c_ffn_g[i]), row(ln_c_ffn_b[i]), alpha)

    res = [yp.reshape(n_pr, seq, d_model), ys.reshape(n_dec, dec_seq, d_model),
           outs["sb_kp"], outs["sb_vp"], outs["sb_ks"], outs["sb_vs"],
           outs["ssm_p"], outs["ssm_s"], outs["conv_p"], outs["conv_s"]]
    for g in range(n_groups):
        res += [outs[f"c_kp{g}"], outs[f"c_vp{g}"], outs[f"c_ks{g}"], outs[f"c_vs{g}"]]
    return tuple(res)
```

```python
import functools
import math

import numpy as np
import jax
import jax.numpy as jnp
from jax import lax
from jax.experimental import pallas as pl
from jax.experimental.pallas import tpu as pltpu

F32 = jnp.float32
BF16 = jnp.bfloat16
I32 = jnp.int32

HEAD_DIM = 64
SSM_HEAD_DIM = 64
SSM_GROUPS = 2
SSM_STATE = 128
CONV_W = 4
SSD_CHUNK = 128
C_GROUPS = ((128, 1), (512, 4), (2048, 16))
ROT_DIM = HEAD_DIM // 4
ROPE_THETA = 500000.0
TOP_K = 2
LN_EPS = 1e-5
RMS_EPS = 1e-6

LANES = 128
SUBLANES = 8
VMEM_LIMIT_BYTES = 56 * 1024 * 1024


def _params(*semantics):
    return pltpu.CompilerParams(dimension_semantics=semantics, vmem_limit_bytes=VMEM_LIMIT_BYTES)


def _dot(a, b):
    return jnp.dot(a, b, preferred_element_type=F32)


def _dot_nt(a, b):
    return lax.dot_general(a, b, (((1,), (1,)), ((), ())), preferred_element_type=F32)


def _layer_norm(y, g, b):
    mu = jnp.mean(y, axis=-1, keepdims=True)
    d = y - mu
    var = jnp.mean(d * d, axis=-1, keepdims=True)
    return d * lax.rsqrt(var + LN_EPS) * g + b


def _silu(a):
    return a * jax.nn.sigmoid(a)


def _split3(a):
    hi = a.astype(BF16)
    r = a - hi.astype(F32)
    mid = r.astype(BF16)
    lo = (r - mid.astype(F32)).astype(BF16)
    return hi, mid, lo


def _mm_body(x_ref, w_ref, o_ref):
    o_ref[...] = _dot(x_ref[...].astype(BF16), w_ref[...])


def _matmul(x, w, tm, name):
    m, k = x.shape
    n = w.shape[1]
    return pl.pallas_call(
        _mm_body,
        grid=(m // tm,),
        in_specs=[pl.BlockSpec((tm, k), lambda i: (i, 0)), pl.BlockSpec((k, n), lambda i: (0, 0))],
        out_specs=pl.BlockSpec((tm, n), lambda i: (i, 0)),
        out_shape=jax.ShapeDtypeStruct((m, n), F32),
        compiler_params=_params("parallel"),
        name=name,
    )(x, w)


MXU_WIDTH = 2 * LANES


def _mm_rope_body(x_ref, w_ref, cos_ref, sa_ref, sb_ref, o_ref, xb_ref, *, n_rot_tiles):
    j = pl.program_id(1)

    @pl.when(j == 0)
    def _():
        xb_ref[...] = x_ref[...].astype(BF16)

    xb = xb_ref[...]

    @pl.when(j < n_rot_tiles)
    def _():
        cos = cos_ref[...]
        sa = sa_ref[...]
        sb = sb_ref[...]
        for c in range(o_ref.shape[1] // MXU_WIDTH):
            y = _dot(xb, w_ref[:, c * MXU_WIDTH:(c + 1) * MXU_WIDTH])
            for h in range(MXU_WIDTH // LANES):
                yc = y[:, h * LANES:(h + 1) * LANES]
                lo = c * MXU_WIDTH + h * LANES
                o_ref[:, lo:lo + LANES] = (
                    yc * cos + pltpu.roll(yc, LANES - ROT_DIM // 2, 1) * sa + pltpu.roll(yc, ROT_DIM // 2, 1) * sb)

    @pl.when(j >= n_rot_tiles)
    def _():
        o_ref[...] = _dot(xb, w_ref[...])


def _matmul_rope(x, w, cos, sa, sb, tm, name):
    m, k = x.shape
    n = w.shape[1]
    tn = n // 3
    nb = cos.shape[0] // tm
    return pl.pallas_call(
        functools.partial(_mm_rope_body, n_rot_tiles=2),
        grid=(m // tm, 3),
        in_specs=[pl.BlockSpec((tm, k), lambda i, j: (i, 0)),
                  pl.BlockSpec((k, tn), lambda i, j: (0, j)),
                  pl.BlockSpec((tm, LANES), lambda i, j: (i % nb, 0)),
                  pl.BlockSpec((tm, LANES), lambda i, j: (i % nb, 0)),
                  pl.BlockSpec((tm, LANES), lambda i, j: (i % nb, 0))],
        out_specs=pl.BlockSpec((tm, tn), lambda i, j: (i, j)),
        out_shape=jax.ShapeDtypeStruct((m, n), F32),
        scratch_shapes=[pltpu.VMEM((tm, k), BF16)],
        compiler_params=_params("parallel", "arbitrary"),
        name=name,
    )(x, w, cos, sa, sb)


def _proj_ln_body(*refs, alpha, n_in):
    h_refs, w_refs = refs[:n_in], refs[n_in:2 * n_in]
    r_ref, g_ref, b_ref, o_ref = refs[2 * n_in:]
    m = _dot(h_refs[0][...].astype(BF16), w_refs[0][...])
    for h_ref, w_ref in zip(h_refs[1:], w_refs[1:]):
        m = m + _dot(h_ref[...].astype(BF16), w_ref[...])
    o_ref[...] = _layer_norm(alpha * r_ref[...] + m, g_ref[...], b_ref[...])


def _proj_ln(hs, ws, resid, g, b, alpha, tm, name):
    m, d = resid.shape
    row = lambda i: (i, 0)
    fix = lambda i: (0, 0)
    return pl.pallas_call(
        functools.partial(_proj_ln_body, alpha=alpha, n_in=len(hs)),
        grid=(m // tm,),
        in_specs=[pl.BlockSpec((tm, h.shape[1]), row) for h in hs]
                 + [pl.BlockSpec((w.shape[0], d), fix) for w in ws]
                 + [pl.BlockSpec((tm, d), row), pl.BlockSpec((1, d), fix), pl.BlockSpec((1, d), fix)],
        out_specs=pl.BlockSpec((tm, d), row),
        out_shape=jax.ShapeDtypeStruct((m, d), F32),
        compiler_params=_params("parallel"),
        name=name,
    )(*hs, *ws, resid, g, b)


def _merge_proj_ln_body(o0_ref, o1_ref, o2_ref, l0_ref, l1_ref, l2_ref, w_ref, r_ref, g_ref, b_ref, o_ref, *, alpha):
    l0, l1, l2 = l0_ref[...], l1_ref[...], l2_ref[...]
    mx = jnp.maximum(jnp.maximum(l0, l1), l2)
    e0, e1, e2 = jnp.exp(l0 - mx), jnp.exp(l1 - mx), jnp.exp(l2 - mx)
    o = (e0 * o0_ref[...] + e1 * o1_ref[...] + e2 * o2_ref[...]) / (e0 + e1 + e2)
    m = _dot(o.astype(BF16), w_ref[...])
    o_ref[...] = _layer_norm(alpha * r_ref[...] + m, g_ref[...], b_ref[...])


def _merge_proj_ln(os_, ls_, w, resid, g, b, alpha, tm, name):
    m, d = resid.shape
    k = w.shape[0]
    row = lambda i: (i, 0)
    fix = lambda i: (0, 0)
    return pl.pallas_call(
        functools.partial(_merge_proj_ln_body, alpha=alpha),
        grid=(m // tm,),
        in_specs=[pl.BlockSpec((tm, k), row)] * 6 + [pl.BlockSpec((k, d), fix), pl.BlockSpec((tm, d), row),
                                                     pl.BlockSpec((1, d), fix), pl.BlockSpec((1, d), fix)],
        out_specs=pl.BlockSpec((tm, d), row),
        out_shape=jax.ShapeDtypeStruct((m, d), F32),
        compiler_params=_params("parallel"),
        name=name,
    )(*os_, *ls_, w, resid, g, b)


def _ffn_ln_body(x_ref, w1_ref, w3_ref, w2_ref, g_ref, b_ref, o_ref, xb_ref, acc_ref, *, alpha):
    f = pl.program_id(1)

    @pl.when(f == 0)
    def _():
        xb_ref[...] = x_ref[...].astype(BF16)
        acc_ref[...] = jnp.zeros_like(acc_ref)

    xb = xb_ref[...]
    hid = _silu(_dot(xb, w1_ref[...])) * _dot(xb, w3_ref[...])
    acc_ref[...] += _dot(hid.astype(BF16), w2_ref[...])

    @pl.when(f == pl.num_programs(1) - 1)
    def _():
        o_ref[...] = _layer_norm(alpha * x_ref[...] + acc_ref[...], g_ref[...], b_ref[...])


def _ffn_ln(x, w1, w3, w2, g, b, alpha, tm, tf, name):
    m, d = x.shape
    ff = w1.shape[1]
    return pl.pallas_call(
        functools.partial(_ffn_ln_body, alpha=alpha),
        grid=(m // tm, ff // tf),
        in_specs=[pl.BlockSpec((tm, d), lambda i, f: (i, 0)),
                  pl.BlockSpec((d, tf), lambda i, f: (0, f)),
                  pl.BlockSpec((d, tf), lambda i, f: (0, f)),
                  pl.BlockSpec((tf, d), lambda i, f: (f, 0)),
                  pl.BlockSpec((1, d), lambda i, f: (0, 0)),
                  pl.BlockSpec((1, d), lambda i, f: (0, 0))],
        out_specs=pl.BlockSpec((tm, d), lambda i, f: (i, 0)),
        out_shape=jax.ShapeDtypeStruct((m, d), F32),
        scratch_shapes=[pltpu.VMEM((tm, d), BF16), pltpu.VMEM((tm, d), F32)],
        compiler_params=_params("parallel", "arbitrary"),
        name=name,
    )(x, w1, w3, w2, g, b)


def _ssd_body(xbc_ref, z_ref, dt_ref, conv0_ref, h0_ref, cw_ref, cb_ref, dtb_ref, aneg_ref, dsk_ref, nw_ref,
              y_ref, hn_ref, xpad_ref, *, q, valid_len, n_heads, d_inner):
    c = pl.program_id(1)
    conv_dim = xpad_ref.shape[1]
    gn = SSM_GROUPS * SSM_STATE
    heads_per_group = n_heads // SSM_GROUPS
    p = SSM_HEAD_DIM

    @pl.when(c == 0)
    def _():
        xpad_ref[0:SUBLANES, :] = jnp.zeros((SUBLANES, conv_dim), F32)
        xpad_ref[SUBLANES - (CONV_W - 1):SUBLANES, :] = conv0_ref[...]
        hn_ref[...] = h0_ref[...]

    x_raw = xbc_ref[...]
    xpad_ref[SUBLANES:SUBLANES + q, :] = x_raw
    cw = cw_ref[...]
    acc = cb_ref[...] + x_raw * cw[CONV_W - 1:CONV_W]
    for j in range(CONV_W - 1):
        off = SUBLANES - (CONV_W - 1) + j
        acc = acc + xpad_ref[off:off + q, :] * cw[j:j + 1]
    xpad_ref[SUBLANES - (CONV_W - 1):SUBLANES, :] = x_raw[q - (CONV_W - 1):q, :]
    xc = _silu(acc)
    xs = xc[:, :d_inner]
    bm = xc[:, d_inner:d_inner + gn].astype(BF16)
    cm = xc[:, d_inner + gn:d_inner + 2 * gn].astype(BF16)

    t = dt_ref[...] + dtb_ref[...]
    dtv = jnp.maximum(t, 0.0) + jnp.log(1.0 + jnp.exp(-jnp.abs(t)))
    rows = lax.broadcasted_iota(I32, (q, q), 0)
    cols = lax.broadcasted_iota(I32, (q, q), 1)
    if valid_len < q:
        dtv = jnp.where(lax.broadcasted_iota(I32, dtv.shape, 0) < valid_len, dtv, 0.0)
    a = dtv * aneg_ref[...]
    causal = cols <= rows
    tri = jnp.where(causal, 1.0, 0.0).astype(BF16)
    a_hi, a_mid, a_lo = _split3(a)
    a_cum = _dot(tri, a_hi) + _dot(tri, a_mid) + _dot(tri, a_lo)
    a_cum_t = a_cum.T
    a_tot = a_cum[q - 1:q, :]

    ys = []
    xws = []
    for g in range(SSM_GROUPS):
        cg = cm[:, g * SSM_STATE:(g + 1) * SSM_STATE]
        bg = bm[:, g * SSM_STATE:(g + 1) * SSM_STATE]
        scores = _dot_nt(cg, bg)
        for r in range(heads_per_group):
            h = g * heads_per_group + r
            col = a_cum[:, h:h + 1]
            row = a_cum_t[h:h + 1, :]
            decay_ls = jnp.where(causal, jnp.exp(col - row), 0.0)
            w_diag = (scores * decay_ls).astype(BF16)
            xs_h = xs[:, h * p:(h + 1) * p]
            xdt = xs_h * dtv[:, h:h + 1]
            y_diag = _dot(w_diag, xdt.astype(BF16))
            y_off = _dot_nt(cg, hn_ref[h].astype(BF16)) * jnp.exp(col)
            ys.append(y_diag + y_off)
            xws.append(xdt * jnp.exp(a_tot[:, h:h + 1] - col))
    y = jnp.concatenate(ys, axis=1) + dsk_ref[...] * xs
    xw = jnp.concatenate(xws, axis=1)

    heads_per_tile = LANES // p
    for j in range(d_inner // LANES):
        xw_t = xw[:, j * LANES:(j + 1) * LANES].T.astype(BF16)
        for hh in range(heads_per_tile):
            h = j * heads_per_tile + hh
            g = h // heads_per_group
            bg = bm[:, g * SSM_STATE:(g + 1) * SSM_STATE]
            st = _dot(xw_t[hh * p:(hh + 1) * p, :], bg)
            hn_ref[h] = jnp.exp(a_tot[:, h:h + 1]) * hn_ref[h] + st

    zz = z_ref[...]
    yg = y * _silu(zz)
    gsz = d_inner // SSM_GROUPS
    outs = []
    for g in range(SSM_GROUPS):
        blk = yg[:, g * gsz:(g + 1) * gsz]
        ms = jnp.mean(blk * blk, axis=-1, keepdims=True)
        outs.append(blk * lax.rsqrt(ms + RMS_EPS))
    y_ref[...] = jnp.concatenate(outs, axis=1) * nw_ref[...]


def _ssd(proj3, conv0, h0, cw, cb, dtb, aneg, dsk, nw, valid_len, name):
    b, length, _ = proj3.shape
    n_heads, p, n = h0.shape[1], h0.shape[2], h0.shape[3]
    d_inner = n_heads * p
    conv_dim = cw.shape[1]
    q = SSD_CHUNK
    z_blk = conv_dim // d_inner
    dt_blk = (proj3.shape[2] - LANES) // LANES
    fix2 = lambda i, c: (0, 0)
    return pl.pallas_call(
        functools.partial(_ssd_body, q=q, valid_len=valid_len, n_heads=n_heads, d_inner=d_inner),
        grid=(b, length // q),
        in_specs=[pl.BlockSpec((None, q, conv_dim), lambda i, c: (i, c, 0)),
                  pl.BlockSpec((None, q, d_inner), lambda i, c: (i, c, z_blk)),
                  pl.BlockSpec((None, q, LANES), lambda i, c: (i, c, dt_blk)),
                  pl.BlockSpec((None, CONV_W - 1, conv_dim), lambda i, c: (i, 0, 0)),
                  pl.BlockSpec((None, n_heads, p, n), lambda i, c: (i, 0, 0, 0)),
                  pl.BlockSpec((CONV_W, conv_dim), fix2),
                  pl.BlockSpec((1, conv_dim), fix2),
                  pl.BlockSpec((1, LANES), fix2),
                  pl.BlockSpec((1, LANES), fix2),
                  pl.BlockSpec((1, d_inner), fix2),
                  pl.BlockSpec((1, d_inner), fix2)],
        out_specs=[pl.BlockSpec((None, q, d_inner), lambda i, c: (i, c, 0)),
                   pl.BlockSpec((None, n_heads, p, n), lambda i, c: (i, 0, 0, 0))],
        out_shape=[jax.ShapeDtypeStruct((b, length, d_inner), F32),
                   jax.ShapeDtypeStruct((b, n_heads, p, n), F32)],
        scratch_shapes=[pltpu.VMEM((SUBLANES + q, conv_dim), F32)],
        compiler_params=_params("parallel", "arbitrary"),
        name=name,
    )(proj3, proj3, proj3, conv0, h0, cw, cb, dtb, aneg, dsk, nw)


LOG2E = math.log2(math.e)
SB_SEGMENTS = SUBLANES


def _sb_weights(z, carry, valid, f_ref):
    tk, tq = z.shape
    seglen = tk // SB_SEGMENTS
    fail = 1.0 / (1.0 + jnp.exp2(z))
    if valid is not None:
        fail = jnp.where(valid, fail, 1.0)
    f_ref[...] = fail
    seg_tot = jnp.ones((SB_SEGMENTS, tq), F32)
    for v in range(seglen):
        seg_tot = seg_tot * f_ref[v * SB_SEGMENTS:(v + 1) * SB_SEGMENTS, :]
    offs = [None] * SB_SEGMENTS
    tot = carry
    for s in reversed(range(SB_SEGMENTS)):
        offs[s] = tot
        tot = tot * seg_tot[s:s + 1, :]
    run = jnp.concatenate(offs, axis=0)
    rows = [None] * seglen
    for v in reversed(range(seglen)):
        nxt = run * f_ref[v * SB_SEGMENTS:(v + 1) * SB_SEGMENTS, :]
        rows[v] = run - nxt
        run = nxt
    return jnp.concatenate(rows, axis=0), tot


def _sb_prompt_body(bias_ref, q_ref, k_ref, v_ref, o_ref, kp_ref, vt_ref, f_ref, z_ref, w_ref, acc_ref,
                    *, tq, hb, d):
    i = pl.program_id(2)
    h0 = pl.program_id(1) * hb
    seglen = tq // SB_SEGMENTS
    nkb = kp_ref.shape[0]
    r = lax.broadcasted_iota(I32, (tq, tq), 0)
    key_pos = (r % SB_SEGMENTS) * seglen + r // SB_SEGMENTS
    strictly_before = key_pos < lax.broadcasted_iota(I32, (tq, tq), 1)

    @pl.when(i == 0)
    def _():
        def permute(kb, c):
            base = pl.multiple_of(kb * tq, tq)
            kp = [k_ref[pl.ds(base + v, SB_SEGMENTS, stride=seglen), :] for v in range(seglen)]
            vp = [v_ref[pl.ds(base + v, SB_SEGMENTS, stride=seglen), :] for v in range(seglen)]
            kp_ref[kb] = jnp.concatenate(kp, axis=0).astype(BF16)
            vt_ref[kb] = jnp.concatenate(vp, axis=0).T.astype(BF16)
            return c

        lax.fori_loop(0, nkb, permute, 0)

    qt_all = (q_ref[...] * (d ** -0.5 * LOG2E)).T
    head_of_row = lax.broadcasted_iota(I32, qt_all.shape, 0) // d
    qts = [jnp.where(head_of_row == hh, qt_all, 0.0).astype(BF16) for hh in range(hb)]

    def logits(hh, kb):
        return _dot(kp_ref[kb], qts[hh]) + bias_ref[h0 + hh] * LOG2E

    def values(hh, kb):
        return vt_ref[kb, hh * d:(hh + 1) * d, :]

    carries = []
    for hh in range(hb):
        w, carry = _sb_weights(logits(hh, i), jnp.ones((1, tq), F32), strictly_before, f_ref.at[hh])
        w_ref[hh] = w.astype(BF16)
        z_ref[hh] = logits(hh, jnp.maximum(i - 1, 0))
        acc_ref[hh] = jnp.zeros((d, tq), F32)
        carries.append(carry)

    def step(it, carries):
        kb = i - 1 - it
        out = []
        for hh in range(hb):
            acc_ref[hh] += _dot(values(hh, kb + 1), w_ref[hh])
            z = z_ref[hh]
            z_ref[hh] = logits(hh, jnp.maximum(kb - 1, 0))
            w, carry = _sb_weights(z, carries[hh], None, f_ref.at[hh])
            w_ref[hh] = w.astype(BF16)
            out.append(carry)
        return tuple(out)

    lax.fori_loop(0, i, step, tuple(carries))
    out_t = [acc_ref[hh] + _dot(values(hh, 0), w_ref[hh]) for hh in range(hb)]
    o_ref[...] = jnp.concatenate(out_t, axis=0).T


def _sb_prompt(bias, proj3, q_col, k_col, v_col, n_heads, d, tq, name):
    b, length, _ = proj3.shape
    hb = LANES // d
    nkb = length // tq
    qc, kc, vc = q_col // LANES, k_col // LANES, v_col // LANES
    return pl.pallas_call(
        functools.partial(_sb_prompt_body, tq=tq, hb=hb, d=d),
        grid=(b, n_heads // hb, nkb),
        in_specs=[pl.BlockSpec(memory_space=pltpu.SMEM),
                  pl.BlockSpec((None, tq, LANES), lambda bi, hi, i: (bi, i, qc + hi)),
                  pl.BlockSpec((None, length, LANES), lambda bi, hi, i: (bi, 0, kc + hi)),
                  pl.BlockSpec((None, length, LANES), lambda bi, hi, i: (bi, 0, vc + hi))],
        out_specs=pl.BlockSpec((None, tq, LANES), lambda bi, hi, i: (bi, i, hi)),
        out_shape=jax.ShapeDtypeStruct((b, length, n_heads * d), F32),
        scratch_shapes=[pltpu.VMEM((nkb, tq, LANES), BF16), pltpu.VMEM((nkb, LANES, tq), BF16),
                        pltpu.VMEM((hb, tq, tq), F32), pltpu.VMEM((hb, tq, tq), F32),
                        pltpu.VMEM((hb, tq, tq), BF16), pltpu.VMEM((hb, d, tq), F32)],
        compiler_params=_params("parallel", "parallel", "arbitrary"),
        name=name,
    )(bias, proj3, proj3, proj3)


def _sb_sample_body(pt_ref, qbd_ref, bias_ref, knew_ref, vnew_ref, *refs, pages_per_step, n_new):
    k_refs = refs[:pages_per_step]
    v_refs = refs[pages_per_step:2 * pages_per_step]
    o_ref = refs[2 * pages_per_step]
    carry_ref = refs[2 * pages_per_step + 1]
    j = pl.program_id(1)
    qbd = qbd_ref[...]
    bias = bias_ref[...]
    nr, hd = qbd.shape
    tk = knew_ref.shape[-1]
    kk = lax.broadcasted_iota(I32, (tk, tk), 0)
    ss = lax.broadcasted_iota(I32, (tk, tk), 1)
    later = jnp.where(kk > ss, 1.0, 0.0).astype(BF16)

    def blocks(kts, vts, valid):
        n = len(kts)
        kcat = jnp.concatenate([kt.reshape(hd, tk).astype(BF16) for kt in kts], axis=1)
        vcat = jnp.concatenate([vt.reshape(hd, tk).astype(BF16) for vt in vts], axis=1)
        z = _dot(qbd, kcat) + bias
        sp = jnp.maximum(z, 0.0) + jnp.log(1.0 + jnp.exp(-jnp.abs(z)))
        log_fail = -sp
        log_beta = z - sp
        if valid is not None:
            log_fail = jnp.where(valid, log_fail, 0.0)
        lf_hi = log_fail.astype(BF16)
        lf_lo = (log_fail - lf_hi.astype(F32)).astype(BF16)
        stack = lambda a: jnp.concatenate([a[:, p * tk:(p + 1) * tk] for p in range(n)], axis=0)
        inner = _dot(stack(lf_hi), later) + _dot(stack(lf_lo), later)
        carry = carry_ref[...]
        tails = []
        for p in range(n):
            tails.append(inner[p * nr:(p + 1) * nr] + carry)
            carry = carry + jnp.sum(log_fail[:, p * tk:(p + 1) * tk], axis=-1, keepdims=True)
        carry_ref[...] = carry
        w = jnp.exp(log_beta + jnp.concatenate(tails, axis=1))
        if valid is not None:
            w = jnp.where(valid, w, 0.0)
        o_ref[...] += _dot_nt(w.astype(BF16), vcat)

    @pl.when(j == 0)
    def _():
        carry_ref[...] = jnp.zeros_like(carry_ref)
        o_ref[...] = jnp.zeros_like(o_ref)
        qi = lax.broadcasted_iota(I32, (nr, tk), 0) % n_new
        jn = lax.broadcasted_iota(I32, (nr, tk), 1)
        blocks([knew_ref[...]], [vnew_ref[...]], jn < qi)

    blocks([k_ref[...] for k_ref in k_refs], [v_ref[...] for v_ref in v_refs], None)


def _sb_sample(page_table, qbd, bias_rows, knew_t, vnew_t, cache_kt, cache_vt, pages_per_step, n_new, name):
    n_seq, n_pages = page_table.shape
    nr, hd = qbd.shape[1], qbd.shape[2]
    n_heads, d, page = cache_kt.shape[1], cache_kt.shape[2], cache_kt.shape[3]
    steps = n_pages // pages_per_step

    def page_spec(p):
        return pl.BlockSpec((None, n_heads, d, page),
                            lambda s, j, pt: (pt[s, n_pages - 1 - (j * pages_per_step + p)], 0, 0, 0))

    seq3 = lambda s, j, pt: (s, 0, 0)
    seq4 = lambda s, j, pt: (s, 0, 0, 0)
    grid_spec = pltpu.PrefetchScalarGridSpec(
        num_scalar_prefetch=1,
        grid=(n_seq, steps),
        in_specs=[pl.BlockSpec((None, nr, hd), seq3),
                  pl.BlockSpec((nr, 1), lambda s, j, pt: (0, 0)),
                  pl.BlockSpec((None, n_heads, d, page), seq4),
                  pl.BlockSpec((None, n_heads, d, page), seq4)]
                 + [page_spec(p) for p in range(pages_per_step)] * 2,
        out_specs=pl.BlockSpec((None, nr, hd), seq3),
        scratch_shapes=[pltpu.VMEM((nr, 1), F32)],
    )
    return pl.pallas_call(
        functools.partial(_sb_sample_body, pages_per_step=pages_per_step, n_new=n_new),
        grid_spec=grid_spec,
        out_shape=jax.ShapeDtypeStruct((n_seq, nr, hd), F32),
        compiler_params=_params("parallel", "arbitrary"),
        name=name,
    )(page_table, qbd, bias_rows, knew_t, vnew_t, *([cache_kt] * pages_per_step), *([cache_vt] * pages_per_step))


DIL_BLOCK = 128


def _dil_group(dil, j, q_ref, kc_ref, kp_ref, vc_ref, vp_ref, og_ref, lg_ref, s_ref, p_ref, vv1_ref, vv2_ref,
               den_ref, lse_ref, d, scale):
    t = DIL_BLOCK
    rows = q_ref.shape[0]
    span = t * dil
    n_units = rows // t
    first_slot = lax.broadcasted_iota(I32, (t, LANES), 1) < d
    u = lax.broadcasted_iota(I32, (2 * t, 2 * t), 0) % t
    c = lax.broadcasted_iota(I32, (2 * t, 2 * t), 1)
    band_prev = jnp.logical_and(c < t, c >= u)
    band_cur = jnp.logical_and(c >= t, c - t <= u)

    def idx(s0):
        if dil == 1:
            return pl.ds(pl.multiple_of(s0, t), t)
        return pl.ds(s0, t, stride=dil)

    def start_of(n):
        return n % dil + (n // dil) * span

    def stage1(n):
        start = start_of(n)
        q2 = q_ref[idx(start), :] * scale
        below = n % dil + rows - span
        k_prev, v_prev = kp_ref[idx(below), :], vp_ref[idx(below), :]
        if span < rows:
            inside = n // dil > 0
            earlier = jnp.maximum(start - span, 0)
            k_prev = jnp.where(inside, kc_ref[idx(earlier), :], k_prev)
            v_prev = jnp.where(inside, vc_ref[idx(earlier), :], v_prev)
        qq = jnp.concatenate([jnp.where(first_slot, q2, 0.0), jnp.where(first_slot, 0.0, q2)], axis=0)
        kk = jnp.concatenate([k_prev, kc_ref[idx(start), :]], axis=0).astype(BF16)
        vv1_ref[...] = jnp.concatenate([v_prev, vc_ref[idx(start), :]], axis=0).astype(BF16)
        s_ref[...] = _dot_nt(qq.astype(BF16), kk)

    def stage2(n):
        prev_ok = jnp.logical_or(n // dil > 0, j > 0)
        mask = jnp.logical_or(band_cur, jnp.logical_and(band_prev, prev_ok))
        s = jnp.where(mask, s_ref[...], -jnp.inf)
        mx = jnp.max(s, axis=-1, keepdims=True)
        p = jnp.exp(s - mx)
        den = jnp.sum(p, axis=-1, keepdims=True)
        p_ref[...] = p.astype(BF16)
        den_ref[...] = den
        lse_ref[...] = mx + jnp.log(den)
        vv2_ref[...] = vv1_ref[...]

    def stage3(n):
        start = start_of(n)
        on = _dot(p_ref[...], vv2_ref[...]) / den_ref[...]
        lse = jnp.broadcast_to(lse_ref[...], on.shape)
        og_ref[idx(start), :] = jnp.where(first_slot, on[:t], on[t:])
        lg_ref[idx(start), :] = jnp.where(first_slot, lse[:t], lse[t:])

    one = jnp.int32(1)
    stage1(0 * one)
    stage2(0 * one)
    stage1(one)

    def trip(n, carry):
        stage3(n - 2)
        stage2(n - 1)
        stage1(n)
        return carry

    lax.fori_loop(2, n_units, trip, 0)
    stage3((n_units - 2) * one)
    stage2((n_units - 1) * one)
    stage3((n_units - 1) * one)


def _dil_prompt_body(*refs, dilations, d, scale):
    n_g = len(dilations)
    ins = refs[:5 * n_g]
    o_ref = refs[5 * n_g]
    og_refs = refs[5 * n_g + 1:5 * n_g + 1 + n_g]
    lg_refs = refs[5 * n_g + 1 + n_g:5 * n_g + 1 + 2 * n_g]
    stage_refs = refs[5 * n_g + 1 + 2 * n_g:]
    j = pl.program_id(2)

    for g, dil in enumerate(dilations):
        _dil_group(dil, j, *ins[5 * g:5 * g + 5], og_refs[g], lg_refs[g], *stage_refs, d, scale)

    ls_ = [lg_ref[...] for lg_ref in lg_refs]
    mx = functools.reduce(jnp.maximum, ls_)
    es = [jnp.exp(l - mx) for l in ls_]
    num = functools.reduce(lambda a, b: a + b, [e * og_ref[...] for e, og_ref in zip(es, og_refs)])
    o_ref[...] = num / functools.reduce(lambda a, b: a + b, es)


def _dil_prompt(qkv3, dilations, n_slots, d, name):
    b, length, _ = qkv3.shape
    n_g = len(dilations)
    rows = DIL_BLOCK * max(dilations)
    sb = LANES // d
    gw = n_g * n_slots * d
    cur = lambda bi, si, j, c0: (bi, j, c0 + si)
    prev = lambda bi, si, j, c0: (bi, jnp.maximum(j - 1, 0), c0 + si)
    in_specs = []
    for g in range(n_g):
        qc, kc, vc = (g * n_slots * d) // LANES, (gw + g * n_slots * d) // LANES, (2 * gw + g * n_slots * d) // LANES
        for fn, c0 in ((cur, qc), (cur, kc), (prev, kc), (cur, vc), (prev, vc)):
            in_specs.append(pl.BlockSpec((None, rows, LANES), functools.partial(fn, c0=c0)))
    return pl.pallas_call(
        functools.partial(_dil_prompt_body, dilations=tuple(dilations), d=d, scale=d ** -0.5),
        grid=(b, n_slots // sb, length // rows),
        in_specs=in_specs,
        out_specs=pl.BlockSpec((None, rows, LANES), lambda bi, si, j: (bi, j, si)),
        out_shape=jax.ShapeDtypeStruct((b, length, n_slots * d), F32),
        scratch_shapes=[pltpu.VMEM((rows, LANES), F32)] * (2 * n_g)
                       + [pltpu.VMEM((2 * DIL_BLOCK, 2 * DIL_BLOCK), F32), pltpu.VMEM((2 * DIL_BLOCK, 2 * DIL_BLOCK), BF16),
                          pltpu.VMEM((2 * DIL_BLOCK, LANES), BF16), pltpu.VMEM((2 * DIL_BLOCK, LANES), BF16),
                          pltpu.VMEM((2 * DIL_BLOCK, 1), F32), pltpu.VMEM((2 * DIL_BLOCK, 1), F32)],
        compiler_params=_params("parallel", "parallel", "arbitrary"),
        name=name,
    )(*([qkv3] * len(in_specs)))


def _dil_sample_body(qbd_ref, kbuf_ref, vbuf_ref, knew_ref, vnew_ref, o_ref, l_ref, kout_ref, vout_ref,
                     *, dil, n_new):
    n_slots, d, w = kbuf_ref.shape
    hd = n_slots * d
    nr = qbd_ref.shape[0]
    pad = knew_ref.shape[1]
    kb = kbuf_ref[...].reshape(hd, w)
    vb = vbuf_ref[...].reshape(hd, w)
    kn = knew_ref[...]
    vn = vnew_ref[...]
    tail_lane = lax.broadcasted_iota(I32, (hd, pad), 1) >= pad - n_new
    for src, new, out_ref in ((kb, kn, kout_ref), (vb, vn, vout_ref)):
        slid = pltpu.roll(src, w - n_new, 1)
        last = jnp.where(tail_lane, pltpu.roll(new, pad - n_new, 1), slid[:, w - pad:])
        if w > pad:
            out_ref[:, :, 0:w - pad] = slid[:, 0:w - pad].reshape(n_slots, d, w - pad)
        out_ref[:, :, w - pad:w] = last.reshape(n_slots, d, pad)

    qbd = qbd_ref[...]
    qi = lax.broadcasted_iota(I32, (nr, w), 0) % n_new
    diff = w + qi - lax.broadcasted_iota(I32, (nr, w), 1)
    ok_buf = jnp.logical_and(diff % dil == 0, diff <= dil * DIL_BLOCK)
    qi_n = lax.broadcasted_iota(I32, (nr, pad), 0) % n_new
    jn = lax.broadcasted_iota(I32, (nr, pad), 1)
    diff_n = qi_n - jn
    ok_new = jnp.logical_and(jnp.logical_and(diff_n >= 0, diff_n % dil == 0), jn < n_new)
    s_buf = jnp.where(ok_buf, _dot(qbd, kb.astype(BF16)), -jnp.inf)
    s_new = jnp.where(ok_new, _dot(qbd, kn.astype(BF16)), -jnp.inf)
    mx = jnp.maximum(jnp.max(s_buf, axis=-1, keepdims=True), jnp.max(s_new, axis=-1, keepdims=True))
    p_buf = jnp.exp(s_buf - mx)
    p_new = jnp.exp(s_new - mx)
    den = jnp.sum(p_buf, axis=-1, keepdims=True) + jnp.sum(p_new, axis=-1, keepdims=True)
    acc = _dot_nt(p_buf.astype(BF16), vb.astype(BF16)) + _dot_nt(p_new.astype(BF16), vn.astype(BF16))
    o_ref[...] = acc / den
    l_ref[...] = jnp.broadcast_to(mx + jnp.log(den), l_ref.shape)


def _dil_sample(qbd, kbuf_t, vbuf_t, knew_t, vnew_t, dil, n_new, name):
    n_seq, n_slots, d, w = kbuf_t.shape
    nr, hd = qbd.shape[1], qbd.shape[2]
    pad = knew_t.shape[2]
    s3 = lambda s: (s, 0, 0)
    s4 = lambda s: (s, 0, 0, 0)
    buf_spec = pl.BlockSpec((None, n_slots, d, w), s4)
    return pl.pallas_call(
        functools.partial(_dil_sample_body, dil=dil, n_new=n_new),
        grid=(n_seq,),
        in_specs=[pl.BlockSpec((None, nr, hd), s3), buf_spec, buf_spec,
                  pl.BlockSpec((None, hd, pad), s3), pl.BlockSpec((None, hd, pad), s3)],
        out_specs=[pl.BlockSpec((None, nr, hd), s3), pl.BlockSpec((None, nr, LANES), s3), buf_spec, buf_spec],
        out_shape=[jax.ShapeDtypeStruct((n_seq, nr, hd), F32), jax.ShapeDtypeStruct((n_seq, nr, LANES), F32),
                   jax.ShapeDtypeStruct(kbuf_t.shape, F32), jax.ShapeDtypeStruct(kbuf_t.shape, F32)],
        compiler_params=_params("parallel"),
        name=name,
    )(qbd, kbuf_t, vbuf_t, knew_t, vnew_t)


def _router_body(x_ref, w_ref, idx_ref, gate_ref, *, n_experts):
    xh, xm, _ = _split3(x_ref[...])
    wh, wm, _ = _split3(w_ref[...])
    logits = _dot(xh, wh) + _dot(xm, wh) + _dot(xh, wm)
    lane = lax.broadcasted_iota(I32, logits.shape, 1).astype(F32)
    logits = jnp.where(lane < n_experts, logits, -jnp.inf)
    m1 = jnp.max(logits, axis=-1, keepdims=True)
    i1 = jnp.min(jnp.where(logits == m1, lane, float(LANES)), axis=-1, keepdims=True)
    rest = jnp.where(lane == i1, -jnp.inf, logits)
    m2 = jnp.max(rest, axis=-1, keepdims=True)
    i2 = jnp.min(jnp.where(rest == m2, lane, float(LANES)), axis=-1, keepdims=True)
    e = jnp.exp(m2 - m1)
    g1 = 1.0 / (1.0 + e)
    idx_ref[...] = jnp.where(lane == 0.0, i1, jnp.where(lane == 1.0, i2, 0.0)).astype(I32)
    gate_ref[...] = jnp.where(lane == 0.0, g1, jnp.where(lane == 1.0, e * g1, 0.0))


def _router(x, w_pad, n_experts, tm, name):
    m, d = x.shape
    return pl.pallas_call(
        functools.partial(_router_body, n_experts=n_experts),
        grid=(m // tm,),
        in_specs=[pl.BlockSpec((tm, d), lambda i: (i, 0)), pl.BlockSpec((d, LANES), lambda i: (0, 0))],
        out_specs=[pl.BlockSpec((tm, LANES), lambda i: (i, 0))] * 2,
        out_shape=[jax.ShapeDtypeStruct((m, LANES), I32), jax.ShapeDtypeStruct((m, LANES), F32)],
        compiler_params=_params("parallel"),
        name=name,
    )(x, w_pad)


DMA_ISSUE_UNROLL = 8


def _row_copy(src_hbm, row, dst, r, sem):
    return pltpu.make_async_copy(src_hbm.at[pl.ds(row, 1)], dst.at[pl.ds(r, 1)], sem)


def _gather_body(src_ref, x_hbm, o_ref, buf, sem, *, rows):
    i = pl.program_id(0)

    def start(step):
        slot = step % 2

        def some(m, c):
            for k in range(DMA_ISSUE_UNROLL):
                r = m * DMA_ISSUE_UNROLL + k
                _row_copy(x_hbm, src_ref[step * rows + r], buf.at[slot], r, sem.at[slot]).start(priority=k % 2)
            return c

        lax.fori_loop(0, rows // DMA_ISSUE_UNROLL, some, 0)

    @pl.when(i == 0)
    def _():
        start(i)

    @pl.when(i + 1 < pl.num_programs(0))
    def _():
        start(i + 1)

    slot = i % 2
    pltpu.make_async_copy(x_hbm.at[pl.ds(0, rows)], buf.at[slot], sem.at[slot]).wait()
    o_ref[...] = buf[slot].astype(BF16)


def _gather_rows(src, x, rows, name):
    n = src.shape[0]
    d = x.shape[1]
    grid_spec = pltpu.PrefetchScalarGridSpec(
        num_scalar_prefetch=1,
        grid=(n // rows,),
        in_specs=[pl.BlockSpec(memory_space=pl.ANY)],
        out_specs=pl.BlockSpec((rows, d), lambda i, s: (i, 0)),
        scratch_shapes=[pltpu.VMEM((2, rows, d), x.dtype), pltpu.SemaphoreType.DMA((2,))],
    )
    return pl.pallas_call(
        functools.partial(_gather_body, rows=rows),
        grid_spec=grid_spec,
        out_shape=jax.ShapeDtypeStruct((n, d), BF16),
        compiler_params=_params("arbitrary"),
        name=name,
    )(src, x)


def _experts_body(te_ref, tb_ref, nu_ref, x_ref, w1_ref, w3_ref, w2_ref, o_ref, acc_ref):
    i = pl.program_id(0)
    f = pl.program_id(1)

    @pl.when(i < nu_ref[0])
    def _():
        @pl.when(f == 0)
        def _():
            acc_ref[...] = jnp.zeros_like(acc_ref)

        x = x_ref[...]
        hid = _silu(_dot(x, w1_ref[...].astype(BF16))) * _dot(x, w3_ref[...].astype(BF16))
        acc_ref[...] += _dot(hid.astype(BF16), w2_ref[...].astype(BF16))

        @pl.when(f == pl.num_programs(1) - 1)
        def _():
            o_ref[...] = acc_ref[...]

    @pl.when(jnp.logical_and(i >= nu_ref[0], f == pl.num_programs(1) - 1))
    def _():
        o_ref[...] = jnp.zeros_like(o_ref)


def _experts(tile_expert, tile_block, n_used, xs, w1, w3, w2, tm, tf, name):
    n_tiles = tile_expert.shape[0]
    d = xs.shape[1]
    ff = w1.shape[2]
    nf = ff // tf

    def fcol(i, f, nu):
        return jnp.where(i < nu[0], f, nf - 1)

    grid_spec = pltpu.PrefetchScalarGridSpec(
        num_scalar_prefetch=3,
        grid=(n_tiles, nf),
        in_specs=[pl.BlockSpec((tm, d), lambda i, f, te, tb, nu: (tb[i], 0)),
                  pl.BlockSpec((None, d, tf), lambda i, f, te, tb, nu: (te[i], 0, fcol(i, f, nu))),
                  pl.BlockSpec((None, d, tf), lambda i, f, te, tb, nu: (te[i], 0, fcol(i, f, nu))),
                  pl.BlockSpec((None, tf, d), lambda i, f, te, tb, nu: (te[i], fcol(i, f, nu), 0))],
        out_specs=pl.BlockSpec((tm, d), lambda i, f, te, tb, nu: (i, 0)),
        scratch_shapes=[pltpu.VMEM((tm, d), F32)],
    )
    return pl.pallas_call(
        _experts_body,
        grid_spec=grid_spec,
        out_shape=jax.ShapeDtypeStruct((n_tiles * tm, d), F32),
        compiler_params=_params("arbitrary", "arbitrary"),
        name=name,
    )(tile_expert, tile_block, n_used, xs, w1, w3, w2)


def _combine_ln_body(slot_ref, y_hbm, x_ref, gate_ref, g_ref, b_ref, o_ref, buf1, buf2, sem, *, rows, t0, alpha):
    base = (pl.program_id(0) * rows + t0) * TOP_K

    def start(m, c):
        for k in range(DMA_ISSUE_UNROLL):
            r = m * DMA_ISSUE_UNROLL + k
            _row_copy(y_hbm, slot_ref[base + TOP_K * r], buf1, r, sem).start(priority=0)
            _row_copy(y_hbm, slot_ref[base + TOP_K * r + 1], buf2, r, sem).start(priority=1)
        return c

    lax.fori_loop(0, rows // DMA_ISSUE_UNROLL, start, 0)
    pltpu.make_async_copy(y_hbm.at[pl.ds(0, rows)], buf1, sem).wait()
    pltpu.make_async_copy(y_hbm.at[pl.ds(0, rows)], buf2, sem).wait()
    gate = gate_ref[...]
    moe = gate[:, 0:1] * buf1[...] + gate[:, 1:2] * buf2[...]
    o_ref[...] = _layer_norm(alpha * x_ref[...] + moe, g_ref[...], b_ref[...])


def _combine_ln(slot, ys, x, gate, g, b, t0, alpha, rows, name):
    m, d = x.shape
    gb = t0 // rows
    grid_spec = pltpu.PrefetchScalarGridSpec(
        num_scalar_prefetch=1,
        grid=(m // rows,),
        in_specs=[pl.BlockSpec(memory_space=pl.ANY),
                  pl.BlockSpec((rows, d), lambda i, s: (i, 0)),
                  pl.BlockSpec((rows, LANES), lambda i, s: (i + gb, 0)),
                  pl.BlockSpec((1, d), lambda i, s: (0, 0)),
                  pl.BlockSpec((1, d), lambda i, s: (0, 0))],
        out_specs=pl.BlockSpec((rows, d), lambda i, s: (i, 0)),
        scratch_shapes=[pltpu.VMEM((rows, d), F32), pltpu.VMEM((rows, d), F32), pltpu.SemaphoreType.DMA(())],
    )
    return pl.pallas_call(
        functools.partial(_combine_ln_body, rows=rows, t0=t0, alpha=alpha),
        grid_spec=grid_spec,
        out_shape=jax.ShapeDtypeStruct((m, d), F32),
        compiler_params=_params("arbitrary"),
        name=name,
    )(slot, ys, x, gate, g, b)


def _moe_ln(xp, xs_, router_w, w1, w3, w2, g, b, alpha):
    n_experts = router_w.shape[1]
    d = xp.shape[1]
    tp, ts = xp.shape[0], xs_.shape[0]
    x_all = jnp.concatenate([xp, xs_], axis=0)
    t_all = tp + ts
    rw = jnp.pad(router_w, ((0, 0), (0, LANES - n_experts)))
    idx, gate = _router(x_all, rw, n_experts, ts, "moe_router")

    tm = 1024
    n_assign = t_all * TOP_K
    e_flat = idx[:, :TOP_K].reshape(n_assign)
    onehot = (e_flat[:, None] == jnp.arange(n_experts, dtype=I32)[None, :]).astype(I32)
    csum = jnp.cumsum(onehot, axis=0)
    pos = jnp.take_along_axis(csum, e_flat[:, None], axis=1)[:, 0] - 1
    counts = csum[-1]
    padded = ((counts + tm - 1) // tm) * tm
    ends = jnp.cumsum(padded)
    offs = ends - padded
    slot = (offs[e_flat] + pos).astype(I32)
    n_tiles = (n_assign + n_experts * (tm - 1)) // tm
    src = jnp.zeros((n_tiles * tm,), I32).at[slot].set(jnp.arange(n_assign, dtype=I32) // TOP_K)
    n_used = (ends[-1] // tm).astype(I32)
    tiles = jnp.arange(n_tiles, dtype=I32)
    t_exp = jnp.minimum(jnp.sum((tiles[:, None] * tm >= ends[None, :]).astype(I32), axis=1), n_experts - 1)
    last = jnp.maximum(n_used - 1, 0)
    tile_expert = jnp.where(tiles < n_used, t_exp, t_exp[last])
    tile_block = jnp.where(tiles < n_used, tiles, last)

    x_sorted = _gather_rows(src, x_all, 256, "moe_gather")
    y_sorted = _experts(tile_expert, tile_block, n_used.reshape(1), x_sorted, w1, w3, w2, tm, 512, "moe_experts")
    yp = _combine_ln(slot, y_sorted, xp, gate, g, b, 0, alpha, 256, "moe_combine_prompt")
    ys = _combine_ln(slot, y_sorted, xs_, gate, g, b, tp, alpha, 256, "moe_combine_sample")
    return yp, ys


def _rope_tables(pos):
    half = ROT_DIM // 2
    inv = ROPE_THETA ** (-jnp.arange(0, ROT_DIM, 2, dtype=F32) / ROT_DIM)
    ang = pos.astype(F32)[:, None] * inv[None, :]
    cos, sin = jnp.cos(ang), jnp.sin(ang)
    n = pos.shape[0]
    ones = jnp.ones((n, HEAD_DIM - ROT_DIM), F32)
    zeros = jnp.zeros((n, HEAD_DIM - ROT_DIM), F32)
    zh = jnp.zeros((n, half), F32)
    reps = LANES // HEAD_DIM
    c = jnp.tile(jnp.concatenate([cos, cos, ones], axis=1), (1, reps))
    sa = jnp.tile(jnp.concatenate([-sin, zh, zeros], axis=1), (1, reps))
    sb = jnp.tile(jnp.concatenate([zh, sin, zeros], axis=1), (1, reps))
    return c, sa, sb


def _block_diag_q(q, scale):
    n_seq, n_new, nh, d = q.shape
    eye = jnp.eye(nh, dtype=F32)
    out = jnp.einsum("sihd,hg->shigd", q * scale, eye)
    return out.reshape(n_seq, nh * n_new, nh * d).astype(BF16)


def _diag_blocks(o, n_new, nh, d):
    n_seq = o.shape[0]
    o5 = o.reshape(n_seq, nh, n_new, nh, d)
    sel = jnp.stack([o5[:, h, :, h, :] for h in range(nh)], axis=2)
    return sel.reshape(n_seq * n_new, nh * d)


def kernel(x_prompt, x_sample, cache_sb_k, cache_sb_v, page_table, state_ssm, state_conv, cache_c_w128_k, cache_c_w128_v, cache_c_w512_k, cache_c_w512_v, cache_c_w2048_k, cache_c_w2048_v, w_in_a, conv_w, conv_b, dt_bias, a_log, d_skip, ssm_norm_w, sb_bias, w_out_a, ln_a_mix_g, ln_a_mix_b, ffn_w1, ffn_w3, ffn_w2, ln_a_ffn_g, ln_a_ffn_b, w_in_c, w_out_c, ln_c_mix_g, ln_c_mix_b, router_w, moe_w1, moe_w3, moe_w2, ln_c_ffn_g, ln_c_ffn_b):
    n_pr, seq, d_model = x_prompt.shape
    n_dec, dec_seq, _ = x_sample.shape
    n_pages = page_table.shape[1]
    page = cache_sb_k.shape[2]
    past_len = n_pages * page
    ssm_heads, p, n_state = state_ssm.shape[2], state_ssm.shape[3], state_ssm.shape[4]
    d_inner = ssm_heads * p
    conv_dim = state_conv.shape[3]
    sb_heads = cache_sb_k.shape[3]
    sb_w = sb_heads * HEAD_DIM
    c_slots = cache_c_w128_k.shape[3]
    c_wd = c_slots * HEAD_DIM
    n_groups = len(C_GROUPS)
    depth = w_in_a.shape[0] + w_in_c.shape[0]
    alpha = (2 * depth) ** 0.25
    tp = n_pr * seq
    ts = n_dec * dec_seq
    win_k = (cache_c_w128_k, cache_c_w512_k, cache_c_w2048_k)
    win_v = (cache_c_w128_v, cache_c_w512_v, cache_c_w2048_v)
    row = lambda v: v.reshape(1, -1)

    yp = x_prompt.reshape(tp, d_model)
    ys = x_sample.reshape(ts, d_model)
    outs = {}

    i = 0
    w = w_in_a[i]
    c0, c1, c2 = d_inner, d_inner + conv_dim, d_inner + conv_dim + ssm_heads
    w_re = jnp.concatenate([w[:, c0:c1], w[:, :c0], w[:, c2:], w[:, c1:c2],
                            jnp.zeros((d_model, LANES - ssm_heads), F32)], axis=1).astype(BF16)
    o_q = conv_dim + d_inner
    proj_p = _matmul(yp, w_re, 512, "in_proj_a_prompt")
    proj_s = _matmul(ys, w_re, ts, "in_proj_a_sample")

    pad_l = LANES - ssm_heads
    ssd_par = (conv_w[i], row(conv_b[i]), row(jnp.pad(dt_bias[i], (0, pad_l))),
               row(jnp.pad(-jnp.exp(a_log[i]), (0, pad_l))), row(jnp.repeat(d_skip[i], p)), row(ssm_norm_w[i]))
    ya_p, ssm_p = _ssd(proj_p.reshape(n_pr, seq, -1), jnp.zeros((n_pr, CONV_W - 1, conv_dim), F32),
                       jnp.zeros((n_pr, ssm_heads, p, n_state), F32), *ssd_par, SSD_CHUNK, "ssd_prompt")
    proj_s3 = proj_s.reshape(n_dec, dec_seq, -1)
    proj_s_pad = jnp.pad(proj_s3, ((0, 0), (0, SSD_CHUNK - dec_seq), (0, 0)))
    ya_s, ssm_s = _ssd(proj_s_pad, state_conv[i], state_ssm[i], *ssd_par, dec_seq, "ssd_sample")
    ya_p = ya_p.reshape(tp, d_inner)
    ya_s = ya_s[:, :dec_seq].reshape(ts, d_inner)
    conv_p = proj_p.reshape(n_pr, seq, -1)[:, seq - (CONV_W - 1):, :conv_dim]
    conv_s = jnp.concatenate([state_conv[i], proj_s3[:, :, :conv_dim]], axis=1)[:, dec_seq:]

    k_p = proj_p[:, o_q + sb_w:o_q + 2 * sb_w].reshape(n_pr, seq, sb_heads, HEAD_DIM)
    v_p = proj_p[:, o_q + 2 * sb_w:o_q + 3 * sb_w].reshape(n_pr, seq, sb_heads, HEAD_DIM)
    o_p = _sb_prompt(sb_bias[i], proj_p.reshape(n_pr, seq, -1), o_q, o_q + sb_w, o_q + 2 * sb_w,
                     sb_heads, HEAD_DIM, 256, "sb_prompt").reshape(tp, sb_w)

    q_s = proj_s[:, o_q:o_q + sb_w].reshape(n_dec, dec_seq, sb_heads, HEAD_DIM)
    k_s = proj_s[:, o_q + sb_w:o_q + 2 * sb_w].reshape(n_dec, dec_seq, sb_w)
    v_s = proj_s[:, o_q + 2 * sb_w:o_q + 3 * sb_w].reshape(n_dec, dec_seq, sb_w)
    qbd = _block_diag_q(q_s, HEAD_DIM ** -0.5)
    bias_rows = jnp.repeat(sb_bias[i], dec_seq).reshape(sb_heads * dec_seq, 1)

    def keys_minor(a):
        return a.transpose(0, 2, 3, 1)

    pad_new = ((0, 0), (0, 0), (0, 0), (0, page - dec_seq))
    o_s = _sb_sample(page_table, qbd, bias_rows,
                     jnp.pad(keys_minor(k_s.reshape(n_dec, dec_seq, sb_heads, HEAD_DIM)), pad_new),
                     jnp.pad(keys_minor(v_s.reshape(n_dec, dec_seq, sb_heads, HEAD_DIM)), pad_new),
                     keys_minor(cache_sb_k[i]), keys_minor(cache_sb_v[i]), 16, dec_seq, "sb_sample")
    o_s = _diag_blocks(o_s, dec_seq, sb_heads, HEAD_DIM)

    wo = w_out_a[i].astype(BF16)
    ln_g, ln_b = row(ln_a_mix_g[i]), row(ln_a_mix_b[i])
    wo2 = [wo[:d_inner], wo[d_inner:]]
    yp = _proj_ln([ya_p, o_p], wo2, yp, ln_g, ln_b, alpha, 512, "out_proj_a_prompt")
    ys = _proj_ln([ya_s, o_s], wo2, ys, ln_g, ln_b, alpha, ts, "out_proj_a_sample")
    f1, f3, f2 = ffn_w1[i].astype(BF16), ffn_w3[i].astype(BF16), ffn_w2[i].astype(BF16)
    ln_g, ln_b = row(ln_a_ffn_g[i]), row(ln_a_ffn_b[i])
    d_ff = f1.shape[1]
    tf = d_ff // 2 if d_ff % (2 * LANES) == 0 else LANES
    yp = _ffn_ln(yp, f1, f3, f2, ln_g, ln_b, alpha, 512, tf, "ffn_prompt")
    ys = _ffn_ln(ys, f1, f3, f2, ln_g, ln_b, alpha, ts, tf, "ffn_sample")

    outs["sb_kp"] = k_p[None]
    outs["sb_vp"] = v_p[None]
    outs["sb_ks"] = k_s.reshape(1, n_dec, dec_seq, sb_heads, HEAD_DIM)
    outs["sb_vs"] = v_s.reshape(1, n_dec, dec_seq, sb_heads, HEAD_DIM)
    outs["ssm_p"], outs["ssm_s"], outs["conv_p"], outs["conv_s"] = ssm_p[None], ssm_s[None], conv_p[None], conv_s[None]

    wc = w_in_c[i].astype(BF16)
    gw = n_groups * c_wd
    cos_p, sa_p, sb_p = _rope_tables(jnp.arange(seq))
    cos_s, sa_s, sb_s = _rope_tables(past_len + jnp.arange(ts) % dec_seq)
    qkv_p = _matmul_rope(yp, wc, cos_p, sa_p, sb_p, 512, "in_proj_c_prompt")
    qkv_s = _matmul_rope(ys, wc, cos_s, sa_s, sb_s, ts, "in_proj_c_sample")

    scale = HEAD_DIM ** -0.5
    o_c_p = _dil_prompt(qkv_p.reshape(n_pr, seq, -1), [dil for _, dil in C_GROUPS], c_slots, HEAD_DIM,
                        "dilated_prompt").reshape(tp, c_wd)
    o_groups_s, l_groups_s = [], []
    for g, (win, dil) in enumerate(C_GROUPS):
        cq, ck, cv = g * c_wd, gw + g * c_wd, 2 * gw + g * c_wd
        keep = min(win, seq)
        last_rows = qkv_p.reshape(n_pr, seq, -1)[:, seq - keep:]
        outs[f"c_kp{g}"] = last_rows[:, :, ck:ck + c_wd].reshape(1, n_pr, keep, c_slots, HEAD_DIM)
        outs[f"c_vp{g}"] = last_rows[:, :, cv:cv + c_wd].reshape(1, n_pr, keep, c_slots, HEAD_DIM)

        qg_s = qkv_s[:, cq:cq + c_wd].reshape(n_dec, dec_seq, c_slots, HEAD_DIM)
        pad_keys = ((0, 0), (0, 0), (0, DIL_BLOCK - dec_seq))
        kn = jnp.pad(qkv_s[:, ck:ck + c_wd].reshape(n_dec, dec_seq, c_wd).transpose(0, 2, 1), pad_keys)
        vn = jnp.pad(qkv_s[:, cv:cv + c_wd].reshape(n_dec, dec_seq, c_wd).transpose(0, 2, 1), pad_keys)
        o_g, l_g, k_new, v_new = _dil_sample(_block_diag_q(qg_s, scale), keys_minor(win_k[g][i]),
                                             keys_minor(win_v[g][i]), kn, vn, dil, dec_seq,
                                             f"dilated_sample_{g}")
        o_groups_s.append(_diag_blocks(o_g, dec_seq, c_slots, HEAD_DIM))
        l_s = l_g[:, :, 0].reshape(n_dec, c_slots, dec_seq).transpose(0, 2, 1)
        l_groups_s.append(jnp.repeat(l_s.reshape(ts, c_slots), HEAD_DIM, axis=1))
        outs[f"c_ks{g}"] = k_new.transpose(0, 3, 1, 2)[None]
        outs[f"c_vs{g}"] = v_new.transpose(0, 3, 1, 2)[None]

    woc = w_out_c[i].astype(BF16)
    ln_g, ln_b = row(ln_c_mix_g[i]), row(ln_c_mix_b[i])
    yp = _proj_ln([o_c_p], [woc], yp, ln_g, ln_b, alpha, 512, "out_proj_c_prompt")
    ys = _merge_proj_ln(o_groups_s, l_groups_s, woc, ys, ln_g, ln_b, alpha, ts, "out_proj_c_sample")
    yp, ys = _moe_ln(yp, ys, router_w[i], moe_w1[i], moe_w3[i], moe_w2[i],
                     row(ln_c_ffn_g[i]), row(ln_c_ffn_b[i]), alpha)

    res = [yp.reshape(n_pr, seq, d_model), ys.reshape(n_dec, dec_seq, d_model),
           outs["sb_kp"], outs["sb_vp"], outs["sb_ks"], outs["sb_vs"],
           outs["ssm_p"], outs["ssm_s"], outs["conv_p"], outs["conv_s"]]
    for g in range(n_groups):
        res += [outs[f"c_kp{g}"], outs[f"c_vp{g}"], outs[f"c_ks{g}"], outs[f"c_vs{g}"]]
    return tuple(res)
```

```python
import functools
import math

import numpy as np
import jax
import jax.numpy as jnp
from jax import lax
from jax.experimental import pallas as pl
from jax.experimental.pallas import tpu as pltpu

F32 = jnp.float32
BF16 = jnp.bfloat16
I32 = jnp.int32

HEAD_DIM = 64
SSM_HEAD_DIM = 64
SSM_GROUPS = 2
SSM_STATE = 128
CONV_W = 4
SSD_CHUNK = 128
C_GROUPS = ((128, 1), (512, 4), (2048, 16))
ROT_DIM = HEAD_DIM // 4
ROPE_THETA = 500000.0
TOP_K = 2
LN_EPS = 1e-5
RMS_EPS = 1e-6

LANES = 128
SUBLANES = 8
VMEM_LIMIT_BYTES = 56 * 1024 * 1024


def _params(*semantics):
    return pltpu.CompilerParams(dimension_semantics=semantics, vmem_limit_bytes=VMEM_LIMIT_BYTES)


def _dot(a, b):
    return jnp.dot(a, b, preferred_element_type=F32)


def _dot_nt(a, b):
    return lax.dot_general(a, b, (((1,), (1,)), ((), ())), preferred_element_type=F32)


def _layer_norm(y, g, b):
    mu = jnp.mean(y, axis=-1, keepdims=True)
    d = y - mu
    var = jnp.mean(d * d, axis=-1, keepdims=True)
    return d * lax.rsqrt(var + LN_EPS) * g + b


def _silu(a):
    return a * jax.nn.sigmoid(a)


def _split3(a):
    hi = a.astype(BF16)
    r = a - hi.astype(F32)
    mid = r.astype(BF16)
    lo = (r - mid.astype(F32)).astype(BF16)
    return hi, mid, lo


def _mm_body(x_ref, w_ref, o_ref):
    o_ref[...] = _dot(x_ref[...].astype(BF16), w_ref[...])


def _matmul(x, w, tm, name):
    m, k = x.shape
    n = w.shape[1]
    return pl.pallas_call(
        _mm_body,
        grid=(m // tm,),
        in_specs=[pl.BlockSpec((tm, k), lambda i: (i, 0)), pl.BlockSpec((k, n), lambda i: (0, 0))],
        out_specs=pl.BlockSpec((tm, n), lambda i: (i, 0)),
        out_shape=jax.ShapeDtypeStruct((m, n), F32),
        compiler_params=_params("parallel"),
        name=name,
    )(x, w)


MXU_WIDTH = 2 * LANES


def _mm_rope_body(x_ref, w_ref, cos_ref, sa_ref, sb_ref, o_ref, xb_ref, *, n_rot_tiles):
    j = pl.program_id(1)

    @pl.when(j == 0)
    def _():
        xb_ref[...] = x_ref[...].astype(BF16)

    xb = xb_ref[...]

    @pl.when(j < n_rot_tiles)
    def _():
        cos = cos_ref[...]
        sa = sa_ref[...]
        sb = sb_ref[...]
        for c in range(o_ref.shape[1] // MXU_WIDTH):
            y = _dot(xb, w_ref[:, c * MXU_WIDTH:(c + 1) * MXU_WIDTH])
            for h in range(MXU_WIDTH // LANES):
                yc = y[:, h * LANES:(h + 1) * LANES]
                lo = c * MXU_WIDTH + h * LANES
                o_ref[:, lo:lo + LANES] = (
                    yc * cos + pltpu.roll(yc, LANES - ROT_DIM // 2, 1) * sa + pltpu.roll(yc, ROT_DIM // 2, 1) * sb)

    @pl.when(j >= n_rot_tiles)
    def _():
        o_ref[...] = _dot(xb, w_ref[...])


def _matmul_rope(x, w, cos, sa, sb, tm, name):
    m, k = x.shape
    n = w.shape[1]
    tn = n // 3
    nb = cos.shape[0] // tm
    return pl.pallas_call(
        functools.partial(_mm_rope_body, n_rot_tiles=2),
        grid=(m // tm, 3),
        in_specs=[pl.BlockSpec((tm, k), lambda i, j: (i, 0)),
                  pl.BlockSpec((k, tn), lambda i, j: (0, j)),
                  pl.BlockSpec((tm, LANES), lambda i, j: (i % nb, 0)),
                  pl.BlockSpec((tm, LANES), lambda i, j: (i % nb, 0)),
                  pl.BlockSpec((tm, LANES), lambda i, j: (i % nb, 0))],
        out_specs=pl.BlockSpec((tm, tn), lambda i, j: (i, j)),
        out_shape=jax.ShapeDtypeStruct((m, n), F32),
        scratch_shapes=[pltpu.VMEM((tm, k), BF16)],
        compiler_params=_params("parallel", "arbitrary"),
        name=name,
    )(x, w, cos, sa, sb)


def _proj_ln_body(*refs, alpha, n_in):
    h_refs, w_refs = refs[:n_in], refs[n_in:2 * n_in]
    r_ref, g_ref, b_ref, o_ref = refs[2 * n_in:]
    m = _dot(h_refs[0][...].astype(BF16), w_refs[0][...])
    for h_ref, w_ref in zip(h_refs[1:], w_refs[1:]):
        m = m + _dot(h_ref[...].astype(BF16), w_ref[...])
    o_ref[...] = _layer_norm(alpha * r_ref[...] + m, g_ref[...], b_ref[...])


def _proj_ln(hs, ws, resid, g, b, alpha, tm, name):
    m, d = resid.shape
    row = lambda i: (i, 0)
    fix = lambda i: (0, 0)
    return pl.pallas_call(
        functools.partial(_proj_ln_body, alpha=alpha, n_in=len(hs)),
        grid=(m // tm,),
        in_specs=[pl.BlockSpec((tm, h.shape[1]), row) for h in hs]
                 + [pl.BlockSpec((w.shape[0], d), fix) for w in ws]
                 + [pl.BlockSpec((tm, d), row), pl.BlockSpec((1, d), fix), pl.BlockSpec((1, d), fix)],
        out_specs=pl.BlockSpec((tm, d), row),
        out_shape=jax.ShapeDtypeStruct((m, d), F32),
        compiler_params=_params("parallel"),
        name=name,
    )(*hs, *ws, resid, g, b)


def _merge_proj_ln_body(o0_ref, o1_ref, o2_ref, l0_ref, l1_ref, l2_ref, w_ref, r_ref, g_ref, b_ref, o_ref, *, alpha):
    l0, l1, l2 = l0_ref[...], l1_ref[...], l2_ref[...]
    mx = jnp.maximum(jnp.maximum(l0, l1), l2)
    e0, e1, e2 = jnp.exp(l0 - mx), jnp.exp(l1 - mx), jnp.exp(l2 - mx)
    o = (e0 * o0_ref[...] + e1 * o1_ref[...] + e2 * o2_ref[...]) / (e0 + e1 + e2)
    m = _dot(o.astype(BF16), w_ref[...])
    o_ref[...] = _layer_norm(alpha * r_ref[...] + m, g_ref[...], b_ref[...])


def _merge_proj_ln(os_, ls_, w, resid, g, b, alpha, tm, name):
    m, d = resid.shape
    k = w.shape[0]
    row = lambda i: (i, 0)
    fix = lambda i: (0, 0)
    return pl.pallas_call(
        functools.partial(_merge_proj_ln_body, alpha=alpha),
        grid=(m // tm,),
        in_specs=[pl.BlockSpec((tm, k), row)] * 6 + [pl.BlockSpec((k, d), fix), pl.BlockSpec((tm, d), row),
                                                     pl.BlockSpec((1, d), fix), pl.BlockSpec((1, d), fix)],
        out_specs=pl.BlockSpec((tm, d), row),
        out_shape=jax.ShapeDtypeStruct((m, d), F32),
        compiler_params=_params("parallel"),
        name=name,
    )(*os_, *ls_, w, resid, g, b)


def _ffn_ln_body(x_ref, w1_ref, w3_ref, w2_ref, g_ref, b_ref, o_ref, xb_ref, acc_ref, *, alpha):
    f = pl.program_id(1)

    @pl.when(f == 0)
    def _():
        xb_ref[...] = x_ref[...].astype(BF16)
        acc_ref[...] = jnp.zeros_like(acc_ref)

    xb = xb_ref[...]
    hid = _silu(_dot(xb, w1_ref[...])) * _dot(xb, w3_ref[...])
    acc_ref[...] += _dot(hid.astype(BF16), w2_ref[...])

    @pl.when(f == pl.num_programs(1) - 1)
    def _():
        o_ref[...] = _layer_norm(alpha * x_ref[...] + acc_ref[...], g_ref[...], b_ref[...])


def _ffn_ln(x, w1, w3, w2, g, b, alpha, tm, tf, name):
    m, d = x.shape
    ff = w1.shape[1]
    return pl.pallas_call(
        functools.partial(_ffn_ln_body, alpha=alpha),
        grid=(m // tm, ff // tf),
        in_specs=[pl.BlockSpec((tm, d), lambda i, f: (i, 0)),
                  pl.BlockSpec((d, tf), lambda i, f: (0, f)),
                  pl.BlockSpec((d, tf), lambda i, f: (0, f)),
                  pl.BlockSpec((tf, d), lambda i, f: (f, 0)),
                  pl.BlockSpec((1, d), lambda i, f: (0, 0)),
                  pl.BlockSpec((1, d), lambda i, f: (0, 0))],
        out_specs=pl.BlockSpec((tm, d), lambda i, f: (i, 0)),
        out_shape=jax.ShapeDtypeStruct((m, d), F32),
        scratch_shapes=[pltpu.VMEM((tm, d), BF16), pltpu.VMEM((tm, d), F32)],
        compiler_params=_params("parallel", "arbitrary"),
        name=name,
    )(x, w1, w3, w2, g, b)


def _ssd_body(xbc_ref, z_ref, dt_ref, conv0_ref, h0_ref, cw_ref, cb_ref, dtb_ref, aneg_ref, dsk_ref, nw_ref,
              y_ref, hn_ref, xpad_ref, *, q, valid_len, n_heads, d_inner):
    c = pl.program_id(1)
    conv_dim = xpad_ref.shape[1]
    gn = SSM_GROUPS * SSM_STATE
    heads_per_group = n_heads // SSM_GROUPS
    p = SSM_HEAD_DIM

    @pl.when(c == 0)
    def _():
        xpad_ref[0:SUBLANES, :] = jnp.zeros((SUBLANES, conv_dim), F32)
        xpad_ref[SUBLANES - (CONV_W - 1):SUBLANES, :] = conv0_ref[...]
        hn_ref[...] = h0_ref[...]

    x_raw = xbc_ref[...]
    xpad_ref[SUBLANES:SUBLANES + q, :] = x_raw
    cw = cw_ref[...]
    acc = cb_ref[...] + x_raw * cw[CONV_W - 1:CONV_W]
    for j in range(CONV_W - 1):
        off = SUBLANES - (CONV_W - 1) + j
        acc = acc + xpad_ref[off:off + q, :] * cw[j:j + 1]
    xpad_ref[SUBLANES - (CONV_W - 1):SUBLANES, :] = x_raw[q - (CONV_W - 1):q, :]
    xc = _silu(acc)
    xs = xc[:, :d_inner]
    bm = xc[:, d_inner:d_inner + gn].astype(BF16)
    cm = xc[:, d_inner + gn:d_inner + 2 * gn].astype(BF16)

    t = dt_ref[...] + dtb_ref[...]
    dtv = jnp.maximum(t, 0.0) + jnp.log(1.0 + jnp.exp(-jnp.abs(t)))
    rows = lax.broadcasted_iota(I32, (q, q), 0)
    cols = lax.broadcasted_iota(I32, (q, q), 1)
    if valid_len < q:
        dtv = jnp.where(lax.broadcasted_iota(I32, dtv.shape, 0) < valid_len, dtv, 0.0)
    a = dtv * aneg_ref[...]
    causal = cols <= rows
    tri = jnp.where(causal, 1.0, 0.0).astype(BF16)
    a_hi, a_mid, a_lo = _split3(a)
    a_cum = _dot(tri, a_hi) + _dot(tri, a_mid) + _dot(tri, a_lo)
    a_cum_t = a_cum.T
    a_tot = a_cum[q - 1:q, :]

    ys = []
    xws = []
    for g in range(SSM_GROUPS):
        cg = cm[:, g * SSM_STATE:(g + 1) * SSM_STATE]
        bg = bm[:, g * SSM_STATE:(g + 1) * SSM_STATE]
        scores = _dot_nt(cg, bg)
        for r in range(heads_per_group):
            h = g * heads_per_group + r
            col = a_cum[:, h:h + 1]
            row = a_cum_t[h:h + 1, :]
            decay_ls = jnp.where(causal, jnp.exp(col - row), 0.0)
            w_diag = (scores * decay_ls).astype(BF16)
            xs_h = xs[:, h * p:(h + 1) * p]
            xdt = xs_h * dtv[:, h:h + 1]
            y_diag = _dot(w_diag, xdt.astype(BF16))
            y_off = _dot_nt(cg, hn_ref[h].astype(BF16)) * jnp.exp(col)
            ys.append(y_diag + y_off)
            xws.append(xdt * jnp.exp(a_tot[:, h:h + 1] - col))
    y = jnp.concatenate(ys, axis=1) + dsk_ref[...] * xs
    xw = jnp.concatenate(xws, axis=1)

    heads_per_tile = LANES // p
    for j in range(d_inner // LANES):
        xw_t = xw[:, j * LANES:(j + 1) * LANES].T.astype(BF16)
        for hh in range(heads_per_tile):
            h = j * heads_per_tile + hh
            g = h // heads_per_group
            bg = bm[:, g * SSM_STATE:(g + 1) * SSM_STATE]
            st = _dot(xw_t[hh * p:(hh + 1) * p, :], bg)
            hn_ref[h] = jnp.exp(a_tot[:, h:h + 1]) * hn_ref[h] + st

    zz = z_ref[...]
    yg = y * _silu(zz)
    gsz = d_inner // SSM_GROUPS
    outs = []
    for g in range(SSM_GROUPS):
        blk = yg[:, g * gsz:(g + 1) * gsz]
        ms = jnp.mean(blk * blk, axis=-1, keepdims=True)
        outs.append(blk * lax.rsqrt(ms + RMS_EPS))
    y_ref[...] = jnp.concatenate(outs, axis=1) * nw_ref[...]


def _ssd(proj3, conv0, h0, cw, cb, dtb, aneg, dsk, nw, valid_len, name):
    b, length, _ = proj3.shape
    n_heads, p, n = h0.shape[1], h0.shape[2], h0.shape[3]
    d_inner = n_heads * p
    conv_dim = cw.shape[1]
    q = SSD_CHUNK
    z_blk = conv_dim // d_inner
    dt_blk = (proj3.shape[2] - LANES) // LANES
    fix2 = lambda i, c: (0, 0)
    return pl.pallas_call(
        functools.partial(_ssd_body, q=q, valid_len=valid_len, n_heads=n_heads, d_inner=d_inner),
        grid=(b, length // q),
        in_specs=[pl.BlockSpec((None, q, conv_dim), lambda i, c: (i, c, 0)),
                  pl.BlockSpec((None, q, d_inner), lambda i, c: (i, c, z_blk)),
                  pl.BlockSpec((None, q, LANES), lambda i, c: (i, c, dt_blk)),
                  pl.BlockSpec((None, CONV_W - 1, conv_dim), lambda i, c: (i, 0, 0)),
                  pl.BlockSpec((None, n_heads, p, n), lambda i, c: (i, 0, 0, 0)),
                  pl.BlockSpec((CONV_W, conv_dim), fix2),
                  pl.BlockSpec((1, conv_dim), fix2),
                  pl.BlockSpec((1, LANES), fix2),
                  pl.BlockSpec((1, LANES), fix2),
                  pl.BlockSpec((1, d_inner), fix2),
                  pl.BlockSpec((1, d_inner), fix2)],
        out_specs=[pl.BlockSpec((None, q, d_inner), lambda i, c: (i, c, 0)),
                   pl.BlockSpec((None, n_heads, p, n), lambda i, c: (i, 0, 0, 0))],
        out_shape=[jax.ShapeDtypeStruct((b, length, d_inner), F32),
                   jax.ShapeDtypeStruct((b, n_heads, p, n), F32)],
        scratch_shapes=[pltpu.VMEM((SUBLANES + q, conv_dim), F32)],
        compiler_params=_params("parallel", "arbitrary"),
        name=name,
    )(proj3, proj3, proj3, conv0, h0, cw, cb, dtb, aneg, dsk, nw)


LOG2E = math.log2(math.e)
SB_SEGMENTS = SUBLANES


def _sb_weights(z, carry, valid, f_ref):
    tk, tq = z.shape
    seglen = tk // SB_SEGMENTS
    fail = 1.0 / (1.0 + jnp.exp2(z))
    if valid is not None:
        fail = jnp.where(valid, fail, 1.0)
    f_ref[...] = fail
    seg_tot = jnp.ones((SB_SEGMENTS, tq), F32)
    for v in range(seglen):
        seg_tot = seg_tot * f_ref[v * SB_SEGMENTS:(v + 1) * SB_SEGMENTS, :]
    offs = [None] * SB_SEGMENTS
    tot = carry
    for s in reversed(range(SB_SEGMENTS)):
        offs[s] = tot
        tot = tot * seg_tot[s:s + 1, :]
    run = jnp.concatenate(offs, axis=0)
    rows = [None] * seglen
    for v in reversed(range(seglen)):
        nxt = run * f_ref[v * SB_SEGMENTS:(v + 1) * SB_SEGMENTS, :]
        rows[v] = run - nxt
        run = nxt
    return jnp.concatenate(rows, axis=0), tot


def _sb_prompt_body(bias_ref, q_ref, k_ref, v_ref, o_ref, kp_ref, vt_ref, f_ref, z_ref, w_ref, acc_ref,
                    *, tq, hb, d):
    i = pl.program_id(2)
    h0 = pl.program_id(1) * hb
    seglen = tq // SB_SEGMENTS
    nkb = kp_ref.shape[0]
    r = lax.broadcasted_iota(I32, (tq, tq), 0)
    key_pos = (r % SB_SEGMENTS) * seglen + r // SB_SEGMENTS
    strictly_before = key_pos < lax.broadcasted_iota(I32, (tq, tq), 1)

    @pl.when(i == 0)
    def _():
        def permute(kb, c):
            base = pl.multiple_of(kb * tq, tq)
            kp = [k_ref[pl.ds(base + v, SB_SEGMENTS, stride=seglen), :] for v in range(seglen)]
            vp = [v_ref[pl.ds(base + v, SB_SEGMENTS, stride=seglen), :] for v in range(seglen)]
            kp_ref[kb] = jnp.concatenate(kp, axis=0).astype(BF16)
            vt_ref[kb] = jnp.concatenate(vp, axis=0).T.astype(BF16)
            return c

        lax.fori_loop(0, nkb, permute, 0)

    qt_all = (q_ref[...] * (d ** -0.5 * LOG2E)).T
    head_of_row = lax.broadcasted_iota(I32, qt_all.shape, 0) // d
    qts = [jnp.where(head_of_row == hh, qt_all, 0.0).astype(BF16) for hh in range(hb)]

    def logits(hh, kb):
        return _dot(kp_ref[kb], qts[hh]) + bias_ref[h0 + hh] * LOG2E

    def values(hh, kb):
        return vt_ref[kb, hh * d:(hh + 1) * d, :]

    carries = []
    for hh in range(hb):
        w, carry = _sb_weights(logits(hh, i), jnp.ones((1, tq), F32), strictly_before, f_ref.at[hh])
        w_ref[hh] = w.astype(BF16)
        z_ref[hh] = logits(hh, jnp.maximum(i - 1, 0))
        acc_ref[hh] = jnp.zeros((d, tq), F32)
        carries.append(carry)

    def step(it, carries):
        kb = i - 1 - it
        out = []
        for hh in range(hb):
            acc_ref[hh] += _dot(values(hh, kb + 1), w_ref[hh])
            z = z_ref[hh]
            z_ref[hh] = logits(hh, jnp.maximum(kb - 1, 0))
            w, carry = _sb_weights(z, carries[hh], None, f_ref.at[hh])
            w_ref[hh] = w.astype(BF16)
            out.append(carry)
        return tuple(out)

    lax.fori_loop(0, i, step, tuple(carries))
    out_t = [acc_ref[hh] + _dot(values(hh, 0), w_ref[hh]) for hh in range(hb)]
    o_ref[...] = jnp.concatenate(out_t, axis=0).T


def _sb_prompt(bias, proj3, q_col, k_col, v_col, n_heads, d, tq, name):
    b, length, _ = proj3.shape
    hb = LANES // d
    nkb = length // tq
    qc, kc, vc = q_col // LANES, k_col // LANES, v_col // LANES
    return pl.pallas_call(
        functools.partial(_sb_prompt_body, tq=tq, hb=hb, d=d),
        grid=(b, n_heads // hb, nkb),
        in_specs=[pl.BlockSpec(memory_space=pltpu.SMEM),
                  pl.BlockSpec((None, tq, LANES), lambda bi, hi, i: (bi, i, qc + hi)),
                  pl.BlockSpec((None, length, LANES), lambda bi, hi, i: (bi, 0, kc + hi)),
                  pl.BlockSpec((None, length, LANES), lambda bi, hi, i: (bi, 0, vc + hi))],
        out_specs=pl.BlockSpec((None, tq, LANES), lambda bi, hi, i: (bi, i, hi)),
        out_shape=jax.ShapeDtypeStruct((b, length, n_heads * d), F32),
        scratch_shapes=[pltpu.VMEM((nkb, tq, LANES), BF16), pltpu.VMEM((nkb, LANES, tq), BF16),
                        pltpu.VMEM((hb, tq, tq), F32), pltpu.VMEM((hb, tq, tq), F32),
                        pltpu.VMEM((hb, tq, tq), BF16), pltpu.VMEM((hb, d, tq), F32)],
        compiler_params=_params("parallel", "parallel", "arbitrary"),
        name=name,
    )(bias, proj3, proj3, proj3)


def _sb_sample_body(pt_ref, qbd_ref, bias_ref, knew_ref, vnew_ref, *refs, pages_per_step, n_new):
    k_refs = refs[:pages_per_step]
    v_refs = refs[pages_per_step:2 * pages_per_step]
    o_ref = refs[2 * pages_per_step]
    carry_ref = refs[2 * pages_per_step + 1]
    j = pl.program_id(1)
    qbd = qbd_ref[...]
    bias = bias_ref[...]
    nr, hd = qbd.shape
    tk = knew_ref.shape[-1]
    kk = lax.broadcasted_iota(I32, (tk, tk), 0)
    ss = lax.broadcasted_iota(I32, (tk, tk), 1)
    later = jnp.where(kk > ss, 1.0, 0.0).astype(BF16)

    def blocks(kts, vts, valid):
        n = len(kts)
        kcat = jnp.concatenate([kt.reshape(hd, tk).astype(BF16) for kt in kts], axis=1)
        vcat = jnp.concatenate([vt.reshape(hd, tk).astype(BF16) for vt in vts], axis=1)
        z = _dot(qbd, kcat) + bias
        sp = jnp.maximum(z, 0.0) + jnp.log(1.0 + jnp.exp(-jnp.abs(z)))
        log_fail = -sp
        log_beta = z - sp
        if valid is not None:
            log_fail = jnp.where(valid, log_fail, 0.0)
        lf_hi = log_fail.astype(BF16)
        lf_lo = (log_fail - lf_hi.astype(F32)).astype(BF16)
        stack = lambda a: jnp.concatenate([a[:, p * tk:(p + 1) * tk] for p in range(n)], axis=0)
        inner = _dot(stack(lf_hi), later) + _dot(stack(lf_lo), later)
        carry = carry_ref[...]
        tails = []
        for p in range(n):
            tails.append(inner[p * nr:(p + 1) * nr] + carry)
            carry = carry + jnp.sum(log_fail[:, p * tk:(p + 1) * tk], axis=-1, keepdims=True)
        carry_ref[...] = carry
        w = jnp.exp(log_beta + jnp.concatenate(tails, axis=1))
        if valid is not None:
            w = jnp.where(valid, w, 0.0)
        o_ref[...] += _dot_nt(w.astype(BF16), vcat)

    @pl.when(j == 0)
    def _():
        carry_ref[...] = jnp.zeros_like(carry_ref)
        o_ref[...] = jnp.zeros_like(o_ref)
        qi = lax.broadcasted_iota(I32, (nr, tk), 0) % n_new
        jn = lax.broadcasted_iota(I32, (nr, tk), 1)
        blocks([knew_ref[...]], [vnew_ref[...]], jn < qi)

    blocks([k_ref[...] for k_ref in k_refs], [v_ref[...] for v_ref in v_refs], None)


def _sb_sample(page_table, qbd, bias_rows, knew_t, vnew_t, cache_kt, cache_vt, pages_per_step, n_new, name):
    n_seq, n_pages = page_table.shape
    nr, hd = qbd.shape[1], qbd.shape[2]
    n_heads, d, page = cache_kt.shape[1], cache_kt.shape[2], cache_kt.shape[3]
    steps = n_pages // pages_per_step

    def page_spec(p):
        return pl.BlockSpec((None, n_heads, d, page),
                            lambda s, j, pt: (pt[s, n_pages - 1 - (j * pages_per_step + p)], 0, 0, 0))

    seq3 = lambda s, j, pt: (s, 0, 0)
    seq4 = lambda s, j, pt: (s, 0, 0, 0)
    grid_spec = pltpu.PrefetchScalarGridSpec(
        num_scalar_prefetch=1,
        grid=(n_seq, steps),
        in_specs=[pl.BlockSpec((None, nr, hd), seq3),
                  pl.BlockSpec((nr, 1), lambda s, j, pt: (0, 0)),
                  pl.BlockSpec((None, n_heads, d, page), seq4),
                  pl.BlockSpec((None, n_heads, d, page), seq4)]
                 + [page_spec(p) for p in range(pages_per_step)] * 2,
        out_specs=pl.BlockSpec((None, nr, hd), seq3),
        scratch_shapes=[pltpu.VMEM((nr, 1), F32)],
    )
    return pl.pallas_call(
        functools.partial(_sb_sample_body, pages_per_step=pages_per_step, n_new=n_new),
        grid_spec=grid_spec,
        out_shape=jax.ShapeDtypeStruct((n_seq, nr, hd), F32),
        compiler_params=_params("parallel", "arbitrary"),
        name=name,
    )(page_table, qbd, bias_rows, knew_t, vnew_t, *([cache_kt] * pages_per_step), *([cache_vt] * pages_per_step))


DIL_BLOCK = 128


def _dil_unit(start, prev_in_block, prev_ok, dil, q_ref, kc_ref, kp_ref, vc_ref, vp_ref, og_ref, lg_ref, d, scale):
    t = DIL_BLOCK
    rows = q_ref.shape[0]
    span = t * dil

    def idx(s0):
        return pl.ds(s0, t) if dil == 1 else pl.ds(s0, t, stride=dil)

    first_slot = lax.broadcasted_iota(I32, (t, LANES), 1) < d
    u = lax.broadcasted_iota(I32, (2 * t, 2 * t), 0) % t
    c = lax.broadcasted_iota(I32, (2 * t, 2 * t), 1)
    band_prev = jnp.logical_and(c < t, c >= u)
    band_cur = jnp.logical_and(c >= t, c - t <= u)
    if prev_ok is not None:
        band_prev = jnp.logical_and(band_prev, prev_ok)

    q2 = q_ref[idx(start), :] * scale
    if prev_in_block:
        k_prev, v_prev = kc_ref[idx(start - span), :], vc_ref[idx(start - span), :]
    else:
        k_prev, v_prev = kp_ref[idx(start + rows - span), :], vp_ref[idx(start + rows - span), :]
    qq = jnp.concatenate([jnp.where(first_slot, q2, 0.0), jnp.where(first_slot, 0.0, q2)], axis=0)
    kk = jnp.concatenate([k_prev, kc_ref[idx(start), :]], axis=0).astype(BF16)
    vv = jnp.concatenate([v_prev, vc_ref[idx(start), :]], axis=0).astype(BF16)
    s = jnp.where(jnp.logical_or(band_prev, band_cur), _dot_nt(qq.astype(BF16), kk), -jnp.inf)
    mx = jnp.max(s, axis=-1, keepdims=True)
    p = jnp.exp(s - mx)
    den = jnp.sum(p, axis=-1, keepdims=True)
    on = _dot(p.astype(BF16), vv) / den
    lse = jnp.broadcast_to(mx + jnp.log(den), on.shape)
    og_ref[idx(start), :] = jnp.where(first_slot, on[:t], on[t:])
    lg_ref[idx(start), :] = jnp.where(first_slot, lse[:t], lse[t:])


def _pairwise_loop(count, fn):
    def pair(m, carry):
        fn(2 * m)
        fn(2 * m + 1)
        return carry

    if count // 2:
        lax.fori_loop(0, count // 2, pair, 0)
    if count % 2:
        fn(count - 1)


def _dil_prompt_body(*refs, dilations, d, scale):
    n_g = len(dilations)
    ins = refs[:5 * n_g]
    o_ref = refs[5 * n_g]
    og_refs = refs[5 * n_g + 1:5 * n_g + 1 + n_g]
    lg_refs = refs[5 * n_g + 1 + n_g:]
    j = pl.program_id(2)
    rows = o_ref.shape[0]

    for g, dil in enumerate(dilations):
        unit = functools.partial(_dil_unit, dil=dil, q_ref=ins[5 * g], kc_ref=ins[5 * g + 1], kp_ref=ins[5 * g + 2],
                                 vc_ref=ins[5 * g + 3], vp_ref=ins[5 * g + 4], og_ref=og_refs[g],
                                 lg_ref=lg_refs[g], d=d, scale=scale)
        span = DIL_BLOCK * dil

        def first(r, unit=unit):
            unit(r, False, j > 0)

        def rest(n, unit=unit, dil=dil, span=span):
            start = n % dil + (n // dil + 1) * span
            if dil == 1:
                start = pl.multiple_of(start, DIL_BLOCK)
            unit(start, True, None)

        _pairwise_loop(dil, first)
        _pairwise_loop((rows // span - 1) * dil, rest)

    ls_ = [lg_ref[...] for lg_ref in lg_refs]
    mx = functools.reduce(jnp.maximum, ls_)
    es = [jnp.exp(l - mx) for l in ls_]
    num = functools.reduce(lambda a, b: a + b, [e * og_ref[...] for e, og_ref in zip(es, og_refs)])
    o_ref[...] = num / functools.reduce(lambda a, b: a + b, es)


def _dil_prompt(qkv3, dilations, n_slots, d, name):
    b, length, _ = qkv3.shape
    n_g = len(dilations)
    rows = DIL_BLOCK * max(dilations)
    sb = LANES // d
    gw = n_g * n_slots * d
    cur = lambda bi, si, j, c0: (bi, j, c0 + si)
    prev = lambda bi, si, j, c0: (bi, jnp.maximum(j - 1, 0), c0 + si)
    in_specs = []
    for g in range(n_g):
        qc, kc, vc = (g * n_slots * d) // LANES, (gw + g * n_slots * d) // LANES, (2 * gw + g * n_slots * d) // LANES
        for fn, c0 in ((cur, qc), (cur, kc), (prev, kc), (cur, vc), (prev, vc)):
            in_specs.append(pl.BlockSpec((None, rows, LANES), functools.partial(fn, c0=c0)))
    return pl.pallas_call(
        functools.partial(_dil_prompt_body, dilations=tuple(dilations), d=d, scale=d ** -0.5),
        grid=(b, n_slots // sb, length // rows),
        in_specs=in_specs,
        out_specs=pl.BlockSpec((None, rows, LANES), lambda bi, si, j: (bi, j, si)),
        out_shape=jax.ShapeDtypeStruct((b, length, n_slots * d), F32),
        scratch_shapes=[pltpu.VMEM((rows, LANES), F32)] * (2 * n_g),
        compiler_params=_params("parallel", "parallel", "arbitrary"),
        name=name,
    )(*([qkv3] * len(in_specs)))


def _dil_sample_body(qbd_ref, kbuf_ref, vbuf_ref, knew_ref, vnew_ref, o_ref, l_ref, kout_ref, vout_ref,
                     *, dil, n_new):
    n_slots, d, w = kbuf_ref.shape
    hd = n_slots * d
    nr = qbd_ref.shape[0]
    pad = knew_ref.shape[1]
    kb = kbuf_ref[...].reshape(hd, w)
    vb = vbuf_ref[...].reshape(hd, w)
    kn = knew_ref[...]
    vn = vnew_ref[...]
    tail_lane = lax.broadcasted_iota(I32, (hd, pad), 1) >= pad - n_new
    for src, new, out_ref in ((kb, kn, kout_ref), (vb, vn, vout_ref)):
        slid = pltpu.roll(src, w - n_new, 1)
        last = jnp.where(tail_lane, pltpu.roll(new, pad - n_new, 1), slid[:, w - pad:])
        if w > pad:
            out_ref[:, :, 0:w - pad] = slid[:, 0:w - pad].reshape(n_slots, d, w - pad)
        out_ref[:, :, w - pad:w] = last.reshape(n_slots, d, pad)

    qbd = qbd_ref[...]
    qi = lax.broadcasted_iota(I32, (nr, w), 0) % n_new
    diff = w + qi - lax.broadcasted_iota(I32, (nr, w), 1)
    ok_buf = jnp.logical_and(diff % dil == 0, diff <= dil * DIL_BLOCK)
    qi_n = lax.broadcasted_iota(I32, (nr, pad), 0) % n_new
    jn = lax.broadcasted_iota(I32, (nr, pad), 1)
    diff_n = qi_n - jn
    ok_new = jnp.logical_and(jnp.logical_and(diff_n >= 0, diff_n % dil == 0), jn < n_new)
    s_buf = jnp.where(ok_buf, _dot(qbd, kb.astype(BF16)), -jnp.inf)
    s_new = jnp.where(ok_new, _dot(qbd, kn.astype(BF16)), -jnp.inf)
    mx = jnp.maximum(jnp.max(s_buf, axis=-1, keepdims=True), jnp.max(s_new, axis=-1, keepdims=True))
    p_buf = jnp.exp(s_buf - mx)
    p_new = jnp.exp(s_new - mx)
    den = jnp.sum(p_buf, axis=-1, keepdims=True) + jnp.sum(p_new, axis=-1, keepdims=True)
    acc = _dot_nt(p_buf.astype(BF16), vb.astype(BF16)) + _dot_nt(p_new.astype(BF16), vn.astype(BF16))
    o_ref[...] = acc / den
    l_ref[...] = jnp.broadcast_to(mx + jnp.log(den), l_ref.shape)


def _dil_sample(qbd, kbuf_t, vbuf_t, knew_t, vnew_t, dil, n_new, name):
    n_seq, n_slots, d, w = kbuf_t.shape
    nr, hd = qbd.shape[1], qbd.shape[2]
    pad = knew_t.shape[2]
    s3 = lambda s: (s, 0, 0)
    s4 = lambda s: (s, 0, 0, 0)
    buf_spec = pl.BlockSpec((None, n_slots, d, w), s4)
    return pl.pallas_call(
        functools.partial(_dil_sample_body, dil=dil, n_new=n_new),
        grid=(n_seq,),
        in_specs=[pl.BlockSpec((None, nr, hd), s3), buf_spec, buf_spec,
                  pl.BlockSpec((None, hd, pad), s3), pl.BlockSpec((None, hd, pad), s3)],
        out_specs=[pl.BlockSpec((None, nr, hd), s3), pl.BlockSpec((None, nr, LANES), s3), buf_spec, buf_spec],
        out_shape=[jax.ShapeDtypeStruct((n_seq, nr, hd), F32), jax.ShapeDtypeStruct((n_seq, nr, LANES), F32),
                   jax.ShapeDtypeStruct(kbuf_t.shape, F32), jax.ShapeDtypeStruct(kbuf_t.shape, F32)],
        compiler_params=_params("parallel"),
        name=name,
    )(qbd, kbuf_t, vbuf_t, knew_t, vnew_t)


def _router_body(x_ref, w_ref, idx_ref, gate_ref, *, n_experts):
    xh, xm, _ = _split3(x_ref[...])
    wh, wm, _ = _split3(w_ref[...])
    logits = _dot(xh, wh) + _dot(xm, wh) + _dot(xh, wm)
    lane = lax.broadcasted_iota(I32, logits.shape, 1).astype(F32)
    logits = jnp.where(lane < n_experts, logits, -jnp.inf)
    m1 = jnp.max(logits, axis=-1, keepdims=True)
    i1 = jnp.min(jnp.where(logits == m1, lane, float(LANES)), axis=-1, keepdims=True)
    rest = jnp.where(lane == i1, -jnp.inf, logits)
    m2 = jnp.max(rest, axis=-1, keepdims=True)
    i2 = jnp.min(jnp.where(rest == m2, lane, float(LANES)), axis=-1, keepdims=True)
    e = jnp.exp(m2 - m1)
    g1 = 1.0 / (1.0 + e)
    idx_ref[...] = jnp.where(lane == 0.0, i1, jnp.where(lane == 1.0, i2, 0.0)).astype(I32)
    gate_ref[...] = jnp.where(lane == 0.0, g1, jnp.where(lane == 1.0, e * g1, 0.0))


def _router(x, w_pad, n_experts, tm, name):
    m, d = x.shape
    return pl.pallas_call(
        functools.partial(_router_body, n_experts=n_experts),
        grid=(m // tm,),
        in_specs=[pl.BlockSpec((tm, d), lambda i: (i, 0)), pl.BlockSpec((d, LANES), lambda i: (0, 0))],
        out_specs=[pl.BlockSpec((tm, LANES), lambda i: (i, 0))] * 2,
        out_shape=[jax.ShapeDtypeStruct((m, LANES), I32), jax.ShapeDtypeStruct((m, LANES), F32)],
        compiler_params=_params("parallel"),
        name=name,
    )(x, w_pad)


DMA_ISSUE_UNROLL = 8


def _row_copy(src_hbm, row, dst, r, sem):
    return pltpu.make_async_copy(src_hbm.at[pl.ds(row, 1)], dst.at[pl.ds(r, 1)], sem)


def _gather_body(src_ref, x_hbm, o_ref, buf, sem, *, rows):
    i = pl.program_id(0)

    def start(step):
        slot = step % 2

        def some(m, c):
            for k in range(DMA_ISSUE_UNROLL):
                r = m * DMA_ISSUE_UNROLL + k
                _row_copy(x_hbm, src_ref[step * rows + r], buf.at[slot], r, sem.at[slot]).start(priority=k % 2)
            return c

        lax.fori_loop(0, rows // DMA_ISSUE_UNROLL, some, 0)

    @pl.when(i == 0)
    def _():
        start(i)

    @pl.when(i + 1 < pl.num_programs(0))
    def _():
        start(i + 1)

    slot = i % 2
    pltpu.make_async_copy(x_hbm.at[pl.ds(0, rows)], buf.at[slot], sem.at[slot]).wait()
    o_ref[...] = buf[slot].astype(BF16)


def _gather_rows(src, x, rows, name):
    n = src.shape[0]
    d = x.shape[1]
    grid_spec = pltpu.PrefetchScalarGridSpec(
        num_scalar_prefetch=1,
        grid=(n // rows,),
        in_specs=[pl.BlockSpec(memory_space=pl.ANY)],
        out_specs=pl.BlockSpec((rows, d), lambda i, s: (i, 0)),
        scratch_shapes=[pltpu.VMEM((2, rows, d), x.dtype), pltpu.SemaphoreType.DMA((2,))],
    )
    return pl.pallas_call(
        functools.partial(_gather_body, rows=rows),
        grid_spec=grid_spec,
        out_shape=jax.ShapeDtypeStruct((n, d), BF16),
        compiler_params=_params("arbitrary"),
        name=name,
    )(src, x)


def _experts_body(te_ref, tb_ref, nu_ref, x_ref, w1_ref, w3_ref, w2_ref, o_ref, acc_ref):
    i = pl.program_id(0)
    f = pl.program_id(1)

    @pl.when(i < nu_ref[0])
    def _():
        @pl.when(f == 0)
        def _():
            acc_ref[...] = jnp.zeros_like(acc_ref)

        x = x_ref[...]
        hid = _silu(_dot(x, w1_ref[...].astype(BF16))) * _dot(x, w3_ref[...].astype(BF16))
        acc_ref[...] += _dot(hid.astype(BF16), w2_ref[...].astype(BF16))

        @pl.when(f == pl.num_programs(1) - 1)
        def _():
            o_ref[...] = acc_ref[...]

    @pl.when(jnp.logical_and(i >= nu_ref[0], f == pl.num_programs(1) - 1))
    def _():
        o_ref[...] = jnp.zeros_like(o_ref)


def _experts(tile_expert, tile_block, n_used, xs, w1, w3, w2, tm, tf, name):
    n_tiles = tile_expert.shape[0]
    d = xs.shape[1]
    ff = w1.shape[2]
    nf = ff // tf

    def fcol(i, f, nu):
        return jnp.where(i < nu[0], f, nf - 1)

    grid_spec = pltpu.PrefetchScalarGridSpec(
        num_scalar_prefetch=3,
        grid=(n_tiles, nf),
        in_specs=[pl.BlockSpec((tm, d), lambda i, f, te, tb, nu: (tb[i], 0)),
                  pl.BlockSpec((None, d, tf), lambda i, f, te, tb, nu: (te[i], 0, fcol(i, f, nu))),
                  pl.BlockSpec((None, d, tf), lambda i, f, te, tb, nu: (te[i], 0, fcol(i, f, nu))),
                  pl.BlockSpec((None, tf, d), lambda i, f, te, tb, nu: (te[i], fcol(i, f, nu), 0))],
        out_specs=pl.BlockSpec((tm, d), lambda i, f, te, tb, nu: (i, 0)),
        scratch_shapes=[pltpu.VMEM((tm, d), F32)],
    )
    return pl.pallas_call(
        _experts_body,
        grid_spec=grid_spec,
        out_shape=jax.ShapeDtypeStruct((n_tiles * tm, d), F32),
        compiler_params=_params("arbitrary", "arbitrary"),
        name=name,
    )(tile_expert, tile_block, n_used, xs, w1, w3, w2)


def _combine_ln_body(slot_ref, y_hbm, x_ref, gate_ref, g_ref, b_ref, o_ref, buf1, buf2, sem, *, rows, t0, alpha):
    base = (pl.program_id(0) * rows + t0) * TOP_K

    def start(m, c):
        for k in range(DMA_ISSUE_UNROLL):
            r = m * DMA_ISSUE_UNROLL + k
            _row_copy(y_hbm, slot_ref[base + TOP_K * r], buf1, r, sem).start(priority=0)
            _row_copy(y_hbm, slot_ref[base + TOP_K * r + 1], buf2, r, sem).start(priority=1)
        return c

    lax.fori_loop(0, rows // DMA_ISSUE_UNROLL, start, 0)
    pltpu.make_async_copy(y_hbm.at[pl.ds(0, rows)], buf1, sem).wait()
    pltpu.make_async_copy(y_hbm.at[pl.ds(0, rows)], buf2, sem).wait()
    gate = gate_ref[...]
    moe = gate[:, 0:1] * buf1[...] + gate[:, 1:2] * buf2[...]
    o_ref[...] = _layer_norm(alpha * x_ref[...] + moe, g_ref[...], b_ref[...])


def _combine_ln(slot, ys, x, gate, g, b, t0, alpha, rows, name):
    m, d = x.shape
    gb = t0 // rows
    grid_spec = pltpu.PrefetchScalarGridSpec(
        num_scalar_prefetch=1,
        grid=(m // rows,),
        in_specs=[pl.BlockSpec(memory_space=pl.ANY),
                  pl.BlockSpec((rows, d), lambda i, s: (i, 0)),
                  pl.BlockSpec((rows, LANES), lambda i, s: (i + gb, 0)),
                  pl.BlockSpec((1, d), lambda i, s: (0, 0)),
                  pl.BlockSpec((1, d), lambda i, s: (0, 0))],
        out_specs=pl.BlockSpec((rows, d), lambda i, s: (i, 0)),
        scratch_shapes=[pltpu.VMEM((rows, d), F32), pltpu.VMEM((rows, d), F32), pltpu.SemaphoreType.DMA(())],
    )
    return pl.pallas_call(
        functools.partial(_combine_ln_body, rows=rows, t0=t0, alpha=alpha),
        grid_spec=grid_spec,
        out_shape=jax.ShapeDtypeStruct((m, d), F32),
        compiler_params=_params("arbitrary"),
        name=name,
    )(slot, ys, x, gate, g, b)


def _moe_ln(xp, xs_, router_w, w1, w3, w2, g, b, alpha):
    n_experts = router_w.shape[1]
    d = xp.shape[1]
    tp, ts = xp.shape[0], xs_.shape[0]
    x_all = jnp.concatenate([xp, xs_], axis=0)
    t_all = tp + ts
    rw = jnp.pad(router_w, ((0, 0), (0, LANES - n_experts)))
    idx, gate = _router(x_all, rw, n_experts, ts, "moe_router")

    tm = 1024
    n_assign = t_all * TOP_K
    e_flat = idx[:, :TOP_K].reshape(n_assign)
    onehot = (e_flat[:, None] == jnp.arange(n_experts, dtype=I32)[None, :]).astype(I32)
    csum = jnp.cumsum(onehot, axis=0)
    pos = jnp.take_along_axis(csum, e_flat[:, None], axis=1)[:, 0] - 1
    counts = csum[-1]
    padded = ((counts + tm - 1) // tm) * tm
    ends = jnp.cumsum(padded)
    offs = ends - padded
    slot = (offs[e_flat] + pos).astype(I32)
    n_tiles = (n_assign + n_experts * (tm - 1)) // tm
    src = jnp.zeros((n_tiles * tm,), I32).at[slot].set(jnp.arange(n_assign, dtype=I32) // TOP_K)
    n_used = (ends[-1] // tm).astype(I32)
    tiles = jnp.arange(n_tiles, dtype=I32)
    t_exp = jnp.minimum(jnp.sum((tiles[:, None] * tm >= ends[None, :]).astype(I32), axis=1), n_experts - 1)
    last = jnp.maximum(n_used - 1, 0)
    tile_expert = jnp.where(tiles < n_used, t_exp, t_exp[last])
    tile_block = jnp.where(tiles < n_used, tiles, last)

    x_sorted = _gather_rows(src, x_all, 512, "moe_gather")
    y_sorted = _experts(tile_expert, tile_block, n_used.reshape(1), x_sorted, w1, w3, w2, tm, 512, "moe_experts")
    yp = _combine_ln(slot, y_sorted, xp, gate, g, b, 0, alpha, 256, "moe_combine_prompt")
    ys = _combine_ln(slot, y_sorted, xs_, gate, g, b, tp, alpha, 256, "moe_combine_sample")
    return yp, ys


def _rope_tables(pos):
    half = ROT_DIM // 2
    inv = ROPE_THETA ** (-jnp.arange(0, ROT_DIM, 2, dtype=F32) / ROT_DIM)
    ang = pos.astype(F32)[:, None] * inv[None, :]
    cos, sin = jnp.cos(ang), jnp.sin(ang)
    n = pos.shape[0]
    ones = jnp.ones((n, HEAD_DIM - ROT_DIM), F32)
    zeros = jnp.zeros((n, HEAD_DIM - ROT_DIM), F32)
    zh = jnp.zeros((n, half), F32)
    reps = LANES // HEAD_DIM
    c = jnp.tile(jnp.concatenate([cos, cos, ones], axis=1), (1, reps))
    sa = jnp.tile(jnp.concatenate([-sin, zh, zeros], axis=1), (1, reps))
    sb = jnp.tile(jnp.concatenate([zh, sin, zeros], axis=1), (1, reps))
    return c, sa, sb


def _block_diag_q(q, scale):
    n_seq, n_new, nh, d = q.shape
    eye = jnp.eye(nh, dtype=F32)
    out = jnp.einsum("sihd,hg->shigd", q * scale, eye)
    return out.reshape(n_seq, nh * n_new, nh * d).astype(BF16)


def _diag_blocks(o, n_new, nh, d):
    n_seq = o.shape[0]
    o5 = o.reshape(n_seq, nh, n_new, nh, d)
    sel = jnp.stack([o5[:, h, :, h, :] for h in range(nh)], axis=2)
    return sel.reshape(n_seq * n_new, nh * d)


def kernel(x_prompt, x_sample, cache_sb_k, cache_sb_v, page_table, state_ssm, state_conv, cache_c_w128_k, cache_c_w128_v, cache_c_w512_k, cache_c_w512_v, cache_c_w2048_k, cache_c_w2048_v, w_in_a, conv_w, conv_b, dt_bias, a_log, d_skip, ssm_norm_w, sb_bias, w_out_a, ln_a_mix_g, ln_a_mix_b, ffn_w1, ffn_w3, ffn_w2, ln_a_ffn_g, ln_a_ffn_b, w_in_c, w_out_c, ln_c_mix_g, ln_c_mix_b, router_w, moe_w1, moe_w3, moe_w2, ln_c_ffn_g, ln_c_ffn_b):
    n_pr, seq, d_model = x_prompt.shape
    n_dec, dec_seq, _ = x_sample.shape
    n_pages = page_table.shape[1]
    page = cache_sb_k.shape[2]
    past_len = n_pages * page
    ssm_heads, p, n_state = state_ssm.shape[2], state_ssm.shape[3], state_ssm.shape[4]
    d_inner = ssm_heads * p
    conv_dim = state_conv.shape[3]
    sb_heads = cache_sb_k.shape[3]
    sb_w = sb_heads * HEAD_DIM
    c_slots = cache_c_w128_k.shape[3]
    c_wd = c_slots * HEAD_DIM
    n_groups = len(C_GROUPS)
    depth = w_in_a.shape[0] + w_in_c.shape[0]
    alpha = (2 * depth) ** 0.25
    tp = n_pr * seq
    ts = n_dec * dec_seq
    win_k = (cache_c_w128_k, cache_c_w512_k, cache_c_w2048_k)
    win_v = (cache_c_w128_v, cache_c_w512_v, cache_c_w2048_v)
    row = lambda v: v.reshape(1, -1)

    yp = x_prompt.reshape(tp, d_model)
    ys = x_sample.reshape(ts, d_model)
    outs = {}

    i = 0
    w = w_in_a[i]
    c0, c1, c2 = d_inner, d_inner + conv_dim, d_inner + conv_dim + ssm_heads
    w_re = jnp.concatenate([w[:, c0:c1], w[:, :c0], w[:, c2:], w[:, c1:c2],
                            jnp.zeros((d_model, LANES - ssm_heads), F32)], axis=1).astype(BF16)
    o_q = conv_dim + d_inner
    proj_p = _matmul(yp, w_re, 512, "in_proj_a_prompt")
    proj_s = _matmul(ys, w_re, ts, "in_proj_a_sample")

    pad_l = LANES - ssm_heads
    ssd_par = (conv_w[i], row(conv_b[i]), row(jnp.pad(dt_bias[i], (0, pad_l))),
               row(jnp.pad(-jnp.exp(a_log[i]), (0, pad_l))), row(jnp.repeat(d_skip[i], p)), row(ssm_norm_w[i]))
    ya_p, ssm_p = _ssd(proj_p.reshape(n_pr, seq, -1), jnp.zeros((n_pr, CONV_W - 1, conv_dim), F32),
                       jnp.zeros((n_pr, ssm_heads, p, n_state), F32), *ssd_par, SSD_CHUNK, "ssd_prompt")
    proj_s3 = proj_s.reshape(n_dec, dec_seq, -1)
    proj_s_pad = jnp.pad(proj_s3, ((0, 0), (0, SSD_CHUNK - dec_seq), (0, 0)))
    ya_s, ssm_s = _ssd(proj_s_pad, state_conv[i], state_ssm[i], *ssd_par, dec_seq, "ssd_sample")
    ya_p = ya_p.reshape(tp, d_inner)
    ya_s = ya_s[:, :dec_seq].reshape(ts, d_inner)
    conv_p = proj_p.reshape(n_pr, seq, -1)[:, seq - (CONV_W - 1):, :conv_dim]
    conv_s = jnp.concatenate([state_conv[i], proj_s3[:, :, :conv_dim]], axis=1)[:, dec_seq:]

    k_p = proj_p[:, o_q + sb_w:o_q + 2 * sb_w].reshape(n_pr, seq, sb_heads, HEAD_DIM)
    v_p = proj_p[:, o_q + 2 * sb_w:o_q + 3 * sb_w].reshape(n_pr, seq, sb_heads, HEAD_DIM)
    o_p = _sb_prompt(sb_bias[i], proj_p.reshape(n_pr, seq, -1), o_q, o_q + sb_w, o_q + 2 * sb_w,
                     sb_heads, HEAD_DIM, 256, "sb_prompt").reshape(tp, sb_w)

    q_s = proj_s[:, o_q:o_q + sb_w].reshape(n_dec, dec_seq, sb_heads, HEAD_DIM)
    k_s = proj_s[:, o_q + sb_w:o_q + 2 * sb_w].reshape(n_dec, dec_seq, sb_w)
    v_s = proj_s[:, o_q + 2 * sb_w:o_q + 3 * sb_w].reshape(n_dec, dec_seq, sb_w)
    qbd = _block_diag_q(q_s, HEAD_DIM ** -0.5)
    bias_rows = jnp.repeat(sb_bias[i], dec_seq).reshape(sb_heads * dec_seq, 1)

    def keys_minor(a):
        return a.transpose(0, 2, 3, 1)

    pad_new = ((0, 0), (0, 0), (0, 0), (0, page - dec_seq))
    o_s = _sb_sample(page_table, qbd, bias_rows,
                     jnp.pad(keys_minor(k_s.reshape(n_dec, dec_seq, sb_heads, HEAD_DIM)), pad_new),
                     jnp.pad(keys_minor(v_s.reshape(n_dec, dec_seq, sb_heads, HEAD_DIM)), pad_new),
                     keys_minor(cache_sb_k[i]), keys_minor(cache_sb_v[i]), 16, dec_seq, "sb_sample")
    o_s = _diag_blocks(o_s, dec_seq, sb_heads, HEAD_DIM)

    wo = w_out_a[i].astype(BF16)
    ln_g, ln_b = row(ln_a_mix_g[i]), row(ln_a_mix_b[i])
    wo2 = [wo[:d_inner], wo[d_inner:]]
    yp = _proj_ln([ya_p, o_p], wo2, yp, ln_g, ln_b, alpha, 512, "out_proj_a_prompt")
    ys = _proj_ln([ya_s, o_s], wo2, ys, ln_g, ln_b, alpha, ts, "out_proj_a_sample")
    f1, f3, f2 = ffn_w1[i].astype(BF16), ffn_w3[i].astype(BF16), ffn_w2[i].astype(BF16)
    ln_g, ln_b = row(ln_a_ffn_g[i]), row(ln_a_ffn_b[i])
    d_ff = f1.shape[1]
    tf = d_ff // 2 if d_ff % (2 * LANES) == 0 else LANES
    yp = _ffn_ln(yp, f1, f3, f2, ln_g, ln_b, alpha, 512, tf, "ffn_prompt")
    ys = _ffn_ln(ys, f1, f3, f2, ln_g, ln_b, alpha, ts, tf, "ffn_sample")

    outs["sb_kp"] = k_p[None]
    outs["sb_vp"] = v_p[None]
    outs["sb_ks"] = k_s.reshape(1, n_dec, dec_seq, sb_heads, HEAD_DIM)
    outs["sb_vs"] = v_s.reshape(1, n_dec, dec_seq, sb_heads, HEAD_DIM)
    outs["ssm_p"], outs["ssm_s"], outs["conv_p"], outs["conv_s"] = ssm_p[None], ssm_s[None], conv_p[None], conv_s[None]

    wc = w_in_c[i].astype(BF16)
    gw = n_groups * c_wd
    cos_p, sa_p, sb_p = _rope_tables(jnp.arange(seq))
    cos_s, sa_s, sb_s = _rope_tables(past_len + jnp.arange(ts) % dec_seq)
    qkv_p = _matmul_rope(yp, wc, cos_p, sa_p, sb_p, 512, "in_proj_c_prompt")
    qkv_s = _matmul_rope(ys, wc, cos_s, sa_s, sb_s, ts, "in_proj_c_sample")

    scale = HEAD_DIM ** -0.5
    o_c_p = _dil_prompt(qkv_p.reshape(n_pr, seq, -1), [dil for _, dil in C_GROUPS], c_slots, HEAD_DIM,
                        "dilated_prompt").reshape(tp, c_wd)
    o_groups_s, l_groups_s = [], []
    for g, (win, dil) in enumerate(C_GROUPS):
        cq, ck, cv = g * c_wd, gw + g * c_wd, 2 * gw + g * c_wd
        keep = min(win, seq)
        last_rows = qkv_p.reshape(n_pr, seq, -1)[:, seq - keep:]
        outs[f"c_kp{g}"] = last_rows[:, :, ck:ck + c_wd].reshape(1, n_pr, keep, c_slots, HEAD_DIM)
        outs[f"c_vp{g}"] = last_rows[:, :, cv:cv + c_wd].reshape(1, n_pr, keep, c_slots, HEAD_DIM)

        qg_s = qkv_s[:, cq:cq + c_wd].reshape(n_dec, dec_seq, c_slots, HEAD_DIM)
        pad_keys = ((0, 0), (0, 0), (0, DIL_BLOCK - dec_seq))
        kn = jnp.pad(qkv_s[:, ck:ck + c_wd].reshape(n_dec, dec_seq, c_wd).transpose(0, 2, 1), pad_keys)
        vn = jnp.pad(qkv_s[:, cv:cv + c_wd].reshape(n_dec, dec_seq, c_wd).transpose(0, 2, 1), pad_keys)
        o_g, l_g, k_new, v_new = _dil_sample(_block_diag_q(qg_s, scale), keys_minor(win_k[g][i]),
                                             keys_minor(win_v[g][i]), kn, vn, dil, dec_seq,
                                             f"dilated_sample_{g}")
        o_groups_s.append(_diag_blocks(o_g, dec_seq, c_slots, HEAD_DIM))
        l_s = l_g[:, :, 0].reshape(n_dec, c_slots, dec_seq).transpose(0, 2, 1)
        l_groups_s.append(jnp.repeat(l_s.reshape(ts, c_slots), HEAD_DIM, axis=1))
        outs[f"c_ks{g}"] = k_new.transpose(0, 3, 1, 2)[None]
        outs[f"c_vs{g}"] = v_new.transpose(0, 3, 1, 2)[None]

    woc = w_out_c[i].astype(BF16)
    ln_g, ln_b = row(ln_c_mix_g[i]), row(ln_c_mix_b[i])
    yp = _proj_ln([o_c_p], [woc], yp, ln_g, ln_b, alpha, 512, "out_proj_c_prompt")
    ys = _merge_proj_ln(o_groups_s, l_groups_s, woc, ys, ln_g, ln_b, alpha, ts, "out_proj_c_sample")
    yp, ys = _moe_ln(yp, ys, router_w[i], moe_w1[i], moe_w3[i], moe_w2[i],
                     row(ln_c_ffn_g[i]), row(ln_c_ffn_b[i]), alpha)

    res = [yp.reshape(n_pr, seq, d_model), ys.reshape(n_dec, dec_seq, d_model),
           outs["sb_kp"], outs["sb_vp"], outs["sb_ks"], outs["sb_vs"],
           outs["ssm_p"], outs["ssm_s"], outs["conv_p"], outs["conv_s"]]
    for g in range(n_groups):
        res += [outs[f"c_kp{g}"], outs[f"c_vp{g}"], outs[f"c_ks{g}"], outs[f"c_vs{g}"]]
    return tuple(res)
```
